```python
import math
import jax, jax.numpy as jnp
from jax import lax
import numpy as np

D_MODEL = 1024
BATCH = 2
SEQ = 8192
DEPTH = 1
DEC_BATCH = 32
DEC_SEQ = 8
PAST_LEN = 16384
PAGE_SIZE = 128

HEAD_DIM = 64
ATT_WIDTH = D_MODEL // 2
N_HEADS = ATT_WIDTH // HEAD_DIM
CONV_WIDTH = D_MODEL - ATT_WIDTH
CONV_KERNEL = 31
MOBA_BLOCK = 256
MOBA_TOPK = 3
Q_BLOCK = 128
N_BUCKETS = 32
MAX_DISTANCE = 128
D_FF = ((8 * D_MODEL // 3 + 127) // 128) * 128
FFN_KERNEL = 3
EPS = 1e-6

kernel_name = 'hybrid_moba_conformer_convffn_step'


def rmsnorm(x, g):
    xf = x.astype(jnp.float32)
    y = xf * lax.rsqrt(jnp.mean(xf * xf, axis=-1, keepdims=True) + EPS)
    return (y * g.astype(jnp.float32)).astype(x.dtype)


def layernorm(x, g, b):
    xf = x.astype(jnp.float32)
    mu = jnp.mean(xf, axis=-1, keepdims=True)
    var = jnp.mean(jnp.square(xf - mu), axis=-1, keepdims=True)
    y = (xf - mu) * lax.rsqrt(var + EPS)
    return (y * g.astype(jnp.float32) + b.astype(jnp.float32)).astype(x.dtype)


def rel_bucket(dist):
    n = jnp.maximum(dist, 0)
    max_exact = N_BUCKETS // 2
    nf = jnp.maximum(n, 1).astype(jnp.float32)
    large = max_exact + (jnp.log(nf / max_exact) / math.log(MAX_DISTANCE / max_exact)
                         * (N_BUCKETS - max_exact)).astype(jnp.int32)
    large = jnp.minimum(large, N_BUCKETS - 1)
    return jnp.where(n < max_exact, n, large)


def causal_depthwise_conv(x, buf, w, b):
    xp = jnp.concatenate([buf.astype(x.dtype), x], axis=1)
    y = lax.conv_general_dilated(xp, w[:, None, :].astype(x.dtype), window_strides=(1,),
                                 padding='VALID', dimension_numbers=('NWC', 'WIO', 'NWC'),
                                 feature_group_count=x.shape[-1])
    return y + b.astype(x.dtype), xp[:, xp.shape[1] - (w.shape[0] - 1):]


def moba_attention(q, k, v, q_pos0, rel_bias):
    B, T, H, Dh = q.shape
    L = k.shape[1]
    n_blk = -(-L // MOBA_BLOCK)
    pad = n_blk * MOBA_BLOCK - L
    k = jnp.pad(k, ((0, 0), (0, pad), (0, 0), (0, 0)))
    v = jnp.pad(v, ((0, 0), (0, pad), (0, 0), (0, 0)))
    kb = k.reshape(B, n_blk, MOBA_BLOCK, H, Dh).transpose(0, 3, 1, 2, 4)
    vb = v.reshape(B, n_blk, MOBA_BLOCK, H, Dh).transpose(0, 3, 1, 2, 4)
    k_mean = jnp.mean(kb, axis=3, dtype=jnp.float32)
    n_sel = min(MOBA_TOPK, n_blk)
    qc = min(Q_BLOCK, T)
    n_qc = -(-T // qc)
    qp = jnp.pad(q, ((0, 0), (0, n_qc * qc - T), (0, 0), (0, 0)))
    q_chunks = qp.reshape(B, n_qc, qc, H, Dh).transpose(1, 0, 2, 3, 4)
    starts = q_pos0 + qc * jnp.arange(n_qc, dtype=jnp.int32)
    scale = HEAD_DIM ** -0.5
    bias_table = rel_bias.T.astype(jnp.float32)
    b_idx = jnp.arange(B)[:, None, None, None]
    h_idx = jnp.arange(H)[None, :, None, None]
    blk_ids = jnp.arange(n_blk)
    offs = jnp.arange(MOBA_BLOCK, dtype=jnp.int32)

    def one_chunk(args):
        qch, s = args
        qpos = s + jnp.arange(qc, dtype=jnp.int32)
        q_blk = qpos // MOBA_BLOCK
        gate = jnp.einsum('bqhd,bhnd->bhqn', qch.astype(jnp.float32), k_mean)
        gate = jnp.where(blk_ids[None, :] < q_blk[:, None], gate, -jnp.inf)
        _, sel = lax.top_k(gate, n_sel)
        sel_ok = sel < q_blk[None, None, :, None]
        k_sel = kb[b_idx, h_idx, sel]
        v_sel = vb[b_idx, h_idx, sel]
        s_sel = jnp.einsum('bqhd,bhqjkd->bhqjk', qch, k_sel,
                           preferred_element_type=jnp.float32) * scale
        kpos_sel = sel[..., None] * MOBA_BLOCK + offs
        bias_sel = bias_table[h_idx[..., None], rel_bucket(qpos[:, None, None] - kpos_sel)]
        s_sel = jnp.where(sel_ok[..., None], s_sel + bias_sel, -jnp.inf)
        own = s // MOBA_BLOCK
        k_own = lax.dynamic_index_in_dim(kb, own, axis=2, keepdims=False)
        v_own = lax.dynamic_index_in_dim(vb, own, axis=2, keepdims=False)
        dist_own = qpos[:, None] - (own * MOBA_BLOCK + offs)[None, :]
        s_own = jnp.einsum('bqhd,bhkd->bhqk', qch, k_own,
                           preferred_element_type=jnp.float32) * scale + bias_table[:, rel_bucket(dist_own)]
        s_own = jnp.where(dist_own >= 0, s_own, -jnp.inf)
        logits = jnp.concatenate([s_sel.reshape(B, H, qc, n_sel * MOBA_BLOCK), s_own], axis=-1)
        p = jax.nn.softmax(logits, axis=-1).astype(v.dtype)
        p_sel = p[..., :n_sel * MOBA_BLOCK].reshape(B, H, qc, n_sel, MOBA_BLOCK)
        p_own = p[..., n_sel * MOBA_BLOCK:]
        return (jnp.einsum('bhqjk,bhqjkd->bqhd', p_sel, v_sel)
                + jnp.einsum('bhqk,bhkd->bqhd', p_own, v_own))

    out = lax.map(one_chunk, (q_chunks, starts))
    return out.transpose(1, 0, 2, 3, 4).reshape(B, n_qc * qc, H, Dh)[:, :T]


def decoder_layer(x, k_past, v_past, conv_buf, ffn_buf, rel_bias, g_pre_mix, w_in, conv_dw_w,
                  conv_dw_b, conv_ln_g, conv_ln_b, w_out, g_post_mix, g_pre_ffn, w_ffn_up,
                  ffn_dw_w, ffn_dw_b, w_ffn_down, g_post_ffn):
    B, T, _ = x.shape
    pos0 = k_past.shape[1]
    h = rmsnorm(x, g_pre_mix)
    proj = h @ w_in.astype(x.dtype)
    q, k, v, ga, gb = jnp.split(proj, [ATT_WIDTH, 2 * ATT_WIDTH, 3 * ATT_WIDTH,
                                       3 * ATT_WIDTH + CONV_WIDTH], axis=-1)
    q = q.reshape(B, T, N_HEADS, HEAD_DIM)
    k = k.reshape(B, T, N_HEADS, HEAD_DIM)
    v = v.reshape(B, T, N_HEADS, HEAD_DIM)
    k_all = jnp.concatenate([k_past.astype(x.dtype), k], axis=1)
    v_all = jnp.concatenate([v_past.astype(x.dtype), v], axis=1)
    att = moba_attention(q, k_all, v_all, pos0, rel_bias).reshape(B, T, ATT_WIDTH)
    glu = ga * jax.nn.sigmoid(gb)
    cv, conv_new = causal_depthwise_conv(glu, conv_buf, conv_dw_w, conv_dw_b)
    cv = jax.nn.silu(layernorm(cv, conv_ln_g, conv_ln_b))
    mix = jnp.concatenate([att, cv], axis=-1) @ w_out.astype(x.dtype)
    x = x + rmsnorm(mix, g_post_mix)
    up = rmsnorm(x, g_pre_ffn) @ w_ffn_up.astype(x.dtype)
    up, ffn_new = causal_depthwise_conv(up, ffn_buf, ffn_dw_w, ffn_dw_b)
    gate, val = jnp.split(up, 2, axis=-1)
    f = (jax.nn.gelu(gate, approximate=True) * val) @ w_ffn_down.astype(x.dtype)
    x = x + rmsnorm(f, g_post_ffn)
    return x, k, v, conv_new, ffn_new


def setup_inputs(seed: int = 0) -> dict:
    key = jax.random.key(seed)
    ks = jax.random.split(key, 24)
    n_pages = PAST_LEN // PAGE_SIZE
    n_used = DEC_BATCH * n_pages
    n_pool = n_used + n_used // 4
    f32 = jnp.float32

    def nrm(k, shape, s):
        return jax.random.normal(k, shape, f32) * s

    def gain(k, shape):
        return 1.0 + 0.05 * jax.random.normal(k, shape, f32)

    mix_in = 3 * ATT_WIDTH + 2 * CONV_WIDTH
    page_table = jax.random.permutation(ks[4], n_pool)[:n_used].reshape(DEC_BATCH, n_pages).astype(jnp.int32)
    return {
        'x_prompt': nrm(ks[0], (BATCH, SEQ, D_MODEL), 1.0),
        'x_sample': nrm(ks[1], (DEC_BATCH, DEC_SEQ, D_MODEL), 1.0),
        'cache_k': nrm(ks[2], (DEPTH, n_pool, PAGE_SIZE, N_HEADS, HEAD_DIM), 1.0),
        'cache_v': nrm(ks[3], (DEPTH, n_pool, PAGE_SIZE, N_HEADS, HEAD_DIM), 1.0),
        'state_conv': nrm(ks[5], (DEPTH, DEC_BATCH, CONV_KERNEL - 1, CONV_WIDTH), 0.5),
        'state_ffn': nrm(ks[6], (DEPTH, DEC_BATCH, FFN_KERNEL - 1, 2 * D_FF), 1.0),
        'page_table': page_table,
        'rel_bias': nrm(ks[7], (N_BUCKETS, N_HEADS), 0.5),
        'g_pre_mix': gain(ks[8], (DEPTH, D_MODEL)),
        'w_in': nrm(ks[9], (DEPTH, D_MODEL, mix_in), D_MODEL ** -0.5),
        'conv_dw_w': nrm(ks[10], (DEPTH, CONV_KERNEL, CONV_WIDTH), CONV_KERNEL ** -0.5),
        'conv_dw_b': nrm(ks[11], (DEPTH, CONV_WIDTH), 0.02),
        'conv_ln_g': gain(ks[12], (DEPTH, CONV_WIDTH)),
        'conv_ln_b': nrm(ks[13], (DEPTH, CONV_WIDTH), 0.02),
        'w_out': nrm(ks[14], (DEPTH, ATT_WIDTH + CONV_WIDTH, D_MODEL), (ATT_WIDTH + CONV_WIDTH) ** -0.5),
        'g_post_mix': gain(ks[15], (DEPTH, D_MODEL)),
        'g_pre_ffn': gain(ks[16], (DEPTH, D_MODEL)),
        'w_ffn_up': nrm(ks[17], (DEPTH, D_MODEL, 2 * D_FF), D_MODEL ** -0.5),
        'ffn_dw_w': nrm(ks[18], (DEPTH, FFN_KERNEL, 2 * D_FF), FFN_KERNEL ** -0.5),
        'ffn_dw_b': nrm(ks[19], (DEPTH, 2 * D_FF), 0.02),
        'w_ffn_down': nrm(ks[20], (DEPTH, D_FF, D_MODEL), D_FF ** -0.5),
        'g_post_ffn': gain(ks[21], (DEPTH, D_MODEL)),
    }


def reference(x_prompt, x_sample, cache_k, cache_v, state_conv, state_ffn, page_table, rel_bias,
              g_pre_mix, w_in, conv_dw_w, conv_dw_b, conv_ln_g, conv_ln_b, w_out, g_post_mix,
              g_pre_ffn, w_ffn_up, ffn_dw_w, ffn_dw_b, w_ffn_down, g_post_ffn):
    bp = x_prompt.shape[0]
    bs = x_sample.shape[0]
    past_len = page_table.shape[1] * cache_k.shape[2]
    yp, ys = x_prompt, x_sample
    kp_l, vp_l, cp_l, fp_l, ks_l, vs_l, cs_l, fs_l = [], [], [], [], [], [], [], []
    for l in range(DEPTH):
        w = (g_pre_mix[l], w_in[l], conv_dw_w[l], conv_dw_b[l], conv_ln_g[l], conv_ln_b[l],
             w_out[l], g_post_mix[l], g_pre_ffn[l], w_ffn_up[l], ffn_dw_w[l], ffn_dw_b[l],
             w_ffn_down[l], g_post_ffn[l])
        empty = jnp.zeros((bp, 0, N_HEADS, HEAD_DIM), yp.dtype)
        conv0 = jnp.zeros((bp, CONV_KERNEL - 1, CONV_WIDTH), yp.dtype)
        ffn0 = jnp.zeros((bp, FFN_KERNEL - 1, 2 * D_FF), yp.dtype)
        yp, kp, vp, cp, fp = decoder_layer(yp, empty, empty, conv0, ffn0, rel_bias, *w)
        k_past = cache_k[l][page_table].reshape(bs, past_len, N_HEADS, HEAD_DIM)
        v_past = cache_v[l][page_table].reshape(bs, past_len, N_HEADS, HEAD_DIM)
        ys, ksm, vsm, csm, fsm = decoder_layer(ys, k_past, v_past, state_conv[l], state_ffn[l], rel_bias, *w)
        kp_l.append(kp); vp_l.append(vp); cp_l.append(cp); fp_l.append(fp)
        ks_l.append(ksm); vs_l.append(vsm); cs_l.append(csm); fs_l.append(fsm)
    return (yp, ys, jnp.stack(kp_l), jnp.stack(vp_l), jnp.stack(cp_l), jnp.stack(fp_l),
            jnp.stack(ks_l), jnp.stack(vs_l), jnp.stack(cs_l), jnp.stack(fs_l))
```

```python
import functools
import math

import numpy as np
import jax
import jax.numpy as jnp
from jax import lax
from jax.experimental import pallas as pl
from jax.experimental.pallas import tpu as pltpu

F32 = jnp.float32
BF16 = jnp.bfloat16

D_MODEL = 1024
HEAD_DIM = 64
ATT_WIDTH = 512
N_HEADS = 8
N_PAIRS = N_HEADS // 2
CONV_WIDTH = 512
CONV_KERNEL = 31
MOBA_BLOCK = 256
MOBA_TOPK = 3
N_BUCKETS = 32
MAX_DISTANCE = 128
D_FF = 2816
FFN_KERNEL = 3
EPS = 1e-6
SCALE = HEAD_DIM ** -0.5
NEG = -1e30
LANES = 128
SUBLANES = 8
VMEM_LIMIT = 56 * 1024 * 1024


def _bucket_thresholds():
    n = np.arange(0, 4 * MAX_DISTANCE)
    max_exact = N_BUCKETS // 2
    nf = np.maximum(n, 1).astype(np.float64)
    large = max_exact + (np.log(nf / max_exact) / math.log(MAX_DISTANCE / max_exact)
                         * (N_BUCKETS - max_exact)).astype(np.int64)
    bucket = np.where(n < max_exact, n, np.minimum(large, N_BUCKETS - 1))
    return tuple(int(np.argmax(bucket >= k)) for k in range(1, N_BUCKETS))


_BUCKET_THR = _bucket_thresholds()


def _rms(x, g):
    return x * lax.rsqrt(jnp.mean(x * x, axis=-1, keepdims=True) + EPS) * g


def _bias_kernel(tbl_ref, near_t_ref, near_s_ref):
    p = pl.program_id(0)

    def bias_of(dist, h):
        b = jnp.full(dist.shape, tbl_ref[0, h], F32)
        for k in range(1, N_BUCKETS):
            b = jnp.where(dist >= _BUCKET_THR[k - 1], tbl_ref[k, h], b)
        return jnp.where(dist >= 0, b, NEG)

    jj = lax.broadcasted_iota(jnp.int32, (2 * MOBA_BLOCK, MOBA_BLOCK), 0)
    ii = lax.broadcasted_iota(jnp.int32, (2 * MOBA_BLOCK, MOBA_BLOCK), 1)
    dist = MOBA_BLOCK + ii - jj
    for half in range(2):
        near_t_ref[0, :, half * MOBA_BLOCK:(half + 1) * MOBA_BLOCK] = bias_of(dist, 2 * p + half)

    @pl.when(p == 0)
    def _():
        tt = lax.broadcasted_iota(jnp.int32, (SUBLANES, 2 * MOBA_BLOCK), 0)
        j2 = lax.broadcasted_iota(jnp.int32, (SUBLANES, 2 * MOBA_BLOCK), 1)
        d2 = MOBA_BLOCK + tt - j2
        for h in range(N_HEADS):
            near_s_ref[h * SUBLANES:(h + 1) * SUBLANES, :] = bias_of(d2, h)


def _bias_tiles(rel_bias):
    return pl.pallas_call(
        _bias_kernel,
        grid=(N_PAIRS,),
        in_specs=[pl.BlockSpec(memory_space=pltpu.SMEM)],
        out_specs=[pl.BlockSpec((1, 2 * MOBA_BLOCK, 2 * MOBA_BLOCK), lambda p: (p, 0, 0)),
                   pl.BlockSpec((N_HEADS * SUBLANES, 2 * MOBA_BLOCK), lambda p: (0, 0))],
        out_shape=[jax.ShapeDtypeStruct((N_PAIRS, 2 * MOBA_BLOCK, 2 * MOBA_BLOCK), F32),
                   jax.ShapeDtypeStruct((N_HEADS * SUBLANES, 2 * MOBA_BLOCK), F32)],
        compiler_params=pltpu.CompilerParams(dimension_semantics=("arbitrary",)),
        name="bias_tiles",
    )(rel_bias)


def _proj_kernel(x_ref, g_ref, w_ref, *out_refs, prompt, tm):
    h = _rms(x_ref[0], g_ref[...])
    p = jnp.dot(h.astype(BF16), w_ref[...], preferred_element_type=F32)
    q = p[:, 0:ATT_WIDTH]
    k = p[:, ATT_WIDTH:2 * ATT_WIDTH]
    v = p[:, 2 * ATT_WIDTH:3 * ATT_WIDTH]
    ga = p[:, 3 * ATT_WIDTH:3 * ATT_WIDTH + CONV_WIDTH]
    gb = p[:, 3 * ATT_WIDTH + CONV_WIDTH:]
    glu = ga * jax.nn.sigmoid(gb)
    if prompt:
        qt_ref, kt_ref, vt_ref, kbf_ref, vtb_ref, km_ref, glu_ref = out_refs
        qt_ref[0] = q.T
        kt_ref[0] = k.T
        vt = v.T
        vt_ref[0] = vt
        kbf_ref[0] = k.astype(BF16)
        for c in range(tm // MOBA_BLOCK):
            vtb_ref[0, c] = vt[:, c * MOBA_BLOCK:(c + 1) * MOBA_BLOCK].astype(BF16)
        km_ref[0] = jnp.mean(k.reshape(tm // MOBA_BLOCK, MOBA_BLOCK, ATT_WIDTH), axis=1)[:, None, :]
        glu_ref[0] = glu
    else:
        q_ref, k_ref, v_ref, glu_ref = out_refs
        q_ref[0] = q
        k_ref[0] = k
        v_ref[0] = v
        glu_ref[0] = glu


def _proj(x, g, w_bf, *, prompt, tm):
    n_seq, rows, _ = x.shape
    n_out = w_bf.shape[1]
    grid = (n_seq, rows // tm)
    row_spec = lambda width: pl.BlockSpec((1, tm, width), lambda s, i: (s, i, 0))
    col_spec = pl.BlockSpec((1, ATT_WIDTH, tm), lambda s, i: (s, 0, i))
    if prompt:
        nb = tm // MOBA_BLOCK
        out_specs = [col_spec, col_spec, col_spec, row_spec(ATT_WIDTH),
                     pl.BlockSpec((1, nb, ATT_WIDTH, MOBA_BLOCK), lambda s, i: (s, i, 0, 0)),
                     pl.BlockSpec((1, nb, 1, ATT_WIDTH), lambda s, i: (s, i, 0, 0)),
                     row_spec(CONV_WIDTH)]
        t_shape = jax.ShapeDtypeStruct((n_seq, ATT_WIDTH, rows), F32)
        out_shape = [t_shape, t_shape, t_shape,
                     jax.ShapeDtypeStruct((n_seq, rows, ATT_WIDTH), BF16),
                     jax.ShapeDtypeStruct((n_seq, rows // MOBA_BLOCK, ATT_WIDTH, MOBA_BLOCK), BF16),
                     jax.ShapeDtypeStruct((n_seq, rows // MOBA_BLOCK, 1, ATT_WIDTH), F32),
                     jax.ShapeDtypeStruct((n_seq, rows, CONV_WIDTH), F32)]
    else:
        out_specs = [row_spec(ATT_WIDTH)] * 3 + [row_spec(CONV_WIDTH)]
        out_shape = [jax.ShapeDtypeStruct((n_seq, rows, ATT_WIDTH), F32)] * 3 + [
            jax.ShapeDtypeStruct((n_seq, rows, CONV_WIDTH), F32)]
    return pl.pallas_call(
        functools.partial(_proj_kernel, prompt=prompt, tm=tm),
        grid=grid,
        in_specs=[row_spec(D_MODEL),
                  pl.BlockSpec((1, D_MODEL), lambda s, i: (0, 0)),
                  pl.BlockSpec((D_MODEL, n_out), lambda s, i: (0, 0))],
        out_specs=out_specs,
        out_shape=out_shape,
        compiler_params=pltpu.CompilerParams(dimension_semantics=("arbitrary", "arbitrary"),
                                             vmem_limit_bytes=VMEM_LIMIT),
        name="proj_prompt" if prompt else "proj_sample",
    )(x, g, w_bf)


def _select_topk(gate, idx, axis, n_valid):
    big = jnp.int32(2 ** 30)
    g = jnp.where(idx < n_valid, gate, -jnp.inf)
    sel = jnp.zeros(gate.shape, jnp.bool_)
    for _ in range(MOBA_TOPK):
        mx = jnp.max(g, axis=axis, keepdims=True)
        first = jnp.min(jnp.where(g == mx, idx, big), axis=axis, keepdims=True)
        hit = idx == first
        sel = sel | (hit & (mx > -jnp.inf))
        g = jnp.where(hit, -jnp.inf, g)
    return sel


def _attn_prompt_kernel(tbl_ref, qt_ref, k_ref, vt_ref, km_ref, near_ref, o_ref, mask_ref, *, nblk):
    p = pl.program_id(1)
    i = pl.program_id(2)
    blk = MOBA_BLOCK
    qt = qt_ref[0]
    z = jnp.zeros((HEAD_DIM, blk), F32)
    q2t = jnp.concatenate([jnp.concatenate([qt[:HEAD_DIM], z], axis=1),
                           jnp.concatenate([z, qt[HEAD_DIM:]], axis=1)], axis=0)
    q2tb = (q2t * SCALE).astype(BF16)

    gate = jnp.dot(km_ref[0], q2t, precision=lax.Precision.HIGHEST, preferred_element_type=F32)
    n_idx = lax.broadcasted_iota(jnp.int32, (nblk, 2 * blk), 0)
    sel = _select_topk(gate, n_idx, 0, i)
    col = lax.broadcasted_iota(jnp.int32, (1, 2 * blk), 1)
    far = jnp.where(col < blk, tbl_ref[N_BUCKETS - 1, 2 * p], tbl_ref[N_BUCKETS - 1, 2 * p + 1])
    mask_ref[...] = jnp.where(sel, far, NEG)

    def scores(j):
        kb = k_ref[0, pl.ds(pl.multiple_of(j * blk, blk), blk), :]
        return jnp.dot(kb, q2tb, preferred_element_type=F32)

    def pv(j, pb):
        vt = vt_ref[0, j]
        return (jnp.dot(vt[:HEAD_DIM], pb[:, :blk], preferred_element_type=F32),
                jnp.dot(vt[HEAD_DIM:], pb[:, blk:], preferred_element_type=F32))

    def update(j, s, carry):
        m, l, a_a, a_b = carry
        m_new = jnp.maximum(m, jnp.max(s, axis=0, keepdims=True))
        alpha = jnp.exp(m - m_new)
        pm = jnp.exp(s - m_new)
        l = alpha * l + jnp.sum(pm, axis=0, keepdims=True)
        d_a, d_b = pv(j, pm.astype(BF16))
        return m_new, l, a_a * alpha[:, :blk] + d_a, a_b * alpha[:, blk:] + d_b

    s = scores(i) + near_ref[0, blk:, :]
    m = jnp.max(s, axis=0, keepdims=True)
    pm = jnp.exp(s - m)
    l = jnp.sum(pm, axis=0, keepdims=True)
    a_a, a_b = pv(i, pm.astype(BF16))
    carry = (m, l, a_a, a_b)

    jp = jnp.maximum(i - 1, 0)
    sel_prev = (mask_ref[pl.ds(jp, 1), :] > 0.5 * NEG) & (i > 0)
    carry = update(jp, scores(jp) + jnp.where(sel_prev, near_ref[0, :blk, :], NEG), carry)

    def step(j, carry):
        return update(j, scores(j) + mask_ref[pl.ds(j, 1), :], carry)

    m, l, a_a, a_b = lax.fori_loop(0, jnp.maximum(i - 1, 0), step, carry)
    out_t = jnp.concatenate([a_a / l[:, :blk], a_b / l[:, blk:]], axis=0)
    o_ref[0] = out_t.T


def _attn_prompt(rel_bias, qt, kbf, vtb, kmean, near_t):
    n_seq, _, t = qt.shape
    nblk = t // MOBA_BLOCK
    return pl.pallas_call(
        functools.partial(_attn_prompt_kernel, nblk=nblk),
        grid=(n_seq, N_PAIRS, nblk),
        in_specs=[pl.BlockSpec(memory_space=pltpu.SMEM),
                  pl.BlockSpec((1, LANES, MOBA_BLOCK), lambda b, p, i: (b, p, i)),
                  pl.BlockSpec((1, t, LANES), lambda b, p, i: (b, 0, p)),
                  pl.BlockSpec((1, nblk, LANES, MOBA_BLOCK), lambda b, p, i: (b, 0, p, 0)),
                  pl.BlockSpec((1, nblk, LANES), lambda b, p, i: (b, 0, p)),
                  pl.BlockSpec((1, 2 * MOBA_BLOCK, 2 * MOBA_BLOCK), lambda b, p, i: (p, 0, 0))],
        out_specs=pl.BlockSpec((1, MOBA_BLOCK, LANES), lambda b, p, i: (b, i, p)),
        out_shape=jax.ShapeDtypeStruct((n_seq, t, ATT_WIDTH), F32),
        scratch_shapes=[pltpu.VMEM((nblk, 2 * MOBA_BLOCK), F32)],
        compiler_params=pltpu.CompilerParams(dimension_semantics=("arbitrary",) * 3,
                                             vmem_limit_bytes=VMEM_LIMIT),
        name="attn_prompt",
    )(rel_bias, qt, kbf, vtb, kmean, near_t)


def _attn_sample_kernel(pt_ref, tbl_ref, q_ref, kn_ref, vn_ref, near_ref, *refs, n_pages, group, n_new):
    k_refs = refs[:group]
    v_refs = refs[group:2 * group]
    o_ref = refs[2 * group]
    s_all, km_t, qbd_f, qbd_b, acc_ref, l_ref = refs[2 * group + 1:]
    ph = pl.program_id(1)
    g = pl.program_id(2)
    n_groups = n_pages // group
    n_rows = N_HEADS * n_new
    page = LANES
    nblk = n_pages * page // MOBA_BLOCK

    row_head = lax.broadcasted_iota(jnp.int32, (n_rows, ATT_WIDTH), 0) // n_new
    col_head = lax.broadcasted_iota(jnp.int32, (n_rows, ATT_WIDTH), 1) // HEAD_DIM
    diag = row_head == col_head

    @pl.when((ph == 0) & (g == 0))
    def _():
        q = q_ref[0]
        qbd = jnp.where(diag, jnp.concatenate([q] * N_HEADS, axis=0), 0.0)
        qbd_f[...] = qbd
        qbd_b[...] = (qbd * SCALE).astype(BF16)
        km_t[...] = jnp.zeros(km_t.shape, F32)

    @pl.when(ph == 0)
    def _():
        lane = lax.broadcasted_iota(jnp.int32, (ATT_WIDTH, LANES), 1)
        qb = qbd_b[...]
        for r in range(0, group, 2):
            kt0 = k_refs[r][0]
            kt1 = k_refs[r + 1][0]
            s_all[g * group + r] = jnp.dot(qb, kt0.astype(BF16), preferred_element_type=F32)
            s_all[g * group + r + 1] = jnp.dot(qb, kt1.astype(BF16), preferred_element_type=F32)
            mean = jnp.sum(kt0 + kt1, axis=1, keepdims=True) * (1.0 / MOBA_BLOCK)
            blk_id = (g * group + r) // 2
            km_t[...] = jnp.where(lane == blk_id, mean, km_t[...])

    @pl.when((ph == 0) & (g == n_groups - 1))
    def _():
        gate = jnp.dot(qbd_f[...], km_t[...], precision=lax.Precision.HIGHEST,
                       preferred_element_type=F32)
        lane = lax.broadcasted_iota(jnp.int32, (n_rows, LANES), 1)
        sel = _select_topk(gate, lane, 1, nblk)
        rh = lax.broadcasted_iota(jnp.int32, (n_rows, 1), 0) // n_new
        far = jnp.zeros((n_rows, 1), F32)
        for h in range(N_HEADS):
            far = jnp.where(rh == h, tbl_ref[N_BUCKETS - 1, h], far)
        mask = jnp.where(sel, far, NEG)
        near = near_ref[...]
        for n in range(nblk):
            if n == nblk - 1:
                seln = jnp.broadcast_to(mask[:, n:n + 1], (n_rows, page)) > 0.5 * NEG
                s_all[2 * n] = s_all[2 * n] + jnp.where(seln, near[:, 0:page], NEG)
                s_all[2 * n + 1] = s_all[2 * n + 1] + jnp.where(seln, near[:, page:2 * page], NEG)
            else:
                add = jnp.broadcast_to(mask[:, n:n + 1], (n_rows, page))
                s_all[2 * n] = s_all[2 * n] + add
                s_all[2 * n + 1] = s_all[2 * n + 1] + add
        kn = jnp.concatenate([kn_ref[0], jnp.zeros((page - n_new, ATT_WIDTH), F32)], axis=0).astype(BF16)
        s_own = lax.dot_general(qbd_b[...], kn, (((1,), (1,)), ((), ())),
                                preferred_element_type=F32) + near[:, 2 * page:3 * page]
        mrun = lax.fori_loop(0, n_pages, lambda j, mm: jnp.maximum(mm, s_all[j]), s_own)
        m = jnp.max(mrun, axis=1, keepdims=True)
        p_own = jnp.exp(s_own - m)

        def exp_step(j, lsum):
            pj = jnp.exp(s_all[j] - m)
            s_all[j] = pj
            return lsum + pj

        lsum = lax.fori_loop(0, n_pages, exp_step, p_own)
        l_ref[...] = jnp.broadcast_to(jnp.sum(lsum, axis=1, keepdims=True), l_ref.shape)
        vn = jnp.concatenate([vn_ref[0], jnp.zeros((page - n_new, ATT_WIDTH), F32)], axis=0).astype(BF16)
        acc_ref[...] = jnp.dot(p_own.astype(BF16), vn, preferred_element_type=F32)

    @pl.when(ph == 1)
    def _():
        acc = acc_ref[...]
        for r in range(group):
            pb = s_all[g * group + r].astype(BF16)
            vt = v_refs[r][0].astype(BF16)
            acc = acc + lax.dot_general(pb, vt, (((1,), (1,)), ((), ())), preferred_element_type=F32)
        acc_ref[...] = acc

    @pl.when((ph == 1) & (g == n_groups - 1))
    def _():
        o = jnp.where(diag, acc_ref[...] / l_ref[:, 0:1], 0.0)
        o_ref[0] = jnp.sum(o.reshape(N_HEADS, n_new, ATT_WIDTH), axis=0)


def _attn_sample(page_table, rel_bias, q, k_new, v_new, near_s, cache_kt, cache_vt, *, group):
    n_seq, n_new, _ = q.shape
    n_pages = page_table.shape[1]
    n_groups = n_pages // group
    n_rows = N_HEADS * n_new
    page = cache_kt.shape[2]

    def k_map(r):
        return lambda b, ph, g, pt: (pt[b, jnp.where(ph == 0, g, n_groups - 1) * group + r], 0, 0)

    def v_map(r):
        return lambda b, ph, g, pt: (pt[b, jnp.where(ph == 1, g, 0) * group + r], 0, 0)

    new_spec = pl.BlockSpec((1, n_new, ATT_WIDTH), lambda b, ph, g, pt: (b, 0, 0))
    in_specs = ([pl.BlockSpec(memory_space=pltpu.SMEM), new_spec, new_spec, new_spec,
                 pl.BlockSpec((n_rows, 2 * MOBA_BLOCK), lambda b, ph, g, pt: (0, 0))]
                + [pl.BlockSpec((1, ATT_WIDTH, page), k_map(r)) for r in range(group)]
                + [pl.BlockSpec((1, ATT_WIDTH, page), v_map(r)) for r in range(group)])
    return pl.pallas_call(
        functools.partial(_attn_sample_kernel, n_pages=n_pages, group=group, n_new=n_new),
        grid_spec=pltpu.PrefetchScalarGridSpec(
            num_scalar_prefetch=1,
            grid=(n_seq, 2, n_groups),
            in_specs=in_specs,
            out_specs=new_spec,
            scratch_shapes=[pltpu.VMEM((n_pages, n_rows, page), F32),
                            pltpu.VMEM((ATT_WIDTH, LANES), F32),
                            pltpu.VMEM((n_rows, ATT_WIDTH), F32),
                            pltpu.VMEM((n_rows, ATT_WIDTH), BF16),
                            pltpu.VMEM((n_rows, ATT_WIDTH), F32),
                            pltpu.VMEM((n_rows, LANES), F32)]),
        out_shape=jax.ShapeDtypeStruct((n_seq, n_new, ATT_WIDTH), F32),
        compiler_params=pltpu.CompilerParams(dimension_semantics=("arbitrary",) * 3,
                                             vmem_limit_bytes=VMEM_LIMIT),
        name="attn_sample",
    )(page_table, rel_bias, q, k_new, v_new, near_s, *([cache_kt] * group), *([cache_vt] * group))


def _mix_kernel(att_ref, glu_ref, x_ref, st_ref, cw_ref, cb_ref, lg_ref, lb_ref, wo_ref, gpm_ref, gpf_ref,
                x1_ref, hn_ref, gbuf, cvbuf, *, tm, stride, halo, chunk):
    i = pl.program_id(1)

    @pl.when(i == 0)
    def _():
        gbuf[0:halo, :] = st_ref[0]

    gbuf[halo:halo + tm, :] = glu_ref[0]
    off0 = halo - (CONV_KERNEL - 1) * stride
    for c in range(tm // chunk):
        r0 = c * chunk
        acc = jnp.broadcast_to(cb_ref[...], (chunk, CONV_WIDTH))
        for j in range(CONV_KERNEL):
            a = off0 + j * stride + r0
            acc = acc + cw_ref[j:j + 1, :] * gbuf[a:a + chunk, :]
        mu = jnp.mean(acc, axis=-1, keepdims=True)
        d = acc - mu
        var = jnp.mean(d * d, axis=-1, keepdims=True)
        y = d * lax.rsqrt(var + EPS) * lg_ref[...] + lb_ref[...]
        cvbuf[r0:r0 + chunk, :] = (y * jax.nn.sigmoid(y)).astype(BF16)

    mix = (jnp.dot(att_ref[0].astype(BF16), wo_ref[0:ATT_WIDTH, :], preferred_element_type=F32)
           + jnp.dot(cvbuf[...], wo_ref[ATT_WIDTH:, :], preferred_element_type=F32))
    x1 = x_ref[0] + _rms(mix, gpm_ref[...])
    x1_ref[0] = x1
    hn_ref[0] = _rms(x1, gpf_ref[...]).astype(BF16)

    if tm >= halo:
        gbuf[0:halo, :] = gbuf[tm:tm + halo, :]


def _mix(att, glu, x, state, cw, cb, lg, lb, wo_bf, gpm, gpf, *, tm, stride):
    n_seq, rows, _ = x.shape
    halo = state.shape[1]
    assert rows == tm or tm >= halo
    row_spec = lambda width: pl.BlockSpec((1, tm, width), lambda s, i: (s, i, 0))
    const = lambda shape: pl.BlockSpec(shape, lambda s, i: (0,) * len(shape))
    return pl.pallas_call(
        functools.partial(_mix_kernel, tm=tm, stride=stride, halo=halo, chunk=min(tm, 32)),
        grid=(n_seq, rows // tm),
        in_specs=[row_spec(ATT_WIDTH), row_spec(CONV_WIDTH), row_spec(D_MODEL),
                  pl.BlockSpec((1, halo, CONV_WIDTH), lambda s, i: (s, 0, 0)),
                  const((CONV_KERNEL, CONV_WIDTH)), const((1, CONV_WIDTH)), const((1, CONV_WIDTH)),
                  const((1, CONV_WIDTH)), const((D_MODEL, D_MODEL)), const((1, D_MODEL)), const((1, D_MODEL))],
        out_specs=[row_spec(D_MODEL), row_spec(D_MODEL)],
        out_shape=[jax.ShapeDtypeStruct((n_seq, rows, D_MODEL), F32),
                   jax.ShapeDtypeStruct((n_seq, rows, D_MODEL), BF16)],
        scratch_shapes=[pltpu.VMEM((halo + tm, CONV_WIDTH), F32), pltpu.VMEM((tm, CONV_WIDTH), BF16)],
        compiler_params=pltpu.CompilerParams(dimension_semantics=("arbitrary", "arbitrary"),
                                             vmem_limit_bytes=VMEM_LIMIT),
        name="mix_s%d" % stride,
    )(att, glu, x, state, cw, cb, lg, lb, wo_bf, gpm, gpf)


def _ffn_kernel(hn_ref, x1_ref, wg_ref, wv_ref, wd_ref, dwg_ref, dwv_ref, dbg_ref, dbv_ref, stg_ref, stv_ref,
                gpo_ref, y_ref, tg_ref, tv_ref, ubuf_g, ubuf_v, car_g, car_v, acc_ref, *, tm, stride, halo, nc):
    i = pl.program_id(1)
    c = pl.program_id(2)
    hn = hn_ref[0]

    def branch(w_ref, dw_ref, db_ref, st_ref, tail_ref, ubuf, car):
        u = jnp.dot(hn, w_ref[...], preferred_element_type=F32)

        @pl.when(i == 0)
        def _():
            ubuf[0:halo, :] = st_ref[0]

        @pl.when(i > 0)
        def _():
            ubuf[0:halo, :] = car[c]

        ubuf[halo:halo + tm, :] = u
        last = ubuf[tm:tm + halo, :]
        car[c] = last
        tail_ref[0, 0] = last
        y = (dw_ref[0:1, :] * ubuf[halo - 2 * stride:halo - 2 * stride + tm, :]
             + dw_ref[1:2, :] * ubuf[halo - stride:halo - stride + tm, :]
             + dw_ref[2:3, :] * u + db_ref[...])
        return y

    yg = branch(wg_ref, dwg_ref, dbg_ref, stg_ref, tg_ref, ubuf_g, car_g)
    yv = branch(wv_ref, dwv_ref, dbv_ref, stv_ref, tv_ref, ubuf_v, car_v)
    a = jax.nn.gelu(yg, approximate=True) * yv
    part = jnp.dot(a.astype(BF16), wd_ref[...], preferred_element_type=F32)

    @pl.when(c == 0)
    def _():
        acc_ref[...] = part

    @pl.when(c > 0)
    def _():
        acc_ref[...] = acc_ref[...] + part

    @pl.when(c == nc - 1)
    def _():
        y_ref[0] = x1_ref[0] + _rms(acc_ref[...], gpo_ref[...])


def _ffn(hn, x1, wup_bf, wdn_bf, dw, db, state, gpo, *, tm, stride, nc):
    n_seq, rows, _ = x1.shape
    halo = state.shape[1]
    ck = D_FF // nc
    row_spec = lambda width: pl.BlockSpec((1, tm, width), lambda s, i, c: (s, i, 0))
    gate_cols = lambda shape: pl.BlockSpec(shape, lambda s, i, c: (0, c))
    val_cols = lambda shape: pl.BlockSpec(shape, lambda s, i, c: (0, c + nc))
    return pl.pallas_call(
        functools.partial(_ffn_kernel, tm=tm, stride=stride, halo=halo, nc=nc),
        grid=(n_seq, rows // tm, nc),
        in_specs=[row_spec(D_MODEL), row_spec(D_MODEL),
                  gate_cols((D_MODEL, ck)), val_cols((D_MODEL, ck)),
                  pl.BlockSpec((ck, D_MODEL), lambda s, i, c: (c, 0)),
                  gate_cols((FFN_KERNEL, ck)), val_cols((FFN_KERNEL, ck)),
                  gate_cols((1, ck)), val_cols((1, ck)),
                  pl.BlockSpec((1, halo, ck), lambda s, i, c: (s, 0, c)),
                  pl.BlockSpec((1, halo, ck), lambda s, i, c: (s, 0, c + nc)),
                  pl.BlockSpec((1, D_MODEL), lambda s, i, c: (0, 0))],
        out_specs=[row_spec(D_MODEL),
                   pl.BlockSpec((1, 1, halo, ck), lambda s, i, c: (s, i, 0, c)),
                   pl.BlockSpec((1, 1, halo, ck), lambda s, i, c: (s, i, 0, c))],
        out_shape=[jax.ShapeDtypeStruct((n_seq, rows, D_MODEL), F32),
                   jax.ShapeDtypeStruct((n_seq, rows // tm, halo, D_FF), F32),
                   jax.ShapeDtypeStruct((n_seq, rows // tm, halo, D_FF), F32)],
        scratch_shapes=[pltpu.VMEM((halo + tm, ck), F32), pltpu.VMEM((halo + tm, ck), F32),
                        pltpu.VMEM((nc, halo, ck), F32), pltpu.VMEM((nc, halo, ck), F32),
                        pltpu.VMEM((tm, D_MODEL), F32)],
        compiler_params=pltpu.CompilerParams(dimension_semantics=("arbitrary",) * 3,
                                             vmem_limit_bytes=VMEM_LIMIT),
        name="ffn_s%d" % stride,
    )(hn, x1, wup_bf, wup_bf, wdn_bf, dw, dw, db, db, state, state, gpo)


def _tile(rows, want):
    return want if rows % want == 0 else rows


def kernel(x_prompt, x_sample, cache_k, cache_v, state_conv, state_ffn, page_table, rel_bias, g_pre_mix, w_in,
           conv_dw_w, conv_dw_b, conv_ln_g, conv_ln_b, w_out, g_post_mix, g_pre_ffn, w_ffn_up, ffn_dw_w,
           ffn_dw_b, w_ffn_down, g_post_ffn):
    depth = w_in.shape[0]
    assert depth == 1, "single-layer trunk"
    bp, seq, _ = x_prompt.shape
    bs, n_new, _ = x_sample.shape
    n_pool, page = cache_k.shape[1], cache_k.shape[2]
    n_pages = page_table.shape[1]
    assert seq % MOBA_BLOCK == 0 and page == LANES and (n_pages * page) % MOBA_BLOCK == 0
    assert n_new == SUBLANES and n_pages * page // MOBA_BLOCK <= LANES

    w_in_bf = w_in[0].astype(BF16)
    w_out_bf = w_out[0].astype(BF16)
    w_up_bf = w_ffn_up[0].astype(BF16)
    w_dn_bf = w_ffn_down[0].astype(BF16)
    cw, cb = conv_dw_w[0], conv_dw_b
    near_t, near_s = _bias_tiles(rel_bias)
    conv_halo_p = 32
    ffn_halo_p = SUBLANES

    tm_p = _tile(seq, 512)
    qt, kt, vt, kbf, vtb, kmean, glu_p = _proj(x_prompt, g_pre_mix, w_in_bf, prompt=True, tm=tm_p)
    att_p = _attn_prompt(rel_bias, qt, kbf, vtb, kmean.reshape(bp, seq // MOBA_BLOCK, ATT_WIDTH), near_t)
    x1_p, hn_p = _mix(att_p, glu_p, x_prompt, jnp.zeros((bp, conv_halo_p, CONV_WIDTH), F32), cw, cb,
                      conv_ln_g, conv_ln_b, w_out_bf, g_post_mix, g_pre_ffn, tm=tm_p, stride=1)
    y_prompt, tail_g, tail_v = _ffn(hn_p, x1_p, w_up_bf, w_dn_bf, ffn_dw_w[0], ffn_dw_b,
                                    jnp.zeros((bp, ffn_halo_p, 2 * D_FF), F32), g_post_ffn,
                                    tm=tm_p, stride=1, nc=2)
    k_prompt = kt.reshape(1, bp, N_HEADS, HEAD_DIM, seq).transpose(0, 1, 4, 2, 3)
    v_prompt = vt.reshape(1, bp, N_HEADS, HEAD_DIM, seq).transpose(0, 1, 4, 2, 3)
    conv_prompt = glu_p[:, seq - (CONV_KERNEL - 1):, :][None]
    ffn_prompt = jnp.concatenate([tail_g[:, -1], tail_v[:, -1]],
                                 axis=-1)[:, ffn_halo_p - (FFN_KERNEL - 1):, :][None]

    rows_s = n_new * bs
    to_tb = lambda a: a.transpose(1, 0, 2).reshape(1, rows_s, a.shape[-1])
    to_bt = lambda a: a.reshape(n_new, bs, a.shape[-1]).transpose(1, 0, 2)
    xs = to_tb(x_sample)
    q_s, k_s, v_s, glu_s = _proj(xs, g_pre_mix, w_in_bf, prompt=False, tm=rows_s)
    q_b, k_b, v_b = to_bt(q_s), to_bt(k_s), to_bt(v_s)
    cache_kt = cache_k[0].transpose(0, 2, 3, 1).reshape(n_pool, ATT_WIDTH, page)
    cache_vt = cache_v[0].transpose(0, 2, 3, 1).reshape(n_pool, ATT_WIDTH, page)
    att_b = _attn_sample(page_table, rel_bias, q_b, k_b, v_b, near_s, cache_kt, cache_vt,
                         group=min(16, n_pages))
    conv_state = state_conv[0].transpose(1, 0, 2).reshape(1, (CONV_KERNEL - 1) * bs, CONV_WIDTH)
    x1_s, hn_s = _mix(to_tb(att_b), glu_s, xs, conv_state, cw, cb, conv_ln_g, conv_ln_b, w_out_bf,
                      g_post_mix, g_pre_ffn, tm=rows_s, stride=bs)
    ffn_state = state_ffn[0].transpose(1, 0, 2).reshape(1, (FFN_KERNEL - 1) * bs, 2 * D_FF)
    y_s, tail_gs, tail_vs = _ffn(hn_s, x1_s, w_up_bf, w_dn_bf, ffn_dw_w[0], ffn_dw_b, ffn_state, g_post_ffn,
                                 tm=rows_s, stride=bs, nc=2)
    y_sample = to_bt(y_s)
    k_sample = k_b.reshape(1, bs, n_new, N_HEADS, HEAD_DIM)
    v_sample = v_b.reshape(1, bs, n_new, N_HEADS, HEAD_DIM)
    conv_all = jnp.concatenate([conv_state[0], glu_s[0]], axis=0)[n_new * bs:]
    conv_sample = conv_all.reshape(CONV_KERNEL - 1, bs, CONV_WIDTH).transpose(1, 0, 2)[None]
    ffn_sample = jnp.concatenate([tail_gs[:, -1], tail_vs[:, -1]], axis=-1).reshape(
        FFN_KERNEL - 1, bs, 2 * D_FF).transpose(1, 0, 2)[None]

    return (y_prompt, y_sample, k_prompt, v_prompt, conv_prompt, ffn_prompt,
            k_sample, v_sample, conv_sample, ffn_sample)
```

```python
import functools
import math

import numpy as np
import jax
import jax.numpy as jnp
from jax import lax
from jax.experimental import pallas as pl
from jax.experimental.pallas import tpu as pltpu

F32 = jnp.float32
BF16 = jnp.bfloat16

D_MODEL = 1024
HEAD_DIM = 64
ATT_WIDTH = 512
N_HEADS = 8
N_PAIRS = N_HEADS // 2
CONV_WIDTH = 512
CONV_KERNEL = 31
MOBA_BLOCK = 256
MOBA_TOPK = 3
N_BUCKETS = 32
MAX_DISTANCE = 128
D_FF = 2816
FFN_KERNEL = 3
EPS = 1e-6
SCALE = HEAD_DIM ** -0.5
NEG = -1e30
LANES = 128
SUBLANES = 8
VMEM_LIMIT = 56 * 1024 * 1024
ONES_ROWS = 16
V_ROWS = HEAD_DIM + ONES_ROWS


def _bucket_thresholds():
    n = np.arange(0, 4 * MAX_DISTANCE)
    max_exact = N_BUCKETS // 2
    nf = np.maximum(n, 1).astype(np.float64)
    large = max_exact + (np.log(nf / max_exact) / math.log(MAX_DISTANCE / max_exact)
                         * (N_BUCKETS - max_exact)).astype(np.int64)
    bucket = np.where(n < max_exact, n, np.minimum(large, N_BUCKETS - 1))
    return tuple(int(np.argmax(bucket >= k)) for k in range(1, N_BUCKETS))


_BUCKET_THR = _bucket_thresholds()


def _rms(x, g):
    return x * lax.rsqrt(jnp.mean(x * x, axis=-1, keepdims=True) + EPS) * g


def _bias_kernel(tbl_ref, near_t_ref, near_s_ref):
    p = pl.program_id(0)

    def bias_of(dist, h):
        b = jnp.full(dist.shape, tbl_ref[0, h], F32)
        for k in range(1, N_BUCKETS):
            b = jnp.where(dist >= _BUCKET_THR[k - 1], tbl_ref[k, h], b)
        return jnp.where(dist >= 0, b, NEG)

    jj = lax.broadcasted_iota(jnp.int32, (2 * MOBA_BLOCK, MOBA_BLOCK), 0)
    ii = lax.broadcasted_iota(jnp.int32, (2 * MOBA_BLOCK, MOBA_BLOCK), 1)
    dist = MOBA_BLOCK + ii - jj
    for half in range(2):
        h = 2 * p + half
        near_t_ref[0, :, half * MOBA_BLOCK:(half + 1) * MOBA_BLOCK] = (
            bias_of(dist, h) - tbl_ref[N_BUCKETS - 1, h])

    @pl.when(p == 0)
    def _():
        tt = lax.broadcasted_iota(jnp.int32, (SUBLANES, 2 * MOBA_BLOCK), 0)
        j2 = lax.broadcasted_iota(jnp.int32, (SUBLANES, 2 * MOBA_BLOCK), 1)
        d2 = MOBA_BLOCK + tt - j2
        for h in range(N_HEADS):
            near_s_ref[h * SUBLANES:(h + 1) * SUBLANES, :] = bias_of(d2, h)


def _bias_tiles(rel_bias):
    return pl.pallas_call(
        _bias_kernel,
        grid=(N_PAIRS,),
        in_specs=[pl.BlockSpec(memory_space=pltpu.SMEM)],
        out_specs=[pl.BlockSpec((1, 2 * MOBA_BLOCK, 2 * MOBA_BLOCK), lambda p: (p, 0, 0)),
                   pl.BlockSpec((N_HEADS * SUBLANES, 2 * MOBA_BLOCK), lambda p: (0, 0))],
        out_shape=[jax.ShapeDtypeStruct((N_PAIRS, 2 * MOBA_BLOCK, 2 * MOBA_BLOCK), F32),
                   jax.ShapeDtypeStruct((N_HEADS * SUBLANES, 2 * MOBA_BLOCK), F32)],
        compiler_params=pltpu.CompilerParams(dimension_semantics=("arbitrary",)),
        name="bias_tiles",
    )(rel_bias)


def _proj_kernel(x_ref, g_ref, w_ref, *out_refs, prompt, tm):
    h = _rms(x_ref[0], g_ref[...])
    p = jnp.dot(h.astype(BF16), w_ref[...], preferred_element_type=F32)
    q = p[:, 0:ATT_WIDTH]
    k = p[:, ATT_WIDTH:2 * ATT_WIDTH]
    v = p[:, 2 * ATT_WIDTH:3 * ATT_WIDTH]
    ga = p[:, 3 * ATT_WIDTH:3 * ATT_WIDTH + CONV_WIDTH]
    gb = p[:, 3 * ATT_WIDTH + CONV_WIDTH:]
    glu = ga * jax.nn.sigmoid(gb)
    if prompt:
        qt_ref, kt_ref, vt_ref, kbf_ref, vtb_ref, km_ref, glu_ref = out_refs
        qt_ref[0] = q.T
        kt_ref[0] = k.T
        vt = v.T
        vt_ref[0] = vt
        row_blk = (pl.program_id(1) * (tm // MOBA_BLOCK)
                   + lax.broadcasted_iota(jnp.int32, (tm, LANES), 0) // MOBA_BLOCK)
        onehot = jnp.where(lax.broadcasted_iota(jnp.int32, (tm, LANES), 1) == row_blk, 1.0, 0.0).astype(BF16)
        kb = k.astype(BF16)
        for pr in range(N_PAIRS):
            kbf_ref[0, :, 2 * pr * LANES:(2 * pr + 1) * LANES] = kb[:, pr * LANES:(pr + 1) * LANES]
            kbf_ref[0, :, (2 * pr + 1) * LANES:(2 * pr + 2) * LANES] = onehot
        ones = jnp.ones((ONES_ROWS, MOBA_BLOCK), F32)
        for c in range(tm // MOBA_BLOCK):
            vc = vt[:, c * MOBA_BLOCK:(c + 1) * MOBA_BLOCK]
            parts = []
            for hd in range(N_HEADS):
                parts += [vc[hd * HEAD_DIM:(hd + 1) * HEAD_DIM], ones]
            vtb_ref[0, c] = jnp.concatenate(parts, axis=0).astype(BF16)
        km_ref[0] = jnp.mean(k.reshape(tm // MOBA_BLOCK, MOBA_BLOCK, ATT_WIDTH), axis=1)[:, None, :]
        glu_ref[0] = glu
    else:
        q_ref, k_ref, v_ref, glu_ref = out_refs
        q_ref[0] = q
        k_ref[0] = k
        v_ref[0] = v
        glu_ref[0] = glu


def _proj(x, g, w_bf, *, prompt, tm):
    n_seq, rows, _ = x.shape
    n_out = w_bf.shape[1]
    grid = (n_seq, rows // tm)
    row_spec = lambda width: pl.BlockSpec((1, tm, width), lambda s, i: (s, i, 0))
    col_spec = pl.BlockSpec((1, ATT_WIDTH, tm), lambda s, i: (s, 0, i))
    if prompt:
        nb = tm // MOBA_BLOCK
        out_specs = [col_spec, col_spec, col_spec, row_spec(2 * ATT_WIDTH),
                     pl.BlockSpec((1, nb, N_HEADS * V_ROWS, MOBA_BLOCK), lambda s, i: (s, i, 0, 0)),
                     pl.BlockSpec((1, nb, 1, ATT_WIDTH), lambda s, i: (s, i, 0, 0)),
                     row_spec(CONV_WIDTH)]
        t_shape = jax.ShapeDtypeStruct((n_seq, ATT_WIDTH, rows), F32)
        out_shape = [t_shape, t_shape, t_shape,
                     jax.ShapeDtypeStruct((n_seq, rows, 2 * ATT_WIDTH), BF16),
                     jax.ShapeDtypeStruct((n_seq, rows // MOBA_BLOCK, N_HEADS * V_ROWS, MOBA_BLOCK), BF16),
                     jax.ShapeDtypeStruct((n_seq, rows // MOBA_BLOCK, 1, ATT_WIDTH), F32),
                     jax.ShapeDtypeStruct((n_seq, rows, CONV_WIDTH), F32)]
    else:
        out_specs = [row_spec(ATT_WIDTH)] * 3 + [row_spec(CONV_WIDTH)]
        out_shape = [jax.ShapeDtypeStruct((n_seq, rows, ATT_WIDTH), F32)] * 3 + [
            jax.ShapeDtypeStruct((n_seq, rows, CONV_WIDTH), F32)]
    return pl.pallas_call(
        functools.partial(_proj_kernel, prompt=prompt, tm=tm),
        grid=grid,
        in_specs=[row_spec(D_MODEL),
                  pl.BlockSpec((1, D_MODEL), lambda s, i: (0, 0)),
                  pl.BlockSpec((D_MODEL, n_out), lambda s, i: (0, 0))],
        out_specs=out_specs,
        out_shape=out_shape,
        compiler_params=pltpu.CompilerParams(dimension_semantics=("arbitrary", "arbitrary"),
                                             vmem_limit_bytes=VMEM_LIMIT),
        name="proj_prompt" if prompt else "proj_sample",
    )(x, g, w_bf)


def _select_topk(gate, idx, axis, n_valid):
    big = jnp.int32(2 ** 30)
    g = jnp.where(idx < n_valid, gate, -jnp.inf)
    sel = jnp.zeros(gate.shape, jnp.bool_)
    for _ in range(MOBA_TOPK):
        mx = jnp.max(g, axis=axis, keepdims=True)
        first = jnp.min(jnp.where(g == mx, idx, big), axis=axis, keepdims=True)
        hit = idx == first
        sel = sel | (hit & (mx > -jnp.inf))
        g = jnp.where(hit, -jnp.inf, g)
    return sel


def _attn_prompt_kernel(qt_ref, k_ref, vt_ref, km_ref, near_ref, o_ref, qaug_ref, sa_ref, sb_ref, *, nblk):
    i = pl.program_id(2)
    blk = MOBA_BLOCK
    qt = qt_ref[0]
    z = jnp.zeros((HEAD_DIM, blk), F32)
    q2t = jnp.concatenate([jnp.concatenate([qt[:HEAD_DIM], z], axis=1),
                           jnp.concatenate([z, qt[HEAD_DIM:]], axis=1)], axis=0)

    gate = jnp.dot(km_ref[0], q2t, precision=lax.Precision.HIGHEST, preferred_element_type=F32)
    n_idx = lax.broadcasted_iota(jnp.int32, (nblk, 2 * blk), 0)
    sel = _select_topk(gate, n_idx, 0, i)
    mrows = jnp.concatenate([jnp.where(sel | (n_idx == i), 0.0, NEG),
                             jnp.full((LANES - nblk, 2 * blk), NEG, F32)], axis=0)
    qaug_ref[0:LANES, :] = (q2t * SCALE).astype(BF16)
    qaug_ref[LANES:, :] = mrows.astype(BF16)

    def scores_into(dst_ref, half, j, bias):
        kb = k_ref[0, pl.ds(pl.multiple_of(j * blk, blk), blk), :]
        s = jnp.dot(kb, qaug_ref[...], preferred_element_type=F32)
        dst_ref[half] = s if bias is None else s + bias

    def softmax_pv(src_ref, j0, j1, carry):
        m, a_a, a_b = carry
        s0, s1 = src_ref[0], src_ref[1]
        m_new = jnp.maximum(m, jnp.maximum(jnp.max(s0, axis=0, keepdims=True),
                                           jnp.max(s1, axis=0, keepdims=True)))
        alpha = jnp.exp(m - m_new)
        p0 = jnp.exp(s0 - m_new).astype(BF16)
        p1 = jnp.exp(s1 - m_new).astype(BF16)
        v0, v1 = vt_ref[0, j0], vt_ref[0, j1]
        d_a = (jnp.dot(v0[:V_ROWS], p0[:, :blk], preferred_element_type=F32)
               + jnp.dot(v1[:V_ROWS], p1[:, :blk], preferred_element_type=F32))
        d_b = (jnp.dot(v0[V_ROWS:], p0[:, blk:], preferred_element_type=F32)
               + jnp.dot(v1[V_ROWS:], p1[:, blk:], preferred_element_type=F32))
        return m_new, a_a * alpha[:, :blk] + d_a, a_b * alpha[:, blk:] + d_b

    none = nblk - 1
    prev = jnp.where(i > 0, i - 1, none)
    n_far = jnp.maximum(i - 1, 0)
    zero = jnp.zeros((V_ROWS, blk), F32)
    carry = (jnp.full((1, 2 * blk), NEG, F32), zero, zero)

    scores_into(sa_ref, 0, i, near_ref[0, blk:, :])
    scores_into(sa_ref, 1, prev, near_ref[0, :blk, :])

    def half_step(dst_ref, src_ref, t, state):
        carry, jc0, jc1 = state
        j0 = 2 * t
        j1 = jnp.where(j0 + 1 < n_far, j0 + 1, none)
        scores_into(dst_ref, 0, j0, None)
        scores_into(dst_ref, 1, j1, None)
        return softmax_pv(src_ref, jc0, jc1, carry), j0, j1

    def step(t, state):
        return lax.cond(t % 2 == 0,
                        lambda st: half_step(sb_ref, sa_ref, t, st),
                        lambda st: half_step(sa_ref, sb_ref, t, st), state)

    n_pairs = (n_far + 1) // 2
    carry, jc0, jc1 = lax.fori_loop(0, n_pairs, step, (carry, i, prev))
    _, a_a, a_b = lax.cond(n_pairs % 2 == 0,
                           lambda c: softmax_pv(sa_ref, jc0, jc1, c),
                           lambda c: softmax_pv(sb_ref, jc0, jc1, c), carry)
    out_t = jnp.concatenate([a_a[:HEAD_DIM] / a_a[HEAD_DIM:HEAD_DIM + 1],
                             a_b[:HEAD_DIM] / a_b[HEAD_DIM:HEAD_DIM + 1]], axis=0)
    o_ref[0] = out_t.T


def _attn_prompt(qt, kaug, vtb, kmean, near_t):
    n_seq, _, t = qt.shape
    nblk = t // MOBA_BLOCK
    assert nblk % 2 == 0 and nblk < LANES
    return pl.pallas_call(
        functools.partial(_attn_prompt_kernel, nblk=nblk),
        grid=(n_seq, N_PAIRS, nblk),
        in_specs=[pl.BlockSpec((1, LANES, MOBA_BLOCK), lambda b, p, i: (b, p, i)),
                  pl.BlockSpec((1, t, 2 * LANES), lambda b, p, i: (b, 0, p)),
                  pl.BlockSpec((1, nblk, 2 * V_ROWS, MOBA_BLOCK), lambda b, p, i: (b, 0, p, 0)),
                  pl.BlockSpec((1, nblk, LANES), lambda b, p, i: (b, 0, p)),
                  pl.BlockSpec((1, 2 * MOBA_BLOCK, 2 * MOBA_BLOCK), lambda b, p, i: (p, 0, 0))],
        out_specs=pl.BlockSpec((1, MOBA_BLOCK, LANES), lambda b, p, i: (b, i, p)),
        out_shape=jax.ShapeDtypeStruct((n_seq, t, ATT_WIDTH), F32),
        scratch_shapes=[pltpu.VMEM((2 * LANES, 2 * MOBA_BLOCK), BF16),
                        pltpu.VMEM((2, MOBA_BLOCK, 2 * MOBA_BLOCK), F32),
                        pltpu.VMEM((2, MOBA_BLOCK, 2 * MOBA_BLOCK), F32)],
        compiler_params=pltpu.CompilerParams(dimension_semantics=("arbitrary",) * 3,
                                             vmem_limit_bytes=VMEM_LIMIT),
        name="attn_prompt",
    )(qt, kaug, vtb, kmean, near_t)


def _attn_sample_kernel(pt_ref, tbl_ref, q_ref, kn_ref, vn_ref, near_ref, *refs, n_pages, group, n_new):
    k_refs = refs[:group]
    v_refs = refs[group:2 * group]
    o_ref = refs[2 * group]
    s_all, km_t, qbd_f, qbd_b, acc_ref, l_ref = refs[2 * group + 1:]
    ph = pl.program_id(1)
    g = pl.program_id(2)
    n_groups = n_pages // group
    n_rows = N_HEADS * n_new
    page = LANES
    nblk = n_pages * page // MOBA_BLOCK

    row_head = lax.broadcasted_iota(jnp.int32, (n_rows, ATT_WIDTH), 0) // n_new
    col_head = lax.broadcasted_iota(jnp.int32, (n_rows, ATT_WIDTH), 1) // HEAD_DIM
    diag = row_head == col_head

    @pl.when((ph == 0) & (g == 0))
    def _():
        q = q_ref[0]
        qbd = jnp.where(diag, jnp.concatenate([q] * N_HEADS, axis=0), 0.0)
        qbd_f[...] = qbd
        qbd_b[...] = (qbd * SCALE).astype(BF16)
        km_t[...] = jnp.zeros(km_t.shape, F32)

    @pl.when(ph == 0)
    def _():
        lane = lax.broadcasted_iota(jnp.int32, (ATT_WIDTH, LANES), 1)
        qb = qbd_b[...]
        for r in range(0, group, 2):
            kt0 = k_refs[r][0]
            kt1 = k_refs[r + 1][0]
            s_all[g * group + r] = jnp.dot(qb, kt0.astype(BF16), preferred_element_type=F32)
            s_all[g * group + r + 1] = jnp.dot(qb, kt1.astype(BF16), preferred_element_type=F32)
            mean = jnp.sum(kt0 + kt1, axis=1, keepdims=True) * (1.0 / MOBA_BLOCK)
            blk_id = (g * group + r) // 2
            km_t[...] = jnp.where(lane == blk_id, mean, km_t[...])

    @pl.when((ph == 0) & (g == n_groups - 1))
    def _():
        gate = jnp.dot(qbd_f[...], km_t[...], precision=lax.Precision.HIGHEST,
                       preferred_element_type=F32)
        lane = lax.broadcasted_iota(jnp.int32, (n_rows, LANES), 1)
        sel = _select_topk(gate, lane, 1, nblk)
        rh = lax.broadcasted_iota(jnp.int32, (n_rows, 1), 0) // n_new
        far = jnp.zeros((n_rows, 1), F32)
        for h in range(N_HEADS):
            far = jnp.where(rh == h, tbl_ref[N_BUCKETS - 1, h], far)
        mask = jnp.where(sel, far, NEG)
        near = near_ref[...]
        for n in range(nblk):
            if n == nblk - 1:
                seln = jnp.broadcast_to(mask[:, n:n + 1], (n_rows, page)) > 0.5 * NEG
                s_all[2 * n] = s_all[2 * n] + jnp.where(seln, near[:, 0:page], NEG)
                s_all[2 * n + 1] = s_all[2 * n + 1] + jnp.where(seln, near[:, page:2 * page], NEG)
            else:
                add = jnp.broadcast_to(mask[:, n:n + 1], (n_rows, page))
                s_all[2 * n] = s_all[2 * n] + add
                s_all[2 * n + 1] = s_all[2 * n + 1] + add
        kn = jnp.concatenate([kn_ref[0], jnp.zeros((page - n_new, ATT_WIDTH), F32)], axis=0).astype(BF16)
        s_own = lax.dot_general(qbd_b[...], kn, (((1,), (1,)), ((), ())),
                                preferred_element_type=F32) + near[:, 2 * page:3 * page]
        mrun = lax.fori_loop(0, n_pages, lambda j, mm: jnp.maximum(mm, s_all[j]), s_own)
        m = jnp.max(mrun, axis=1, keepdims=True)
        p_own = jnp.exp(s_own - m)

        def exp_step(j, lsum):
            pj = jnp.exp(s_all[j] - m)
            s_all[j] = pj
            return lsum + pj

        lsum = lax.fori_loop(0, n_pages, exp_step, p_own)
        l_ref[...] = jnp.broadcast_to(jnp.sum(lsum, axis=1, keepdims=True), l_ref.shape)
        vn = jnp.concatenate([vn_ref[0], jnp.zeros((page - n_new, ATT_WIDTH), F32)], axis=0).astype(BF16)
        acc_ref[...] = jnp.dot(p_own.astype(BF16), vn, preferred_element_type=F32)

    @pl.when(ph == 1)
    def _():
        acc = acc_ref[...]
        for r in range(group):
            pb = s_all[g * group + r].astype(BF16)
            vt = v_refs[r][0].astype(BF16)
            acc = acc + lax.dot_general(pb, vt, (((1,), (1,)), ((), ())), preferred_element_type=F32)
        acc_ref[...] = acc

    @pl.when((ph == 1) & (g == n_groups - 1))
    def _():
        o = jnp.where(diag, acc_ref[...] / l_ref[:, 0:1], 0.0)
        o_ref[0] = jnp.sum(o.reshape(N_HEADS, n_new, ATT_WIDTH), axis=0)


def _attn_sample(page_table, rel_bias, q, k_new, v_new, near_s, cache_kt, cache_vt, *, group):
    n_seq, n_new, _ = q.shape
    n_pages = page_table.shape[1]
    n_groups = n_pages // group
    n_rows = N_HEADS * n_new
    page = cache_kt.shape[2]

    def k_map(r):
        return lambda b, ph, g, pt: (pt[b, jnp.where(ph == 0, g, n_groups - 1) * group + r], 0, 0)

    def v_map(r):
        return lambda b, ph, g, pt: (pt[b, jnp.where(ph == 1, g, 0) * group + r], 0, 0)

    new_spec = pl.BlockSpec((1, n_new, ATT_WIDTH), lambda b, ph, g, pt: (b, 0, 0))
    in_specs = ([pl.BlockSpec(memory_space=pltpu.SMEM), new_spec, new_spec, new_spec,
                 pl.BlockSpec((n_rows, 2 * MOBA_BLOCK), lambda b, ph, g, pt: (0, 0))]
                + [pl.BlockSpec((1, ATT_WIDTH, page), k_map(r)) for r in range(group)]
                + [pl.BlockSpec((1, ATT_WIDTH, page), v_map(r)) for r in range(group)])
    return pl.pallas_call(
        functools.partial(_attn_sample_kernel, n_pages=n_pages, group=group, n_new=n_new),
        grid_spec=pltpu.PrefetchScalarGridSpec(
            num_scalar_prefetch=1,
            grid=(n_seq, 2, n_groups),
            in_specs=in_specs,
            out_specs=new_spec,
            scratch_shapes=[pltpu.VMEM((n_pages, n_rows, page), F32),
                            pltpu.VMEM((ATT_WIDTH, LANES), F32),
                            pltpu.VMEM((n_rows, ATT_WIDTH), F32),
                            pltpu.VMEM((n_rows, ATT_WIDTH), BF16),
                            pltpu.VMEM((n_rows, ATT_WIDTH), F32),
                            pltpu.VMEM((n_rows, LANES), F32)]),
        out_shape=jax.ShapeDtypeStruct((n_seq, n_new, ATT_WIDTH), F32),
        compiler_params=pltpu.CompilerParams(dimension_semantics=("arbitrary",) * 3,
                                             vmem_limit_bytes=VMEM_LIMIT),
        name="attn_sample",
    )(page_table, rel_bias, q, k_new, v_new, near_s, *([cache_kt] * group), *([cache_vt] * group))


def _mix_kernel(att_ref, glu_ref, x_ref, st_ref, cw_ref, cb_ref, lg_ref, lb_ref, wo_ref, gpm_ref, gpf_ref,
                x1_ref, hn_ref, gbuf, cvbuf, *, tm, stride, halo, chunk):
    i = pl.program_id(1)

    @pl.when(i == 0)
    def _():
        gbuf[0:halo, :] = st_ref[0]

    gbuf[halo:halo + tm, :] = glu_ref[0]
    off0 = halo - (CONV_KERNEL - 1) * stride
    for c in range(tm // chunk):
        r0 = c * chunk
        acc = jnp.broadcast_to(cb_ref[...], (chunk, CONV_WIDTH))
        for j in range(CONV_KERNEL):
            a = off0 + j * stride + r0
            acc = acc + cw_ref[j:j + 1, :] * gbuf[a:a + chunk, :]
        mu = jnp.mean(acc, axis=-1, keepdims=True)
        d = acc - mu
        var = jnp.mean(d * d, axis=-1, keepdims=True)
        y = d * lax.rsqrt(var + EPS) * lg_ref[...] + lb_ref[...]
        cvbuf[r0:r0 + chunk, :] = (y * jax.nn.sigmoid(y)).astype(BF16)

    mix = (jnp.dot(att_ref[0].astype(BF16), wo_ref[0:ATT_WIDTH, :], preferred_element_type=F32)
           + jnp.dot(cvbuf[...], wo_ref[ATT_WIDTH:, :], preferred_element_type=F32))
    x1 = x_ref[0] + _rms(mix, gpm_ref[...])
    x1_ref[0] = x1
    hn_ref[0] = _rms(x1, gpf_ref[...]).astype(BF16)

    if tm >= halo:
        gbuf[0:halo, :] = gbuf[tm:tm + halo, :]


def _mix(att, glu, x, state, cw, cb, lg, lb, wo_bf, gpm, gpf, *, tm, stride):
    n_seq, rows, _ = x.shape
    halo = state.shape[1]
    assert rows == tm or tm >= halo
    row_spec = lambda width: pl.BlockSpec((1, tm, width), lambda s, i: (s, i, 0))
    const = lambda shape: pl.BlockSpec(shape, lambda s, i: (0,) * len(shape))
    return pl.pallas_call(
        functools.partial(_mix_kernel, tm=tm, stride=stride, halo=halo, chunk=min(tm, 32)),
        grid=(n_seq, rows // tm),
        in_specs=[row_spec(ATT_WIDTH), row_spec(CONV_WIDTH), row_spec(D_MODEL),
                  pl.BlockSpec((1, halo, CONV_WIDTH), lambda s, i: (s, 0, 0)),
                  const((CONV_KERNEL, CONV_WIDTH)), const((1, CONV_WIDTH)), const((1, CONV_WIDTH)),
                  const((1, CONV_WIDTH)), const((D_MODEL, D_MODEL)), const((1, D_MODEL)), const((1, D_MODEL))],
        out_specs=[row_spec(D_MODEL), row_spec(D_MODEL)],
        out_shape=[jax.ShapeDtypeStruct((n_seq, rows, D_MODEL), F32),
                   jax.ShapeDtypeStruct((n_seq, rows, D_MODEL), BF16)],
        scratch_shapes=[pltpu.VMEM((halo + tm, CONV_WIDTH), F32), pltpu.VMEM((tm, CONV_WIDTH), BF16)],
        compiler_params=pltpu.CompilerParams(dimension_semantics=("arbitrary", "arbitrary"),
                                             vmem_limit_bytes=VMEM_LIMIT),
        name="mix_s%d" % stride,
    )(att, glu, x, state, cw, cb, lg, lb, wo_bf, gpm, gpf)


def _ffn_kernel(hn_ref, x1_ref, wg_ref, wv_ref, wd_ref, dwg_ref, dwv_ref, dbg_ref, dbv_ref, stg_ref, stv_ref,
                gpo_ref, y_ref, tg_ref, tv_ref, ubuf_g, ubuf_v, car_g, car_v, acc_ref, *, tm, stride, halo, nc):
    i = pl.program_id(1)
    c = pl.program_id(2)
    hn = hn_ref[0]

    def branch(w_ref, dw_ref, db_ref, st_ref, tail_ref, ubuf, car):
        u = jnp.dot(hn, w_ref[...], preferred_element_type=F32)

        @pl.when(i == 0)
        def _():
            ubuf[0:halo, :] = st_ref[0]

        @pl.when(i > 0)
        def _():
            ubuf[0:halo, :] = car[c]

        ubuf[halo:halo + tm, :] = u
        last = ubuf[tm:tm + halo, :]
        car[c] = last
        tail_ref[0, 0] = last
        y = (dw_ref[0:1, :] * ubuf[halo - 2 * stride:halo - 2 * stride + tm, :]
             + dw_ref[1:2, :] * ubuf[halo - stride:halo - stride + tm, :]
             + dw_ref[2:3, :] * u + db_ref[...])
        return y

    yg = branch(wg_ref, dwg_ref, dbg_ref, stg_ref, tg_ref, ubuf_g, car_g)
    yv = branch(wv_ref, dwv_ref, dbv_ref, stv_ref, tv_ref, ubuf_v, car_v)
    a = jax.nn.gelu(yg, approximate=True) * yv
    part = jnp.dot(a.astype(BF16), wd_ref[...], preferred_element_type=F32)

    @pl.when(c == 0)
    def _():
        acc_ref[...] = part

    @pl.when(c > 0)
    def _():
        acc_ref[...] = acc_ref[...] + part

    @pl.when(c == nc - 1)
    def _():
        y_ref[0] = x1_ref[0] + _rms(acc_ref[...], gpo_ref[...])


def _ffn(hn, x1, wup_bf, wdn_bf, dw, db, state, gpo, *, tm, stride, nc):
    n_seq, rows, _ = x1.shape
    halo = state.shape[1]
    ck = D_FF // nc
    row_spec = lambda width: pl.BlockSpec((1, tm, width), lambda s, i, c: (s, i, 0))
    gate_cols = lambda shape: pl.BlockSpec(shape, lambda s, i, c: (0, c))
    val_cols = lambda shape: pl.BlockSpec(shape, lambda s, i, c: (0, c + nc))
    return pl.pallas_call(
        functools.partial(_ffn_kernel, tm=tm, stride=stride, halo=halo, nc=nc),
        grid=(n_seq, rows // tm, nc),
        in_specs=[row_spec(D_MODEL), row_spec(D_MODEL),
                  gate_cols((D_MODEL, ck)), val_cols((D_MODEL, ck)),
                  pl.BlockSpec((ck, D_MODEL), lambda s, i, c: (c, 0)),
                  gate_cols((FFN_KERNEL, ck)), val_cols((FFN_KERNEL, ck)),
                  gate_cols((1, ck)), val_cols((1, ck)),
                  pl.BlockSpec((1, halo, ck), lambda s, i, c: (s, 0, c)),
                  pl.BlockSpec((1, halo, ck), lambda s, i, c: (s, 0, c + nc)),
                  pl.BlockSpec((1, D_MODEL), lambda s, i, c: (0, 0))],
        out_specs=[row_spec(D_MODEL),
                   pl.BlockSpec((1, 1, halo, ck), lambda s, i, c: (s, i, 0, c)),
                   pl.BlockSpec((1, 1, halo, ck), lambda s, i, c: (s, i, 0, c))],
        out_shape=[jax.ShapeDtypeStruct((n_seq, rows, D_MODEL), F32),
                   jax.ShapeDtypeStruct((n_seq, rows // tm, halo, D_FF), F32),
                   jax.ShapeDtypeStruct((n_seq, rows // tm, halo, D_FF), F32)],
        scratch_shapes=[pltpu.VMEM((halo + tm, ck), F32), pltpu.VMEM((halo + tm, ck), F32),
                        pltpu.VMEM((nc, halo, ck), F32), pltpu.VMEM((nc, halo, ck), F32),
                        pltpu.VMEM((tm, D_MODEL), F32)],
        compiler_params=pltpu.CompilerParams(dimension_semantics=("arbitrary",) * 3,
                                             vmem_limit_bytes=VMEM_LIMIT),
        name="ffn_s%d" % stride,
    )(hn, x1, wup_bf, wup_bf, wdn_bf, dw, dw, db, db, state, state, gpo)


def _tile(rows, want):
    return want if rows % want == 0 else rows


def kernel(x_prompt, x_sample, cache_k, cache_v, state_conv, state_ffn, page_table, rel_bias, g_pre_mix, w_in,
           conv_dw_w, conv_dw_b, conv_ln_g, conv_ln_b, w_out, g_post_mix, g_pre_ffn, w_ffn_up, ffn_dw_w,
           ffn_dw_b, w_ffn_down, g_post_ffn):
    depth = w_in.shape[0]
    assert depth == 1, "single-layer trunk"
    bp, seq, _ = x_prompt.shape
    bs, n_new, _ = x_sample.shape
    n_pool, page = cache_k.shape[1], cache_k.shape[2]
    n_pages = page_table.shape[1]
    assert seq % MOBA_BLOCK == 0 and page == LANES and (n_pages * page) % MOBA_BLOCK == 0
    assert n_new == SUBLANES and n_pages * page // MOBA_BLOCK <= LANES

    w_in_bf = w_in[0].astype(BF16)
    w_out_bf = w_out[0].astype(BF16)
    w_up_bf = w_ffn_up[0].astype(BF16)
    w_dn_bf = w_ffn_down[0].astype(BF16)
    cw, cb = conv_dw_w[0], conv_dw_b
    near_t, near_s = _bias_tiles(rel_bias)
    conv_halo_p = 32
    ffn_halo_p = SUBLANES

    tm_p = _tile(seq, 512)
    qt, kt, vt, kbf, vtb, kmean, glu_p = _proj(x_prompt, g_pre_mix, w_in_bf, prompt=True, tm=tm_p)
    att_p = _attn_prompt(qt, kbf, vtb, kmean.reshape(bp, seq // MOBA_BLOCK, ATT_WIDTH), near_t)
    x1_p, hn_p = _mix(att_p, glu_p, x_prompt, jnp.zeros((bp, conv_halo_p, CONV_WIDTH), F32), cw, cb,
                      conv_ln_g, conv_ln_b, w_out_bf, g_post_mix, g_pre_ffn, tm=tm_p, stride=1)
    y_prompt, tail_g, tail_v = _ffn(hn_p, x1_p, w_up_bf, w_dn_bf, ffn_dw_w[0], ffn_dw_b,
                                    jnp.zeros((bp, ffn_halo_p, 2 * D_FF), F32), g_post_ffn,
                                    tm=tm_p, stride=1, nc=2)
    k_prompt = kt.reshape(1, bp, N_HEADS, HEAD_DIM, seq).transpose(0, 1, 4, 2, 3)
    v_prompt = vt.reshape(1, bp, N_HEADS, HEAD_DIM, seq).transpose(0, 1, 4, 2, 3)
    conv_prompt = glu_p[:, seq - (CONV_KERNEL - 1):, :][None]
    ffn_prompt = jnp.concatenate([tail_g[:, -1], tail_v[:, -1]],
                                 axis=-1)[:, ffn_halo_p - (FFN_KERNEL - 1):, :][None]

    rows_s = n_new * bs
    to_tb = lambda a: a.transpose(1, 0, 2).reshape(1, rows_s, a.shape[-1])
    to_bt = lambda a: a.reshape(n_new, bs, a.shape[-1]).transpose(1, 0, 2)
    xs = to_tb(x_sample)
    q_s, k_s, v_s, glu_s = _proj(xs, g_pre_mix, w_in_bf, prompt=False, tm=rows_s)
    q_b, k_b, v_b = to_bt(q_s), to_bt(k_s), to_bt(v_s)
    cache_kt = cache_k[0].transpose(0, 2, 3, 1).reshape(n_pool, ATT_WIDTH, page)
    cache_vt = cache_v[0].transpose(0, 2, 3, 1).reshape(n_pool, ATT_WIDTH, page)
    att_b = _attn_sample(page_table, rel_bias, q_b, k_b, v_b, near_s, cache_kt, cache_vt,
                         group=min(16, n_pages))
    conv_state = state_conv[0].transpose(1, 0, 2).reshape(1, (CONV_KERNEL - 1) * bs, CONV_WIDTH)
    x1_s, hn_s = _mix(to_tb(att_b), glu_s, xs, conv_state, cw, cb, conv_ln_g, conv_ln_b, w_out_bf,
                      g_post_mix, g_pre_ffn, tm=rows_s, stride=bs)
    ffn_state = state_ffn[0].transpose(1, 0, 2).reshape(1, (FFN_KERNEL - 1) * bs, 2 * D_FF)
    y_s, tail_gs, tail_vs = _ffn(hn_s, x1_s, w_up_bf, w_dn_bf, ffn_dw_w[0], ffn_dw_b, ffn_state, g_post_ffn,
                                 tm=rows_s, stride=bs, nc=2)
    y_sample = to_bt(y_s)
    k_sample = k_b.reshape(1, bs, n_new, N_HEADS, HEAD_DIM)
    v_sample = v_b.reshape(1, bs, n_new, N_HEADS, HEAD_DIM)
    conv_all = jnp.concatenate([conv_state[0], glu_s[0]], axis=0)[n_new * bs:]
    conv_sample = conv_all.reshape(CONV_KERNEL - 1, bs, CONV_WIDTH).transpose(1, 0, 2)[None]
    ffn_sample = jnp.concatenate([tail_gs[:, -1], tail_vs[:, -1]], axis=-1).reshape(
        FFN_KERNEL - 1, bs, 2 * D_FF).transpose(1, 0, 2)[None]

    return (y_prompt, y_sample, k_prompt, v_prompt, conv_prompt, ffn_prompt,
            k_sample, v_sample, conv_sample, ffn_sample)
```

```python
import functools
import math

import numpy as np
import jax
import jax.numpy as jnp
from jax import lax
from jax.experimental import pallas as pl
from jax.experimental.pallas import tpu as pltpu

F32 = jnp.float32
BF16 = jnp.bfloat16

D_MODEL = 1024
HEAD_DIM = 64
ATT_WIDTH = 512
N_HEADS = 8
N_PAIRS = N_HEADS // 2
CONV_WIDTH = 512
CONV_KERNEL = 31
MOBA_BLOCK = 256
MOBA_TOPK = 3
N_BUCKETS = 32
MAX_DISTANCE = 128
D_FF = 2816
FFN_KERNEL = 3
EPS = 1e-6
SCALE = HEAD_DIM ** -0.5
NEG = -1e30
LANES = 128
SUBLANES = 8
VMEM_LIMIT = 56 * 1024 * 1024
MXU_COLS = 256
ONES_ROWS = 16
V_ROWS = HEAD_DIM + ONES_ROWS


def _bucket_thresholds():
    n = np.arange(0, 4 * MAX_DISTANCE)
    max_exact = N_BUCKETS // 2
    nf = np.maximum(n, 1).astype(np.float64)
    large = max_exact + (np.log(nf / max_exact) / math.log(MAX_DISTANCE / max_exact)
                         * (N_BUCKETS - max_exact)).astype(np.int64)
    bucket = np.where(n < max_exact, n, np.minimum(large, N_BUCKETS - 1))
    return tuple(int(np.argmax(bucket >= k)) for k in range(1, N_BUCKETS))


_BUCKET_THR = _bucket_thresholds()


def _rms(x, g):
    return x * lax.rsqrt(jnp.mean(x * x, axis=-1, keepdims=True) + EPS) * g


def _bias_kernel(tbl_ref, near_t_ref, near_s_ref):
    p = pl.program_id(0)

    def bias_of(dist, h):
        b = jnp.full(dist.shape, tbl_ref[0, h], F32)
        for k in range(1, N_BUCKETS):
            b = jnp.where(dist >= _BUCKET_THR[k - 1], tbl_ref[k, h], b)
        return jnp.where(dist >= 0, b, NEG)

    jj = lax.broadcasted_iota(jnp.int32, (2 * MOBA_BLOCK, MOBA_BLOCK), 0)
    ii = lax.broadcasted_iota(jnp.int32, (2 * MOBA_BLOCK, MOBA_BLOCK), 1)
    dist = MOBA_BLOCK + ii - jj
    for half in range(2):
        h = 2 * p + half
        near_t_ref[0, :, half * MOBA_BLOCK:(half + 1) * MOBA_BLOCK] = (
            bias_of(dist, h) - tbl_ref[N_BUCKETS - 1, h])

    @pl.when(p == 0)
    def _():
        tt = lax.broadcasted_iota(jnp.int32, (SUBLANES, 2 * MOBA_BLOCK), 0)
        j2 = lax.broadcasted_iota(jnp.int32, (SUBLANES, 2 * MOBA_BLOCK), 1)
        d2 = MOBA_BLOCK + tt - j2
        for h in range(N_HEADS):
            near_s_ref[h * SUBLANES:(h + 1) * SUBLANES, :] = bias_of(d2, h)


def _bias_tiles(rel_bias):
    return pl.pallas_call(
        _bias_kernel,
        grid=(N_PAIRS,),
        in_specs=[pl.BlockSpec(memory_space=pltpu.SMEM)],
        out_specs=[pl.BlockSpec((1, 2 * MOBA_BLOCK, 2 * MOBA_BLOCK), lambda p: (p, 0, 0)),
                   pl.BlockSpec((N_HEADS * SUBLANES, 2 * MOBA_BLOCK), lambda p: (0, 0))],
        out_shape=[jax.ShapeDtypeStruct((N_PAIRS, 2 * MOBA_BLOCK, 2 * MOBA_BLOCK), F32),
                   jax.ShapeDtypeStruct((N_HEADS * SUBLANES, 2 * MOBA_BLOCK), F32)],
        compiler_params=pltpu.CompilerParams(dimension_semantics=("arbitrary",)),
        name="bias_tiles",
    )(rel_bias)


def _proj_kernel(x_ref, g_ref, w_ref, *out_refs, prompt, tm):
    h = _rms(x_ref[0], g_ref[...])
    p = jnp.dot(h.astype(BF16), w_ref[...], preferred_element_type=F32)
    q = p[:, 0:ATT_WIDTH]
    k = p[:, ATT_WIDTH:2 * ATT_WIDTH]
    v = p[:, 2 * ATT_WIDTH:3 * ATT_WIDTH]
    ga = p[:, 3 * ATT_WIDTH:3 * ATT_WIDTH + CONV_WIDTH]
    gb = p[:, 3 * ATT_WIDTH + CONV_WIDTH:]
    glu = ga * jax.nn.sigmoid(gb)
    if prompt:
        qt_ref, kt_ref, vt_ref, kbf_ref, vtb_ref, km_ref, glu_ref = out_refs
        qt_ref[0] = q.T
        kt_ref[0] = k.T
        vt = v.T
        vt_ref[0] = vt
        row_blk = (pl.program_id(1) * (tm // MOBA_BLOCK)
                   + lax.broadcasted_iota(jnp.int32, (tm, LANES), 0) // MOBA_BLOCK)
        onehot = jnp.where(lax.broadcasted_iota(jnp.int32, (tm, LANES), 1) == row_blk, 1.0, 0.0).astype(BF16)
        kb = k.astype(BF16)
        for pr in range(N_PAIRS):
            kbf_ref[0, :, 2 * pr * LANES:(2 * pr + 1) * LANES] = kb[:, pr * LANES:(pr + 1) * LANES]
            kbf_ref[0, :, (2 * pr + 1) * LANES:(2 * pr + 2) * LANES] = onehot
        ones = jnp.ones((ONES_ROWS, MOBA_BLOCK), F32)
        for c in range(tm // MOBA_BLOCK):
            vc = vt[:, c * MOBA_BLOCK:(c + 1) * MOBA_BLOCK]
            parts = []
            for hd in range(N_HEADS):
                parts += [vc[hd * HEAD_DIM:(hd + 1) * HEAD_DIM], ones]
            vtb_ref[0, c] = jnp.concatenate(parts, axis=0).astype(BF16)
        km_ref[0] = jnp.mean(k.reshape(tm // MOBA_BLOCK, MOBA_BLOCK, ATT_WIDTH), axis=1)[:, None, :]
        glu_ref[0] = glu
    else:
        q_ref, k_ref, v_ref, glu_ref = out_refs
        q_ref[0] = q
        k_ref[0] = k
        v_ref[0] = v
        glu_ref[0] = glu


def _proj(x, g, w_bf, *, prompt, tm):
    n_seq, rows, _ = x.shape
    n_out = w_bf.shape[1]
    grid = (n_seq, rows // tm)
    row_spec = lambda width: pl.BlockSpec((1, tm, width), lambda s, i: (s, i, 0))
    col_spec = pl.BlockSpec((1, ATT_WIDTH, tm), lambda s, i: (s, 0, i))
    if prompt:
        nb = tm // MOBA_BLOCK
        out_specs = [col_spec, col_spec, col_spec, row_spec(2 * ATT_WIDTH),
                     pl.BlockSpec((1, nb, N_HEADS * V_ROWS, MOBA_BLOCK), lambda s, i: (s, i, 0, 0)),
                     pl.BlockSpec((1, nb, 1, ATT_WIDTH), lambda s, i: (s, i, 0, 0)),
                     row_spec(CONV_WIDTH)]
        t_shape = jax.ShapeDtypeStruct((n_seq, ATT_WIDTH, rows), F32)
        out_shape = [t_shape, t_shape, t_shape,
                     jax.ShapeDtypeStruct((n_seq, rows, 2 * ATT_WIDTH), BF16),
                     jax.ShapeDtypeStruct((n_seq, rows // MOBA_BLOCK, N_HEADS * V_ROWS, MOBA_BLOCK), BF16),
                     jax.ShapeDtypeStruct((n_seq, rows // MOBA_BLOCK, 1, ATT_WIDTH), F32),
                     jax.ShapeDtypeStruct((n_seq, rows, CONV_WIDTH), F32)]
    else:
        out_specs = [row_spec(ATT_WIDTH)] * 3 + [row_spec(CONV_WIDTH)]
        out_shape = [jax.ShapeDtypeStruct((n_seq, rows, ATT_WIDTH), F32)] * 3 + [
            jax.ShapeDtypeStruct((n_seq, rows, CONV_WIDTH), F32)]
    return pl.pallas_call(
        functools.partial(_proj_kernel, prompt=prompt, tm=tm),
        grid=grid,
        in_specs=[row_spec(D_MODEL),
                  pl.BlockSpec((1, D_MODEL), lambda s, i: (0, 0)),
                  pl.BlockSpec((D_MODEL, n_out), lambda s, i: (0, 0))],
        out_specs=out_specs,
        out_shape=out_shape,
        compiler_params=pltpu.CompilerParams(dimension_semantics=("arbitrary", "arbitrary"),
                                             vmem_limit_bytes=VMEM_LIMIT),
        name="proj_prompt" if prompt else "proj_sample",
    )(x, g, w_bf)


def _select_topk(gate, idx, axis, n_valid):
    big = jnp.int32(2 ** 30)
    g = jnp.where(idx < n_valid, gate, -jnp.inf)
    sel = jnp.zeros(gate.shape, jnp.bool_)
    for _ in range(MOBA_TOPK):
        mx = jnp.max(g, axis=axis, keepdims=True)
        first = jnp.min(jnp.where(g == mx, idx, big), axis=axis, keepdims=True)
        hit = idx == first
        sel = sel | (hit & (mx > -jnp.inf))
        g = jnp.where(hit, -jnp.inf, g)
    return sel


def _attn_prompt_kernel(qt_ref, k_ref, vt_ref, km_ref, near_ref, o_ref, qaug_ref, sa_ref, sb_ref, *, nblk):
    i = pl.program_id(2)
    blk = MOBA_BLOCK
    qt = qt_ref[0]
    z = jnp.zeros((HEAD_DIM, blk), F32)
    q2t = jnp.concatenate([jnp.concatenate([qt[:HEAD_DIM], z], axis=1),
                           jnp.concatenate([z, qt[HEAD_DIM:]], axis=1)], axis=0)

    gate = jnp.dot(km_ref[0], q2t, precision=lax.Precision.HIGHEST, preferred_element_type=F32)
    n_idx = lax.broadcasted_iota(jnp.int32, (nblk, 2 * blk), 0)
    sel = _select_topk(gate, n_idx, 0, i)
    mrows = jnp.concatenate([jnp.where(sel | (n_idx == i), 0.0, NEG),
                             jnp.full((LANES - nblk, 2 * blk), NEG, F32)], axis=0)
    qaug_ref[0:LANES, :] = (q2t * SCALE).astype(BF16)
    qaug_ref[LANES:, :] = mrows.astype(BF16)

    def scores_into(dst_ref, half, j, bias):
        kb = k_ref[0, pl.ds(pl.multiple_of(j * blk, blk), blk), :]
        s = jnp.dot(kb, qaug_ref[...], preferred_element_type=F32)
        dst_ref[half] = s if bias is None else s + bias

    def softmax_pv(src_ref, j0, j1, carry):
        m, a_a, a_b = carry
        s0, s1 = src_ref[0], src_ref[1]
        m_new = jnp.maximum(m, jnp.maximum(jnp.max(s0, axis=0, keepdims=True),
                                           jnp.max(s1, axis=0, keepdims=True)))
        alpha = jnp.exp(m - m_new)
        p0 = jnp.exp(s0 - m_new).astype(BF16)
        p1 = jnp.exp(s1 - m_new).astype(BF16)
        v0, v1 = vt_ref[0, j0], vt_ref[0, j1]
        d_a = (jnp.dot(v0[:V_ROWS], p0[:, :blk], preferred_element_type=F32)
               + jnp.dot(v1[:V_ROWS], p1[:, :blk], preferred_element_type=F32))
        d_b = (jnp.dot(v0[V_ROWS:], p0[:, blk:], preferred_element_type=F32)
               + jnp.dot(v1[V_ROWS:], p1[:, blk:], preferred_element_type=F32))
        return m_new, a_a * alpha[:, :blk] + d_a, a_b * alpha[:, blk:] + d_b

    none = nblk - 1
    prev = jnp.where(i > 0, i - 1, none)
    n_far = jnp.maximum(i - 1, 0)
    zero = jnp.zeros((V_ROWS, blk), F32)
    carry = (jnp.full((1, 2 * blk), NEG, F32), zero, zero)

    scores_into(sa_ref, 0, i, near_ref[0, blk:, :])
    scores_into(sa_ref, 1, prev, near_ref[0, :blk, :])

    def half_step(dst_ref, src_ref, t, state):
        carry, jc0, jc1 = state
        j0 = 2 * t
        j1 = jnp.where(j0 + 1 < n_far, j0 + 1, none)
        scores_into(dst_ref, 0, j0, None)
        scores_into(dst_ref, 1, j1, None)
        return softmax_pv(src_ref, jc0, jc1, carry), j0, j1

    def step(t, state):
        return lax.cond(t % 2 == 0,
                        lambda st: half_step(sb_ref, sa_ref, t, st),
                        lambda st: half_step(sa_ref, sb_ref, t, st), state)

    n_pairs = (n_far + 1) // 2
    carry, jc0, jc1 = lax.fori_loop(0, n_pairs, step, (carry, i, prev))
    _, a_a, a_b = lax.cond(n_pairs % 2 == 0,
                           lambda c: softmax_pv(sa_ref, jc0, jc1, c),
                           lambda c: softmax_pv(sb_ref, jc0, jc1, c), carry)
    out_t = jnp.concatenate([a_a[:HEAD_DIM] / a_a[HEAD_DIM:HEAD_DIM + 1],
                             a_b[:HEAD_DIM] / a_b[HEAD_DIM:HEAD_DIM + 1]], axis=0)
    o_ref[0] = out_t.T


def _attn_prompt(qt, kaug, vtb, kmean, near_t):
    n_seq, _, t = qt.shape
    nblk = t // MOBA_BLOCK
    assert nblk % 2 == 0 and nblk < LANES
    return pl.pallas_call(
        functools.partial(_attn_prompt_kernel, nblk=nblk),
        grid=(n_seq, N_PAIRS, nblk),
        in_specs=[pl.BlockSpec((1, LANES, MOBA_BLOCK), lambda b, p, i: (b, p, i)),
                  pl.BlockSpec((1, t, 2 * LANES), lambda b, p, i: (b, 0, p)),
                  pl.BlockSpec((1, nblk, 2 * V_ROWS, MOBA_BLOCK), lambda b, p, i: (b, 0, p, 0)),
                  pl.BlockSpec((1, nblk, LANES), lambda b, p, i: (b, 0, p)),
                  pl.BlockSpec((1, 2 * MOBA_BLOCK, 2 * MOBA_BLOCK), lambda b, p, i: (p, 0, 0))],
        out_specs=pl.BlockSpec((1, MOBA_BLOCK, LANES), lambda b, p, i: (b, i, p)),
        out_shape=jax.ShapeDtypeStruct((n_seq, t, ATT_WIDTH), F32),
        scratch_shapes=[pltpu.VMEM((2 * LANES, 2 * MOBA_BLOCK), BF16),
                        pltpu.VMEM((2, MOBA_BLOCK, 2 * MOBA_BLOCK), F32),
                        pltpu.VMEM((2, MOBA_BLOCK, 2 * MOBA_BLOCK), F32)],
        compiler_params=pltpu.CompilerParams(dimension_semantics=("arbitrary",) * 3,
                                             vmem_limit_bytes=VMEM_LIMIT),
        name="attn_prompt",
    )(qt, kaug, vtb, kmean, near_t)


def _attn_sample_kernel(pt_ref, tbl_ref, q_ref, kn_ref, vn_ref, near_ref, ck_hbm, cv_hbm, o_ref,
                        s_all, km_t, kbuf, vbuf, ksem, vsem, *, n_pages, nbuf, step, n_new):
    b = pl.program_id(0)
    n_seq = pl.num_programs(0)
    n_rows = N_HEADS * n_new
    page = LANES
    nblk = n_pages * page // MOBA_BLOCK

    def k_copy(seq, p, slot):
        return pltpu.make_async_copy(ck_hbm.at[pt_ref[seq, p]], kbuf.at[slot], ksem.at[slot])

    def v_copy(p, slot):
        return pltpu.make_async_copy(cv_hbm.at[pt_ref[b, p]], vbuf.at[slot], vsem.at[slot])

    @pl.when(b == 0)
    def _():
        for p in range(nbuf):
            k_copy(0, p, p).start()

    row_head = lax.broadcasted_iota(jnp.int32, (n_rows, ATT_WIDTH), 0) // n_new
    col_head = lax.broadcasted_iota(jnp.int32, (n_rows, ATT_WIDTH), 1) // HEAD_DIM
    diag = row_head == col_head
    q = q_ref[0]
    qbd_f = jnp.where(diag, jnp.concatenate([q] * N_HEADS, axis=0), 0.0)
    qbd_b = (qbd_f * SCALE).astype(BF16)
    km_t[...] = jnp.zeros(km_t.shape, F32)
    lane_k = lax.broadcasted_iota(jnp.int32, (ATT_WIDTH, LANES), 1)

    def k_step(it, _):
        p0 = it * step
        slot0 = p0 % nbuf
        for r in range(step):
            k_copy(b, p0 + r, slot0 + r).wait()
        km = km_t[...]
        for r in range(0, step, 2):
            kt0 = kbuf[slot0 + r]
            kt1 = kbuf[slot0 + r + 1]
            s_all[p0 + r] = jnp.dot(qbd_b, kt0.astype(BF16), preferred_element_type=F32)
            s_all[p0 + r + 1] = jnp.dot(qbd_b, kt1.astype(BF16), preferred_element_type=F32)
            mean = jnp.sum(kt0 + kt1, axis=1, keepdims=True) * (1.0 / MOBA_BLOCK)
            km = jnp.where(lane_k == (p0 + r) // 2, mean, km)
        km_t[...] = km

        @pl.when(p0 + nbuf < n_pages)
        def _():
            for r in range(step):
                k_copy(b, p0 + nbuf + r, slot0 + r).start()

        return 0

    lax.fori_loop(0, n_pages // step, k_step, 0)

    @pl.when(b + 1 < n_seq)
    def _():
        for p in range(nbuf):
            k_copy(b + 1, p, p).start()

    for p in range(nbuf):
        v_copy(p, p).start()

    gate = jnp.dot(qbd_f, km_t[...], precision=lax.Precision.HIGHEST,
                   preferred_element_type=F32)
    lane = lax.broadcasted_iota(jnp.int32, (n_rows, LANES), 1)
    sel = _select_topk(gate, lane, 1, nblk)
    rh = lax.broadcasted_iota(jnp.int32, (n_rows, 1), 0) // n_new
    far = jnp.zeros((n_rows, 1), F32)
    for h in range(N_HEADS):
        far = jnp.where(rh == h, tbl_ref[N_BUCKETS - 1, h], far)
    mask = jnp.where(sel, far, NEG)
    near = near_ref[...]
    for n in range(nblk):
        if n == nblk - 1:
            seln = jnp.broadcast_to(mask[:, n:n + 1], (n_rows, page)) > 0.5 * NEG
            s_all[2 * n] = s_all[2 * n] + jnp.where(seln, near[:, 0:page], NEG)
            s_all[2 * n + 1] = s_all[2 * n + 1] + jnp.where(seln, near[:, page:2 * page], NEG)
        else:
            add = jnp.broadcast_to(mask[:, n:n + 1], (n_rows, page))
            s_all[2 * n] = s_all[2 * n] + add
            s_all[2 * n + 1] = s_all[2 * n + 1] + add
    kn = jnp.concatenate([kn_ref[0], jnp.zeros((page - n_new, ATT_WIDTH), F32)], axis=0).astype(BF16)
    s_own = lax.dot_general(qbd_b, kn, (((1,), (1,)), ((), ())),
                            preferred_element_type=F32) + near[:, 2 * page:3 * page]
    mrun = lax.fori_loop(0, n_pages, lambda j, mm: jnp.maximum(mm, s_all[j]), s_own)
    m = jnp.max(mrun, axis=1, keepdims=True)
    p_own = jnp.exp(s_own - m)

    def exp_step(j, lsum):
        pj = jnp.exp(s_all[j] - m)
        s_all[j] = pj
        return lsum + pj

    lsum = lax.fori_loop(0, n_pages, exp_step, p_own)
    l = jnp.sum(lsum, axis=1, keepdims=True)
    vn = jnp.concatenate([vn_ref[0], jnp.zeros((page - n_new, ATT_WIDTH), F32)], axis=0).astype(BF16)
    acc = jnp.dot(p_own.astype(BF16), vn, preferred_element_type=F32)

    def v_step(it, acc):
        p0 = it * step
        slot0 = p0 % nbuf
        for r in range(step):
            v_copy(p0 + r, slot0 + r).wait()
        parts = []
        for r in range(step):
            pb = s_all[p0 + r].astype(BF16)
            vt = vbuf[slot0 + r].astype(BF16)
            parts.append(lax.dot_general(pb, vt, (((1,), (1,)), ((), ())), preferred_element_type=F32))
        while len(parts) > 1:
            parts = [x + y for x, y in zip(parts[0::2], parts[1::2])]

        @pl.when(p0 + nbuf < n_pages)
        def _():
            for r in range(step):
                v_copy(p0 + nbuf + r, slot0 + r).start()

        return acc + parts[0]

    acc = lax.fori_loop(0, n_pages // step, v_step, acc)
    o = jnp.where(diag, acc / l, 0.0)
    o_ref[0] = jnp.sum(o.reshape(N_HEADS, n_new, ATT_WIDTH), axis=0)


def _attn_sample(page_table, rel_bias, q, k_new, v_new, near_s, cache_kt, cache_vt, *, nbuf):
    n_seq, n_new, _ = q.shape
    n_pages = page_table.shape[1]
    n_rows = N_HEADS * n_new
    page = cache_kt.shape[2]
    step = min(8, nbuf)
    assert step & (step - 1) == 0 and step >= 2 and nbuf % step == 0 and n_pages % nbuf == 0
    new_spec = pl.BlockSpec((1, n_new, ATT_WIDTH), lambda b, pt: (b, 0, 0))
    in_specs = [pl.BlockSpec(memory_space=pltpu.SMEM), new_spec, new_spec, new_spec,
                pl.BlockSpec((n_rows, 2 * MOBA_BLOCK), lambda b, pt: (0, 0)),
                pl.BlockSpec(memory_space=pl.ANY), pl.BlockSpec(memory_space=pl.ANY)]
    return pl.pallas_call(
        functools.partial(_attn_sample_kernel, n_pages=n_pages, nbuf=nbuf, step=step, n_new=n_new),
        grid_spec=pltpu.PrefetchScalarGridSpec(
            num_scalar_prefetch=1,
            grid=(n_seq,),
            in_specs=in_specs,
            out_specs=new_spec,
            scratch_shapes=[pltpu.VMEM((n_pages, n_rows, page), F32),
                            pltpu.VMEM((ATT_WIDTH, LANES), F32),
                            pltpu.VMEM((nbuf, ATT_WIDTH, page), F32),
                            pltpu.VMEM((nbuf, ATT_WIDTH, page), F32),
                            pltpu.SemaphoreType.DMA((nbuf,)),
                            pltpu.SemaphoreType.DMA((nbuf,))]),
        out_shape=jax.ShapeDtypeStruct((n_seq, n_new, ATT_WIDTH), F32),
        compiler_params=pltpu.CompilerParams(dimension_semantics=("arbitrary",),
                                             vmem_limit_bytes=VMEM_LIMIT),
        name="attn_sample",
    )(page_table, rel_bias, q, k_new, v_new, near_s, cache_kt, cache_vt)


def _mix_kernel(att_ref, glu_ref, x_ref, st_ref, cw_ref, cb_ref, lg_ref, lb_ref, wo_ref, gpm_ref, gpf_ref,
                x1_ref, hn_ref, gbuf, cvbuf, *, tm, stride, halo, chunk):
    i = pl.program_id(1)

    @pl.when(i == 0)
    def _():
        gbuf[0:halo, :] = st_ref[0]

    gbuf[halo:halo + tm, :] = glu_ref[0]
    off0 = halo - (CONV_KERNEL - 1) * stride
    for c in range(tm // chunk):
        r0 = c * chunk
        acc = jnp.broadcast_to(cb_ref[...], (chunk, CONV_WIDTH))
        for j in range(CONV_KERNEL):
            a = off0 + j * stride + r0
            acc = acc + cw_ref[j:j + 1, :] * gbuf[a:a + chunk, :]
        mu = jnp.mean(acc, axis=-1, keepdims=True)
        d = acc - mu
        var = jnp.mean(d * d, axis=-1, keepdims=True)
        y = d * lax.rsqrt(var + EPS) * lg_ref[...] + lb_ref[...]
        cvbuf[r0:r0 + chunk, :] = (y * jax.nn.sigmoid(y)).astype(BF16)

    mix = (jnp.dot(att_ref[0].astype(BF16), wo_ref[0:ATT_WIDTH, :], preferred_element_type=F32)
           + jnp.dot(cvbuf[...], wo_ref[ATT_WIDTH:, :], preferred_element_type=F32))
    x1 = x_ref[0] + _rms(mix, gpm_ref[...])
    x1_ref[0] = x1
    hn_ref[0] = _rms(x1, gpf_ref[...]).astype(BF16)

    if tm >= halo:
        gbuf[0:halo, :] = gbuf[tm:tm + halo, :]


def _mix(att, glu, x, state, cw, cb, lg, lb, wo_bf, gpm, gpf, *, tm, stride):
    n_seq, rows, _ = x.shape
    halo = state.shape[1]
    assert rows == tm or tm >= halo
    row_spec = lambda width: pl.BlockSpec((1, tm, width), lambda s, i: (s, i, 0))
    const = lambda shape: pl.BlockSpec(shape, lambda s, i: (0,) * len(shape))
    return pl.pallas_call(
        functools.partial(_mix_kernel, tm=tm, stride=stride, halo=halo, chunk=min(tm, 32)),
        grid=(n_seq, rows // tm),
        in_specs=[row_spec(ATT_WIDTH), row_spec(CONV_WIDTH), row_spec(D_MODEL),
                  pl.BlockSpec((1, halo, CONV_WIDTH), lambda s, i: (s, 0, 0)),
                  const((CONV_KERNEL, CONV_WIDTH)), const((1, CONV_WIDTH)), const((1, CONV_WIDTH)),
                  const((1, CONV_WIDTH)), const((D_MODEL, D_MODEL)), const((1, D_MODEL)), const((1, D_MODEL))],
        out_specs=[row_spec(D_MODEL), row_spec(D_MODEL)],
        out_shape=[jax.ShapeDtypeStruct((n_seq, rows, D_MODEL), F32),
                   jax.ShapeDtypeStruct((n_seq, rows, D_MODEL), BF16)],
        scratch_shapes=[pltpu.VMEM((halo + tm, CONV_WIDTH), F32), pltpu.VMEM((tm, CONV_WIDTH), BF16)],
        compiler_params=pltpu.CompilerParams(dimension_semantics=("arbitrary", "arbitrary"),
                                             vmem_limit_bytes=VMEM_LIMIT),
        name="mix_s%d" % stride,
    )(att, glu, x, state, cw, cb, lg, lb, wo_bf, gpm, gpf)


def _ffn_kernel(hn_ref, x1_ref, wg_ref, wv_ref, wd_ref, dwg_ref, dwv_ref, dbg_ref, dbv_ref, stg_ref, stv_ref,
                gpo_ref, y_ref, tg_ref, tv_ref, ubuf_g, ubuf_v, car_g, car_v, acc_ref, *, tm, stride, halo, nc):
    i = pl.program_id(1)
    c = pl.program_id(2)
    hn = hn_ref[0]

    @pl.when(i == 0)
    def _():
        ubuf_g[0:halo, :] = stg_ref[0]
        ubuf_v[0:halo, :] = stv_ref[0]

    @pl.when(i > 0)
    def _():
        ubuf_g[0:halo, :] = car_g[c]
        ubuf_v[0:halo, :] = car_v[c]

    @pl.when(c == 0)
    def _():
        acc_ref[...] = jnp.zeros(acc_ref.shape, F32)

    def up(c0, c1):
        ubuf_g[halo:halo + tm, c0:c1] = jnp.dot(hn, wg_ref[:, c0:c1], preferred_element_type=F32)
        ubuf_v[halo:halo + tm, c0:c1] = jnp.dot(hn, wv_ref[:, c0:c1], preferred_element_type=F32)

    def conv(c0, c1, dw_ref, db_ref, ubuf):
        return (dw_ref[0:1, c0:c1] * ubuf[halo - 2 * stride:halo - 2 * stride + tm, c0:c1]
                + dw_ref[1:2, c0:c1] * ubuf[halo - stride:halo - stride + tm, c0:c1]
                + dw_ref[2:3, c0:c1] * ubuf[halo:halo + tm, c0:c1] + db_ref[:, c0:c1])

    ck = wg_ref.shape[1]
    bounds = [(c0, min(c0 + MXU_COLS, ck)) for c0 in range(0, ck, MXU_COLS)]
    def down(c0, c1, a):
        acc_ref[...] += jnp.dot(a, wd_ref[c0:c1, :], preferred_element_type=F32)

    up(*bounds[0])
    pending = None
    for n, (c0, c1) in enumerate(bounds):
        if n + 1 < len(bounds):
            up(*bounds[n + 1])
        if pending is not None:
            down(*pending)
        a = jax.nn.gelu(conv(c0, c1, dwg_ref, dbg_ref, ubuf_g), approximate=True) * conv(c0, c1, dwv_ref, dbv_ref,
                                                                                        ubuf_v)
        pending = (c0, c1, a.astype(BF16))
    down(*pending)

    for ubuf, car, tail_ref in ((ubuf_g, car_g, tg_ref), (ubuf_v, car_v, tv_ref)):
        last = ubuf[tm:tm + halo, :]
        car[c] = last
        tail_ref[0, 0] = last

    @pl.when(c == nc - 1)
    def _():
        y_ref[0] = x1_ref[0] + _rms(acc_ref[...], gpo_ref[...])


def _ffn(hn, x1, wup_bf, wdn_bf, dw, db, state, gpo, *, tm, stride, nc):
    n_seq, rows, _ = x1.shape
    halo = state.shape[1]
    ck = D_FF // nc
    row_spec = lambda width: pl.BlockSpec((1, tm, width), lambda s, i, c: (s, i, 0))
    gate_cols = lambda shape: pl.BlockSpec(shape, lambda s, i, c: (0, c))
    val_cols = lambda shape: pl.BlockSpec(shape, lambda s, i, c: (0, c + nc))
    return pl.pallas_call(
        functools.partial(_ffn_kernel, tm=tm, stride=stride, halo=halo, nc=nc),
        grid=(n_seq, rows // tm, nc),
        in_specs=[row_spec(D_MODEL), row_spec(D_MODEL),
                  gate_cols((D_MODEL, ck)), val_cols((D_MODEL, ck)),
                  pl.BlockSpec((ck, D_MODEL), lambda s, i, c: (c, 0)),
                  gate_cols((FFN_KERNEL, ck)), val_cols((FFN_KERNEL, ck)),
                  gate_cols((1, ck)), val_cols((1, ck)),
                  pl.BlockSpec((1, halo, ck), lambda s, i, c: (s, 0, c)),
                  pl.BlockSpec((1, halo, ck), lambda s, i, c: (s, 0, c + nc)),
                  pl.BlockSpec((1, D_MODEL), lambda s, i, c: (0, 0))],
        out_specs=[row_spec(D_MODEL),
                   pl.BlockSpec((1, 1, halo, ck), lambda s, i, c: (s, i, 0, c)),
                   pl.BlockSpec((1, 1, halo, ck), lambda s, i, c: (s, i, 0, c))],
        out_shape=[jax.ShapeDtypeStruct((n_seq, rows, D_MODEL), F32),
                   jax.ShapeDtypeStruct((n_seq, rows // tm, halo, D_FF), F32),
                   jax.ShapeDtypeStruct((n_seq, rows // tm, halo, D_FF), F32)],
        scratch_shapes=[pltpu.VMEM((halo + tm, ck), F32), pltpu.VMEM((halo + tm, ck), F32),
                        pltpu.VMEM((nc, halo, ck), F32), pltpu.VMEM((nc, halo, ck), F32),
                        pltpu.VMEM((tm, D_MODEL), F32)],
        compiler_params=pltpu.CompilerParams(dimension_semantics=("arbitrary",) * 3,
                                             vmem_limit_bytes=VMEM_LIMIT),
        name="ffn_s%d" % stride,
    )(hn, x1, wup_bf, wup_bf, wdn_bf, dw, dw, db, db, state, state, gpo)


def _tile(rows, want):
    return want if rows % want == 0 else rows


def kernel(x_prompt, x_sample, cache_k, cache_v, state_conv, state_ffn, page_table, rel_bias, g_pre_mix, w_in,
           conv_dw_w, conv_dw_b, conv_ln_g, conv_ln_b, w_out, g_post_mix, g_pre_ffn, w_ffn_up, ffn_dw_w,
           ffn_dw_b, w_ffn_down, g_post_ffn):
    depth = w_in.shape[0]
    assert depth == 1, "single-layer trunk"
    bp, seq, _ = x_prompt.shape
    bs, n_new, _ = x_sample.shape
    n_pool, page = cache_k.shape[1], cache_k.shape[2]
    n_pages = page_table.shape[1]
    assert seq % MOBA_BLOCK == 0 and page == LANES and (n_pages * page) % MOBA_BLOCK == 0
    assert n_new == SUBLANES and n_pages * page // MOBA_BLOCK <= LANES

    w_in_bf = w_in[0].astype(BF16)
    w_out_bf = w_out[0].astype(BF16)
    w_up_bf = w_ffn_up[0].astype(BF16)
    w_dn_bf = w_ffn_down[0].astype(BF16)
    cw, cb = conv_dw_w[0], conv_dw_b
    near_t, near_s = _bias_tiles(rel_bias)
    conv_halo_p = 32
    ffn_halo_p = SUBLANES

    tm_p = _tile(seq, 512)
    qt, kt, vt, kbf, vtb, kmean, glu_p = _proj(x_prompt, g_pre_mix, w_in_bf, prompt=True, tm=tm_p)
    att_p = _attn_prompt(qt, kbf, vtb, kmean.reshape(bp, seq // MOBA_BLOCK, ATT_WIDTH), near_t)
    x1_p, hn_p = _mix(att_p, glu_p, x_prompt, jnp.zeros((bp, conv_halo_p, CONV_WIDTH), F32), cw, cb,
                      conv_ln_g, conv_ln_b, w_out_bf, g_post_mix, g_pre_ffn, tm=tm_p, stride=1)
    y_prompt, tail_g, tail_v = _ffn(hn_p, x1_p, w_up_bf, w_dn_bf, ffn_dw_w[0], ffn_dw_b,
                                    jnp.zeros((bp, ffn_halo_p, 2 * D_FF), F32), g_post_ffn,
                                    tm=tm_p, stride=1, nc=2)
    k_prompt = kt.reshape(1, bp, N_HEADS, HEAD_DIM, seq).transpose(0, 1, 4, 2, 3)
    v_prompt = vt.reshape(1, bp, N_HEADS, HEAD_DIM, seq).transpose(0, 1, 4, 2, 3)
    conv_prompt = glu_p[:, seq - (CONV_KERNEL - 1):, :][None]
    ffn_prompt = jnp.concatenate([tail_g[:, -1], tail_v[:, -1]],
                                 axis=-1)[:, ffn_halo_p - (FFN_KERNEL - 1):, :][None]

    rows_s = n_new * bs
    to_tb = lambda a: a.transpose(1, 0, 2).reshape(1, rows_s, a.shape[-1])
    to_bt = lambda a: a.reshape(n_new, bs, a.shape[-1]).transpose(1, 0, 2)
    xs = to_tb(x_sample)
    q_s, k_s, v_s, glu_s = _proj(xs, g_pre_mix, w_in_bf, prompt=False, tm=rows_s)
    q_b, k_b, v_b = to_bt(q_s), to_bt(k_s), to_bt(v_s)
    cache_kt = cache_k[0].transpose(0, 2, 3, 1).reshape(n_pool, ATT_WIDTH, page)
    cache_vt = cache_v[0].transpose(0, 2, 3, 1).reshape(n_pool, ATT_WIDTH, page)
    att_b = _attn_sample(page_table, rel_bias, q_b, k_b, v_b, near_s, cache_kt, cache_vt,
                         nbuf=min(32, n_pages))
    conv_state = state_conv[0].transpose(1, 0, 2).reshape(1, (CONV_KERNEL - 1) * bs, CONV_WIDTH)
    x1_s, hn_s = _mix(to_tb(att_b), glu_s, xs, conv_state, cw, cb, conv_ln_g, conv_ln_b, w_out_bf,
                      g_post_mix, g_pre_ffn, tm=rows_s, stride=bs)
    ffn_state = state_ffn[0].transpose(1, 0, 2).reshape(1, (FFN_KERNEL - 1) * bs, 2 * D_FF)
    y_s, tail_gs, tail_vs = _ffn(hn_s, x1_s, w_up_bf, w_dn_bf, ffn_dw_w[0], ffn_dw_b, ffn_state, g_post_ffn,
                                 tm=rows_s, stride=bs, nc=2)
    y_sample = to_bt(y_s)
    k_sample = k_b.reshape(1, bs, n_new, N_HEADS, HEAD_DIM)
    v_sample = v_b.reshape(1, bs, n_new, N_HEADS, HEAD_DIM)
    conv_all = jnp.concatenate([conv_state[0], glu_s[0]], axis=0)[n_new * bs:]
    conv_sample = conv_all.reshape(CONV_KERNEL - 1, bs, CONV_WIDTH).transpose(1, 0, 2)[None]
    ffn_sample = jnp.concatenate([tail_gs[:, -1], tail_vs[:, -1]], axis=-1).reshape(
        FFN_KERNEL - 1, bs, 2 * D_FF).transpose(1, 0, 2)[None]

    return (y_prompt, y_sample, k_prompt, v_prompt, conv_prompt, ffn_prompt,
            k_sample, v_sample, conv_sample, ffn_sample)
```

```python
import functools
import math

import numpy as np
import jax
import jax.numpy as jnp
from jax import lax
from jax.experimental import pallas as pl
from jax.experimental.pallas import tpu as pltpu

F32 = jnp.float32
BF16 = jnp.bfloat16

D_MODEL = 1024
HEAD_DIM = 64
ATT_WIDTH = 512
N_HEADS = 8
N_PAIRS = N_HEADS // 2
CONV_WIDTH = 512
CONV_KERNEL = 31
MOBA_BLOCK = 256
MOBA_TOPK = 3
N_BUCKETS = 32
MAX_DISTANCE = 128
D_FF = 2816
FFN_KERNEL = 3
EPS = 1e-6
SCALE = HEAD_DIM ** -0.5
LOG2E = math.log2(math.e)
NEG = -1e30
LANES = 128
SUBLANES = 8
VMEM_LIMIT = 56 * 1024 * 1024
MXU_COLS = 256
ONES_ROWS = 16
V_ROWS = HEAD_DIM + ONES_ROWS


def _bucket_thresholds():
    n = np.arange(0, 4 * MAX_DISTANCE)
    max_exact = N_BUCKETS // 2
    nf = np.maximum(n, 1).astype(np.float64)
    large = max_exact + (np.log(nf / max_exact) / math.log(MAX_DISTANCE / max_exact)
                         * (N_BUCKETS - max_exact)).astype(np.int64)
    bucket = np.where(n < max_exact, n, np.minimum(large, N_BUCKETS - 1))
    return tuple(int(np.argmax(bucket >= k)) for k in range(1, N_BUCKETS))


_BUCKET_THR = _bucket_thresholds()


def _rms(x, g):
    return x * lax.rsqrt(jnp.mean(x * x, axis=-1, keepdims=True) + EPS) * g


def _bias_kernel(tbl_ref, near_t_ref, near_s_ref):
    p = pl.program_id(0)

    def bias_of(dist, h):
        b = jnp.full(dist.shape, tbl_ref[0, h], F32)
        for k in range(1, N_BUCKETS):
            b = jnp.where(dist >= _BUCKET_THR[k - 1], tbl_ref[k, h], b)
        return jnp.where(dist >= 0, b, NEG)

    jj = lax.broadcasted_iota(jnp.int32, (2 * MOBA_BLOCK, MOBA_BLOCK), 0)
    ii = lax.broadcasted_iota(jnp.int32, (2 * MOBA_BLOCK, MOBA_BLOCK), 1)
    dist = MOBA_BLOCK + ii - jj
    for half in range(2):
        h = 2 * p + half
        near_t_ref[0, :, half * MOBA_BLOCK:(half + 1) * MOBA_BLOCK] = (
            bias_of(dist, h) - tbl_ref[N_BUCKETS - 1, h]) * LOG2E

    @pl.when(p == 0)
    def _():
        tt = lax.broadcasted_iota(jnp.int32, (SUBLANES, 2 * MOBA_BLOCK), 0)
        j2 = lax.broadcasted_iota(jnp.int32, (SUBLANES, 2 * MOBA_BLOCK), 1)
        d2 = MOBA_BLOCK + tt - j2
        for h in range(N_HEADS):
            near_s_ref[h * SUBLANES:(h + 1) * SUBLANES, :] = bias_of(d2, h)


def _bias_tiles(rel_bias):
    return pl.pallas_call(
        _bias_kernel,
        grid=(N_PAIRS,),
        in_specs=[pl.BlockSpec(memory_space=pltpu.SMEM)],
        out_specs=[pl.BlockSpec((1, 2 * MOBA_BLOCK, 2 * MOBA_BLOCK), lambda p: (p, 0, 0)),
                   pl.BlockSpec((N_HEADS * SUBLANES, 2 * MOBA_BLOCK), lambda p: (0, 0))],
        out_shape=[jax.ShapeDtypeStruct((N_PAIRS, 2 * MOBA_BLOCK, 2 * MOBA_BLOCK), F32),
                   jax.ShapeDtypeStruct((N_HEADS * SUBLANES, 2 * MOBA_BLOCK), F32)],
        compiler_params=pltpu.CompilerParams(dimension_semantics=("arbitrary",)),
        name="bias_tiles",
    )(rel_bias)


def _proj_kernel(x_ref, g_ref, w_ref, *out_refs, prompt, tm):
    h = _rms(x_ref[0], g_ref[...])
    p = jnp.dot(h.astype(BF16), w_ref[...], preferred_element_type=F32)
    q = p[:, 0:ATT_WIDTH]
    k = p[:, ATT_WIDTH:2 * ATT_WIDTH]
    v = p[:, 2 * ATT_WIDTH:3 * ATT_WIDTH]
    ga = p[:, 3 * ATT_WIDTH:3 * ATT_WIDTH + CONV_WIDTH]
    gb = p[:, 3 * ATT_WIDTH + CONV_WIDTH:]
    glu = ga * jax.nn.sigmoid(gb)
    if prompt:
        qt_ref, kt_ref, vt_ref, kbf_ref, vtb_ref, km_ref, glu_ref = out_refs
        qt_ref[0] = q.T
        kt_ref[0] = k.T
        vt = v.T
        vt_ref[0] = vt
        row_blk = (pl.program_id(1) * (tm // MOBA_BLOCK)
                   + lax.broadcasted_iota(jnp.int32, (tm, LANES), 0) // MOBA_BLOCK)
        onehot = jnp.where(lax.broadcasted_iota(jnp.int32, (tm, LANES), 1) == row_blk, 1.0, 0.0).astype(BF16)
        kb = k.astype(BF16)
        for pr in range(N_PAIRS):
            kbf_ref[0, :, 2 * pr * LANES:(2 * pr + 1) * LANES] = kb[:, pr * LANES:(pr + 1) * LANES]
            kbf_ref[0, :, (2 * pr + 1) * LANES:(2 * pr + 2) * LANES] = onehot
        ones = jnp.ones((ONES_ROWS, MOBA_BLOCK), F32)
        for c in range(tm // MOBA_BLOCK):
            vc = vt[:, c * MOBA_BLOCK:(c + 1) * MOBA_BLOCK]
            parts = []
            for hd in range(N_HEADS):
                parts += [vc[hd * HEAD_DIM:(hd + 1) * HEAD_DIM], ones]
            vtb_ref[0, c] = jnp.concatenate(parts, axis=0).astype(BF16)
        km_ref[0] = jnp.mean(k.reshape(tm // MOBA_BLOCK, MOBA_BLOCK, ATT_WIDTH), axis=1)[:, None, :]
        glu_ref[0] = glu
    else:
        q_ref, k_ref, v_ref, glu_ref = out_refs
        q_ref[0] = q
        k_ref[0] = k
        v_ref[0] = v
        glu_ref[0] = glu


def _proj(x, g, w_bf, *, prompt, tm):
    n_seq, rows, _ = x.shape
    n_out = w_bf.shape[1]
    grid = (n_seq, rows // tm)
    row_spec = lambda width: pl.BlockSpec((1, tm, width), lambda s, i: (s, i, 0))
    col_spec = pl.BlockSpec((1, ATT_WIDTH, tm), lambda s, i: (s, 0, i))
    if prompt:
        nb = tm // MOBA_BLOCK
        out_specs = [col_spec, col_spec, col_spec, row_spec(2 * ATT_WIDTH),
                     pl.BlockSpec((1, nb, N_HEADS * V_ROWS, MOBA_BLOCK), lambda s, i: (s, i, 0, 0)),
                     pl.BlockSpec((1, nb, 1, ATT_WIDTH), lambda s, i: (s, i, 0, 0)),
                     row_spec(CONV_WIDTH)]
        t_shape = jax.ShapeDtypeStruct((n_seq, ATT_WIDTH, rows), F32)
        out_shape = [t_shape, t_shape, t_shape,
                     jax.ShapeDtypeStruct((n_seq, rows, 2 * ATT_WIDTH), BF16),
                     jax.ShapeDtypeStruct((n_seq, rows // MOBA_BLOCK, N_HEADS * V_ROWS, MOBA_BLOCK), BF16),
                     jax.ShapeDtypeStruct((n_seq, rows // MOBA_BLOCK, 1, ATT_WIDTH), F32),
                     jax.ShapeDtypeStruct((n_seq, rows, CONV_WIDTH), F32)]
    else:
        out_specs = [row_spec(ATT_WIDTH)] * 3 + [row_spec(CONV_WIDTH)]
        out_shape = [jax.ShapeDtypeStruct((n_seq, rows, ATT_WIDTH), F32)] * 3 + [
            jax.ShapeDtypeStruct((n_seq, rows, CONV_WIDTH), F32)]
    return pl.pallas_call(
        functools.partial(_proj_kernel, prompt=prompt, tm=tm),
        grid=grid,
        in_specs=[row_spec(D_MODEL),
                  pl.BlockSpec((1, D_MODEL), lambda s, i: (0, 0)),
                  pl.BlockSpec((D_MODEL, n_out), lambda s, i: (0, 0))],
        out_specs=out_specs,
        out_shape=out_shape,
        compiler_params=pltpu.CompilerParams(dimension_semantics=("arbitrary", "arbitrary"),
                                             vmem_limit_bytes=VMEM_LIMIT),
        name="proj_prompt" if prompt else "proj_sample",
    )(x, g, w_bf)


def _select_topk(gate, idx, axis, n_valid):
    big = jnp.int32(2 ** 30)
    g = jnp.where(idx < n_valid, gate, -jnp.inf)
    sel = jnp.zeros(gate.shape, jnp.bool_)
    for _ in range(MOBA_TOPK):
        mx = jnp.max(g, axis=axis, keepdims=True)
        first = jnp.min(jnp.where(g == mx, idx, big), axis=axis, keepdims=True)
        hit = idx == first
        sel = sel | (hit & (mx > -jnp.inf))
        g = jnp.where(hit, -jnp.inf, g)
    return sel


def _attn_prompt_kernel(qt_ref, k_ref, vt_ref, km_ref, near_ref, o_ref, qaug_ref, sa_ref, sb_ref, pa_ref, pb_ref,
                        *, nblk):
    i = pl.program_id(2)
    blk = MOBA_BLOCK
    qt = qt_ref[0]
    z = jnp.zeros((HEAD_DIM, blk), F32)
    q2t = jnp.concatenate([jnp.concatenate([qt[:HEAD_DIM], z], axis=1),
                           jnp.concatenate([z, qt[HEAD_DIM:]], axis=1)], axis=0)

    gate = jnp.dot(km_ref[0], q2t, precision=lax.Precision.HIGHEST, preferred_element_type=F32)
    n_idx = lax.broadcasted_iota(jnp.int32, (nblk, 2 * blk), 0)
    sel = _select_topk(gate, n_idx, 0, i)
    mrows = jnp.concatenate([jnp.where(sel | (n_idx == i), 0.0, NEG),
                             jnp.full((LANES - nblk, 2 * blk), NEG, F32)], axis=0)
    qaug_ref[0:LANES, :] = (q2t * (SCALE * LOG2E)).astype(BF16)
    qaug_ref[LANES:, :] = mrows.astype(BF16)

    def scores_into(dst_ref, j0, j1, bias):
        mx = None
        for half, j in enumerate((j0, j1)):
            kb = k_ref[0, pl.ds(pl.multiple_of(j * blk, blk), blk), :]
            s = jnp.dot(kb, qaug_ref[...], preferred_element_type=F32)
            if bias is not None:
                s = s + bias[half]
            dst_ref[half] = s
            smax = jnp.max(s, axis=0, keepdims=True)
            mx = smax if mx is None else jnp.maximum(mx, smax)
        return mx

    def probs(src_ref, dst_ref, m, mx):
        m_new = jnp.maximum(m, mx)
        dst_ref[0] = jnp.exp2(src_ref[0] - m_new).astype(BF16)
        dst_ref[1] = jnp.exp2(src_ref[1] - m_new).astype(BF16)
        return m_new, jnp.exp2(m - m_new)

    def accumulate(p_ref, j0, j1, alpha, a_a, a_b):
        p0, p1 = p_ref[0], p_ref[1]
        v0, v1 = vt_ref[0, j0], vt_ref[0, j1]
        d_a = (jnp.dot(v0[:V_ROWS], p0[:, :blk], preferred_element_type=F32)
               + jnp.dot(v1[:V_ROWS], p1[:, :blk], preferred_element_type=F32))
        d_b = (jnp.dot(v0[V_ROWS:], p0[:, blk:], preferred_element_type=F32)
               + jnp.dot(v1[V_ROWS:], p1[:, blk:], preferred_element_type=F32))
        return a_a * alpha[:, :blk] + d_a, a_b * alpha[:, blk:] + d_b

    none = nblk - 1
    prev = jnp.where(i > 0, i - 1, none)
    n_far = jnp.maximum(i - 1, 0)
    n_pairs = (n_far + 1) // 2

    mx = scores_into(sa_ref, i, prev, (near_ref[0, blk:, :], near_ref[0, :blk, :]))
    pb_ref[...] = jnp.zeros(pb_ref.shape, BF16)
    zero = jnp.zeros((V_ROWS, blk), F32)
    state = (jnp.full((1, 2 * blk), NEG, F32), mx, jnp.ones((1, 2 * blk), F32), zero, zero, i, prev, i, i)

    def stage(s_src, s_dst, p_dst, p_src, t, state):
        m, mx, alpha_p, a_a, a_b, js0, js1, jp0, jp1 = state
        j0 = 2 * t
        j1 = jnp.where(j0 + 1 < n_far, j0 + 1, none)
        mx_next = scores_into(s_dst, j0, j1, None)
        m_new, alpha = probs(s_src, p_dst, m, mx)
        a_a, a_b = accumulate(p_src, jp0, jp1, alpha_p, a_a, a_b)
        return m_new, mx_next, alpha, a_a, a_b, j0, j1, js0, js1

    def step(t, state):
        return lax.cond(t % 2 == 0,
                        lambda st: stage(sa_ref, sb_ref, pa_ref, pb_ref, t, st),
                        lambda st: stage(sb_ref, sa_ref, pb_ref, pa_ref, t, st), state)

    def drain(s_src, p_dst, p_src, state):
        m, mx, alpha_p, a_a, a_b, js0, js1, jp0, jp1 = state
        a_a, a_b = accumulate(p_src, jp0, jp1, alpha_p, a_a, a_b)
        _, alpha = probs(s_src, p_dst, m, mx)
        return accumulate(p_dst, js0, js1, alpha, a_a, a_b)

    state = lax.fori_loop(0, n_pairs, step, state)
    a_a, a_b = lax.cond(n_pairs % 2 == 0,
                        lambda st: drain(sa_ref, pa_ref, pb_ref, st),
                        lambda st: drain(sb_ref, pb_ref, pa_ref, st), state)
    out_t = jnp.concatenate([a_a[:HEAD_DIM] / a_a[HEAD_DIM:HEAD_DIM + 1],
                             a_b[:HEAD_DIM] / a_b[HEAD_DIM:HEAD_DIM + 1]], axis=0)
    o_ref[0] = out_t.T


def _attn_prompt(qt, kaug, vtb, kmean, near_t):
    n_seq, _, t = qt.shape
    nblk = t // MOBA_BLOCK
    assert nblk % 2 == 0 and nblk < LANES
    return pl.pallas_call(
        functools.partial(_attn_prompt_kernel, nblk=nblk),
        grid=(n_seq, N_PAIRS, nblk),
        in_specs=[pl.BlockSpec((1, LANES, MOBA_BLOCK), lambda b, p, i: (b, p, i)),
                  pl.BlockSpec((1, t, 2 * LANES), lambda b, p, i: (b, 0, p)),
                  pl.BlockSpec((1, nblk, 2 * V_ROWS, MOBA_BLOCK), lambda b, p, i: (b, 0, p, 0)),
                  pl.BlockSpec((1, nblk, LANES), lambda b, p, i: (b, 0, p)),
                  pl.BlockSpec((1, 2 * MOBA_BLOCK, 2 * MOBA_BLOCK), lambda b, p, i: (p, 0, 0))],
        out_specs=pl.BlockSpec((1, MOBA_BLOCK, LANES), lambda b, p, i: (b, i, p)),
        out_shape=jax.ShapeDtypeStruct((n_seq, t, ATT_WIDTH), F32),
        scratch_shapes=[pltpu.VMEM((2 * LANES, 2 * MOBA_BLOCK), BF16),
                        pltpu.VMEM((2, MOBA_BLOCK, 2 * MOBA_BLOCK), F32),
                        pltpu.VMEM((2, MOBA_BLOCK, 2 * MOBA_BLOCK), F32),
                        pltpu.VMEM((2, MOBA_BLOCK, 2 * MOBA_BLOCK), BF16),
                        pltpu.VMEM((2, MOBA_BLOCK, 2 * MOBA_BLOCK), BF16)],
        compiler_params=pltpu.CompilerParams(dimension_semantics=("arbitrary",) * 3,
                                             vmem_limit_bytes=VMEM_LIMIT),
        name="attn_prompt",
    )(qt, kaug, vtb, kmean, near_t)


def _attn_sample_kernel(pt_ref, tbl_ref, q_ref, kn_ref, vn_ref, near_ref, ck_hbm, cv_hbm, o_ref,
                        s_all, km_t, kbuf, vbuf, ksem, vsem, *, n_pages, nbuf, step, n_new):
    b = pl.program_id(0)
    n_seq = pl.num_programs(0)
    n_rows = N_HEADS * n_new
    page = LANES
    nblk = n_pages * page // MOBA_BLOCK

    def k_copy(seq, p, slot):
        return pltpu.make_async_copy(ck_hbm.at[pt_ref[seq, p]], kbuf.at[slot], ksem.at[slot])

    def v_copy(p, slot):
        return pltpu.make_async_copy(cv_hbm.at[pt_ref[b, p]], vbuf.at[slot], vsem.at[slot])

    @pl.when(b == 0)
    def _():
        for p in range(nbuf):
            k_copy(0, p, p).start()

    row_head = lax.broadcasted_iota(jnp.int32, (n_rows, ATT_WIDTH), 0) // n_new
    col_head = lax.broadcasted_iota(jnp.int32, (n_rows, ATT_WIDTH), 1) // HEAD_DIM
    diag = row_head == col_head
    q = q_ref[0]
    qbd_f = jnp.where(diag, jnp.concatenate([q] * N_HEADS, axis=0), 0.0)
    qbd_b = (qbd_f * SCALE).astype(BF16)
    km_t[...] = jnp.zeros(km_t.shape, F32)
    lane_k = lax.broadcasted_iota(jnp.int32, (ATT_WIDTH, LANES), 1)

    def k_step(it, _):
        p0 = it * step
        slot0 = p0 % nbuf
        for r in range(step):
            k_copy(b, p0 + r, slot0 + r).wait()
        km = km_t[...]
        for r in range(0, step, 2):
            kt0 = kbuf[slot0 + r]
            kt1 = kbuf[slot0 + r + 1]
            s_all[p0 + r] = jnp.dot(qbd_b, kt0.astype(BF16), preferred_element_type=F32)
            s_all[p0 + r + 1] = jnp.dot(qbd_b, kt1.astype(BF16), preferred_element_type=F32)
            mean = jnp.sum(kt0 + kt1, axis=1, keepdims=True) * (1.0 / MOBA_BLOCK)
            km = jnp.where(lane_k == (p0 + r) // 2, mean, km)
        km_t[...] = km

        @pl.when(p0 + nbuf < n_pages)
        def _():
            for r in range(step):
                k_copy(b, p0 + nbuf + r, slot0 + r).start()

        return 0

    lax.fori_loop(0, n_pages // step, k_step, 0)

    @pl.when(b + 1 < n_seq)
    def _():
        for p in range(nbuf):
            k_copy(b + 1, p, p).start()

    for p in range(nbuf):
        v_copy(p, p).start()

    gate = jnp.dot(qbd_f, km_t[...], precision=lax.Precision.HIGHEST,
                   preferred_element_type=F32)
    lane = lax.broadcasted_iota(jnp.int32, (n_rows, LANES), 1)
    sel = _select_topk(gate, lane, 1, nblk)
    rh = lax.broadcasted_iota(jnp.int32, (n_rows, 1), 0) // n_new
    far = jnp.zeros((n_rows, 1), F32)
    for h in range(N_HEADS):
        far = jnp.where(rh == h, tbl_ref[N_BUCKETS - 1, h], far)
    mask = jnp.where(sel, far, NEG)
    near = near_ref[...]
    for n in range(nblk):
        if n == nblk - 1:
            seln = jnp.broadcast_to(mask[:, n:n + 1], (n_rows, page)) > 0.5 * NEG
            s_all[2 * n] = s_all[2 * n] + jnp.where(seln, near[:, 0:page], NEG)
            s_all[2 * n + 1] = s_all[2 * n + 1] + jnp.where(seln, near[:, page:2 * page], NEG)
        else:
            add = jnp.broadcast_to(mask[:, n:n + 1], (n_rows, page))
            s_all[2 * n] = s_all[2 * n] + add
            s_all[2 * n + 1] = s_all[2 * n + 1] + add
    kn = jnp.concatenate([kn_ref[0], jnp.zeros((page - n_new, ATT_WIDTH), F32)], axis=0).astype(BF16)
    s_own = lax.dot_general(qbd_b, kn, (((1,), (1,)), ((), ())),
                            preferred_element_type=F32) + near[:, 2 * page:3 * page]
    mrun = lax.fori_loop(0, n_pages, lambda j, mm: jnp.maximum(mm, s_all[j]), s_own)
    m = jnp.max(mrun, axis=1, keepdims=True)
    p_own = jnp.exp(s_own - m)

    def exp_step(j, lsum):
        pj = jnp.exp(s_all[j] - m)
        s_all[j] = pj
        return lsum + pj

    lsum = lax.fori_loop(0, n_pages, exp_step, p_own)
    l = jnp.sum(lsum, axis=1, keepdims=True)
    vn = jnp.concatenate([vn_ref[0], jnp.zeros((page - n_new, ATT_WIDTH), F32)], axis=0).astype(BF16)
    acc = jnp.dot(p_own.astype(BF16), vn, preferred_element_type=F32)

    def v_step(it, acc):
        p0 = it * step
        slot0 = p0 % nbuf
        for r in range(step):
            v_copy(p0 + r, slot0 + r).wait()
        parts = []
        for r in range(step):
            pb = s_all[p0 + r].astype(BF16)
            vt = vbuf[slot0 + r].astype(BF16)
            parts.append(lax.dot_general(pb, vt, (((1,), (1,)), ((), ())), preferred_element_type=F32))
        while len(parts) > 1:
            parts = [x + y for x, y in zip(parts[0::2], parts[1::2])]

        @pl.when(p0 + nbuf < n_pages)
        def _():
            for r in range(step):
                v_copy(p0 + nbuf + r, slot0 + r).start()

        return acc + parts[0]

    acc = lax.fori_loop(0, n_pages // step, v_step, acc)
    o = jnp.where(diag, acc / l, 0.0)
    o_ref[0] = jnp.sum(o.reshape(N_HEADS, n_new, ATT_WIDTH), axis=0)


def _attn_sample(page_table, rel_bias, q, k_new, v_new, near_s, cache_kt, cache_vt, *, nbuf):
    n_seq, n_new, _ = q.shape
    n_pages = page_table.shape[1]
    n_rows = N_HEADS * n_new
    page = cache_kt.shape[2]
    step = min(8, nbuf)
    assert step & (step - 1) == 0 and step >= 2 and nbuf % step == 0 and n_pages % nbuf == 0
    new_spec = pl.BlockSpec((1, n_new, ATT_WIDTH), lambda b, pt: (b, 0, 0))
    in_specs = [pl.BlockSpec(memory_space=pltpu.SMEM), new_spec, new_spec, new_spec,
                pl.BlockSpec((n_rows, 2 * MOBA_BLOCK), lambda b, pt: (0, 0)),
                pl.BlockSpec(memory_space=pl.ANY), pl.BlockSpec(memory_space=pl.ANY)]
    return pl.pallas_call(
        functools.partial(_attn_sample_kernel, n_pages=n_pages, nbuf=nbuf, step=step, n_new=n_new),
        grid_spec=pltpu.PrefetchScalarGridSpec(
            num_scalar_prefetch=1,
            grid=(n_seq,),
            in_specs=in_specs,
            out_specs=new_spec,
            scratch_shapes=[pltpu.VMEM((n_pages, n_rows, page), F32),
                            pltpu.VMEM((ATT_WIDTH, LANES), F32),
                            pltpu.VMEM((nbuf, ATT_WIDTH, page), F32),
                            pltpu.VMEM((nbuf, ATT_WIDTH, page), F32),
                            pltpu.SemaphoreType.DMA((nbuf,)),
                            pltpu.SemaphoreType.DMA((nbuf,))]),
        out_shape=jax.ShapeDtypeStruct((n_seq, n_new, ATT_WIDTH), F32),
        compiler_params=pltpu.CompilerParams(dimension_semantics=("arbitrary",),
                                             vmem_limit_bytes=VMEM_LIMIT),
        name="attn_sample",
    )(page_table, rel_bias, q, k_new, v_new, near_s, cache_kt, cache_vt)


def _conv_offsets(halo, stride):
    off0 = halo - (CONV_KERNEL - 1) * stride
    return [off0 + j * stride for j in range(CONV_KERNEL)]


def _shift_classes(offs):
    return sorted({o % SUBLANES for o in offs} - {0})


def _mix_kernel(att_ref, glu_ref, x_ref, st_ref, cw_ref, cb_ref, lg_ref, lb_ref, wo_ref, gpm_ref, gpf_ref,
                x1_ref, hn_ref, gbuf, cvbuf, *shifted, tm, stride, halo, chunk):
    i = pl.program_id(1)

    @pl.when(i == 0)
    def _():
        gbuf[0:halo, :] = st_ref[0]

    gbuf[halo:halo + tm, :] = glu_ref[0]
    offs = _conv_offsets(halo, stride)
    for r, ref in zip(_shift_classes(offs), shifted):
        ref[...] = gbuf[r:r + ref.shape[0], :]
    src = dict(zip(_shift_classes(offs), shifted))
    src[0] = gbuf
    for c in range(tm // chunk):
        r0 = c * chunk
        acc = jnp.broadcast_to(cb_ref[...], (chunk, CONV_WIDTH))
        for j, o in enumerate(offs):
            a = o - o % SUBLANES + r0
            acc = acc + cw_ref[j:j + 1, :] * src[o % SUBLANES][a:a + chunk, :]
        mu = jnp.mean(acc, axis=-1, keepdims=True)
        d = acc - mu
        var = jnp.mean(d * d, axis=-1, keepdims=True)
        y = d * lax.rsqrt(var + EPS) * lg_ref[...] + lb_ref[...]
        cvbuf[r0:r0 + chunk, :] = (y * jax.nn.sigmoid(y)).astype(BF16)

    mix = (jnp.dot(att_ref[0].astype(BF16), wo_ref[0:ATT_WIDTH, :], preferred_element_type=F32)
           + jnp.dot(cvbuf[...], wo_ref[ATT_WIDTH:, :], preferred_element_type=F32))
    x1 = x_ref[0] + _rms(mix, gpm_ref[...])
    x1_ref[0] = x1
    hn_ref[0] = _rms(x1, gpf_ref[...]).astype(BF16)

    if tm >= halo:
        gbuf[0:halo, :] = gbuf[tm:tm + halo, :]


def _mix(att, glu, x, state, cw, cb, lg, lb, wo_bf, gpm, gpf, *, tm, stride):
    n_seq, rows, _ = x.shape
    halo = state.shape[1]
    assert rows == tm or tm >= halo
    offs = _conv_offsets(halo, stride)
    row_spec = lambda width: pl.BlockSpec((1, tm, width), lambda s, i: (s, i, 0))
    const = lambda shape: pl.BlockSpec(shape, lambda s, i: (0,) * len(shape))
    return pl.pallas_call(
        functools.partial(_mix_kernel, tm=tm, stride=stride, halo=halo, chunk=min(tm, 32)),
        grid=(n_seq, rows // tm),
        in_specs=[row_spec(ATT_WIDTH), row_spec(CONV_WIDTH), row_spec(D_MODEL),
                  pl.BlockSpec((1, halo, CONV_WIDTH), lambda s, i: (s, 0, 0)),
                  const((CONV_KERNEL, CONV_WIDTH)), const((1, CONV_WIDTH)), const((1, CONV_WIDTH)),
                  const((1, CONV_WIDTH)), const((D_MODEL, D_MODEL)), const((1, D_MODEL)), const((1, D_MODEL))],
        out_specs=[row_spec(D_MODEL), row_spec(D_MODEL)],
        out_shape=[jax.ShapeDtypeStruct((n_seq, rows, D_MODEL), F32),
                   jax.ShapeDtypeStruct((n_seq, rows, D_MODEL), BF16)],
        scratch_shapes=[pltpu.VMEM((halo + tm, CONV_WIDTH), F32), pltpu.VMEM((tm, CONV_WIDTH), BF16)] + [
            pltpu.VMEM((max(o - r for o in offs if o % SUBLANES == r) + tm, CONV_WIDTH), F32)
            for r in _shift_classes(offs)],
        compiler_params=pltpu.CompilerParams(dimension_semantics=("arbitrary", "arbitrary"),
                                             vmem_limit_bytes=VMEM_LIMIT),
        name="mix_s%d" % stride,
    )(att, glu, x, state, cw, cb, lg, lb, wo_bf, gpm, gpf)


def _ffn_kernel(hn_ref, x1_ref, wg_ref, wv_ref, wd_ref, dwg_ref, dwv_ref, dbg_ref, dbv_ref, stg_ref, stv_ref,
                gpo_ref, y_ref, tg_ref, tv_ref, ubuf_g, ubuf_v, car_g, car_v, acc_ref, *, tm, stride, halo, nc):
    i = pl.program_id(1)
    c = pl.program_id(2)
    hn = hn_ref[0]

    @pl.when(i == 0)
    def _():
        ubuf_g[0:halo, :] = stg_ref[0]
        ubuf_v[0:halo, :] = stv_ref[0]

    @pl.when(i > 0)
    def _():
        ubuf_g[0:halo, :] = car_g[c]
        ubuf_v[0:halo, :] = car_v[c]

    @pl.when(c == 0)
    def _():
        acc_ref[...] = jnp.zeros(acc_ref.shape, F32)

    def up(c0, c1):
        ubuf_g[halo:halo + tm, c0:c1] = jnp.dot(hn, wg_ref[:, c0:c1], preferred_element_type=F32)
        ubuf_v[halo:halo + tm, c0:c1] = jnp.dot(hn, wv_ref[:, c0:c1], preferred_element_type=F32)

    def conv(c0, c1, dw_ref, db_ref, ubuf):
        return (dw_ref[0:1, c0:c1] * ubuf[halo - 2 * stride:halo - 2 * stride + tm, c0:c1]
                + dw_ref[1:2, c0:c1] * ubuf[halo - stride:halo - stride + tm, c0:c1]
                + dw_ref[2:3, c0:c1] * ubuf[halo:halo + tm, c0:c1] + db_ref[:, c0:c1])

    ck = wg_ref.shape[1]
    bounds = [(c0, min(c0 + MXU_COLS, ck)) for c0 in range(0, ck, MXU_COLS)]
    def down(c0, c1, a):
        acc_ref[...] += jnp.dot(a, wd_ref[c0:c1, :], preferred_element_type=F32)

    up(*bounds[0])
    pending = None
    for n, (c0, c1) in enumerate(bounds):
        if n + 1 < len(bounds):
            up(*bounds[n + 1])
        if pending is not None:
            down(*pending)
        a = jax.nn.gelu(conv(c0, c1, dwg_ref, dbg_ref, ubuf_g), approximate=True) * conv(c0, c1, dwv_ref, dbv_ref,
                                                                                        ubuf_v)
        pending = (c0, c1, a.astype(BF16))
    down(*pending)

    for ubuf, car, tail_ref in ((ubuf_g, car_g, tg_ref), (ubuf_v, car_v, tv_ref)):
        last = ubuf[tm:tm + halo, :]
        car[c] = last
        tail_ref[0, 0] = last

    @pl.when(c == nc - 1)
    def _():
        y_ref[0] = x1_ref[0] + _rms(acc_ref[...], gpo_ref[...])


def _ffn(hn, x1, wup_bf, wdn_bf, dw, db, state, gpo, *, tm, stride, nc):
    n_seq, rows, _ = x1.shape
    halo = state.shape[1]
    ck = D_FF // nc
    row_spec = lambda width: pl.BlockSpec((1, tm, width), lambda s, i, c: (s, i, 0))
    gate_cols = lambda shape: pl.BlockSpec(shape, lambda s, i, c: (0, c))
    val_cols = lambda shape: pl.BlockSpec(shape, lambda s, i, c: (0, c + nc))
    return pl.pallas_call(
        functools.partial(_ffn_kernel, tm=tm, stride=stride, halo=halo, nc=nc),
        grid=(n_seq, rows // tm, nc),
        in_specs=[row_spec(D_MODEL), row_spec(D_MODEL),
                  gate_cols((D_MODEL, ck)), val_cols((D_MODEL, ck)),
                  pl.BlockSpec((ck, D_MODEL), lambda s, i, c: (c, 0)),
                  gate_cols((FFN_KERNEL, ck)), val_cols((FFN_KERNEL, ck)),
                  gate_cols((1, ck)), val_cols((1, ck)),
                  pl.BlockSpec((1, halo, ck), lambda s, i, c: (s, 0, c)),
                  pl.BlockSpec((1, halo, ck), lambda s, i, c: (s, 0, c + nc)),
                  pl.BlockSpec((1, D_MODEL), lambda s, i, c: (0, 0))],
        out_specs=[row_spec(D_MODEL),
                   pl.BlockSpec((1, 1, halo, ck), lambda s, i, c: (s, i, 0, c)),
                   pl.BlockSpec((1, 1, halo, ck), lambda s, i, c: (s, i, 0, c))],
        out_shape=[jax.ShapeDtypeStruct((n_seq, rows, D_MODEL), F32),
                   jax.ShapeDtypeStruct((n_seq, rows // tm, halo, D_FF), F32),
                   jax.ShapeDtypeStruct((n_seq, rows // tm, halo, D_FF), F32)],
        scratch_shapes=[pltpu.VMEM((halo + tm, ck), F32), pltpu.VMEM((halo + tm, ck), F32),
                        pltpu.VMEM((nc, halo, ck), F32), pltpu.VMEM((nc, halo, ck), F32),
                        pltpu.VMEM((tm, D_MODEL), F32)],
        compiler_params=pltpu.CompilerParams(dimension_semantics=("arbitrary",) * 3,
                                             vmem_limit_bytes=VMEM_LIMIT),
        name="ffn_s%d" % stride,
    )(hn, x1, wup_bf, wup_bf, wdn_bf, dw, dw, db, db, state, state, gpo)


def _tile(rows, want):
    return want if rows % want == 0 else rows


def kernel(x_prompt, x_sample, cache_k, cache_v, state_conv, state_ffn, page_table, rel_bias, g_pre_mix, w_in,
           conv_dw_w, conv_dw_b, conv_ln_g, conv_ln_b, w_out, g_post_mix, g_pre_ffn, w_ffn_up, ffn_dw_w,
           ffn_dw_b, w_ffn_down, g_post_ffn):
    depth = w_in.shape[0]
    assert depth == 1, "single-layer trunk"
    bp, seq, _ = x_prompt.shape
    bs, n_new, _ = x_sample.shape
    n_pool, page = cache_k.shape[1], cache_k.shape[2]
    n_pages = page_table.shape[1]
    assert seq % MOBA_BLOCK == 0 and page == LANES and (n_pages * page) % MOBA_BLOCK == 0
    assert n_new == SUBLANES and n_pages * page // MOBA_BLOCK <= LANES

    w_in_bf = w_in[0].astype(BF16)
    w_out_bf = w_out[0].astype(BF16)
    w_up_bf = w_ffn_up[0].astype(BF16)
    w_dn_bf = w_ffn_down[0].astype(BF16)
    cw, cb = conv_dw_w[0], conv_dw_b
    near_t, near_s = _bias_tiles(rel_bias)
    conv_halo_p = 32
    ffn_halo_p = SUBLANES

    tm_p = _tile(seq, 512)
    qt, kt, vt, kbf, vtb, kmean, glu_p = _proj(x_prompt, g_pre_mix, w_in_bf, prompt=True, tm=tm_p)
    att_p = _attn_prompt(qt, kbf, vtb, kmean.reshape(bp, seq // MOBA_BLOCK, ATT_WIDTH), near_t)
    x1_p, hn_p = _mix(att_p, glu_p, x_prompt, jnp.zeros((bp, conv_halo_p, CONV_WIDTH), F32), cw, cb,
                      conv_ln_g, conv_ln_b, w_out_bf, g_post_mix, g_pre_ffn, tm=tm_p, stride=1)
    y_prompt, tail_g, tail_v = _ffn(hn_p, x1_p, w_up_bf, w_dn_bf, ffn_dw_w[0], ffn_dw_b,
                                    jnp.zeros((bp, ffn_halo_p, 2 * D_FF), F32), g_post_ffn,
                                    tm=tm_p, stride=1, nc=2)
    k_prompt = kt.reshape(1, bp, N_HEADS, HEAD_DIM, seq).transpose(0, 1, 4, 2, 3)
    v_prompt = vt.reshape(1, bp, N_HEADS, HEAD_DIM, seq).transpose(0, 1, 4, 2, 3)
    conv_prompt = glu_p[:, seq - (CONV_KERNEL - 1):, :][None]
    ffn_prompt = jnp.concatenate([tail_g[:, -1], tail_v[:, -1]],
                                 axis=-1)[:, ffn_halo_p - (FFN_KERNEL - 1):, :][None]

    rows_s = n_new * bs
    to_tb = lambda a: a.transpose(1, 0, 2).reshape(1, rows_s, a.shape[-1])
    to_bt = lambda a: a.reshape(n_new, bs, a.shape[-1]).transpose(1, 0, 2)
    xs = to_tb(x_sample)
    q_s, k_s, v_s, glu_s = _proj(xs, g_pre_mix, w_in_bf, prompt=False, tm=rows_s)
    q_b, k_b, v_b = to_bt(q_s), to_bt(k_s), to_bt(v_s)
    cache_kt = cache_k[0].transpose(0, 2, 3, 1).reshape(n_pool, ATT_WIDTH, page)
    cache_vt = cache_v[0].transpose(0, 2, 3, 1).reshape(n_pool, ATT_WIDTH, page)
    att_b = _attn_sample(page_table, rel_bias, q_b, k_b, v_b, near_s, cache_kt, cache_vt,
                         nbuf=min(32, n_pages))
    conv_state = state_conv[0].transpose(1, 0, 2).reshape(1, (CONV_KERNEL - 1) * bs, CONV_WIDTH)
    x1_s, hn_s = _mix(to_tb(att_b), glu_s, xs, conv_state, cw, cb, conv_ln_g, conv_ln_b, w_out_bf,
                      g_post_mix, g_pre_ffn, tm=rows_s, stride=bs)
    ffn_state = state_ffn[0].transpose(1, 0, 2).reshape(1, (FFN_KERNEL - 1) * bs, 2 * D_FF)
    y_s, tail_gs, tail_vs = _ffn(hn_s, x1_s, w_up_bf, w_dn_bf, ffn_dw_w[0], ffn_dw_b, ffn_state, g_post_ffn,
                                 tm=rows_s, stride=bs, nc=2)
    y_sample = to_bt(y_s)
    k_sample = k_b.reshape(1, bs, n_new, N_HEADS, HEAD_DIM)
    v_sample = v_b.reshape(1, bs, n_new, N_HEADS, HEAD_DIM)
    conv_all = jnp.concatenate([conv_state[0], glu_s[0]], axis=0)[n_new * bs:]
    conv_sample = conv_all.reshape(CONV_KERNEL - 1, bs, CONV_WIDTH).transpose(1, 0, 2)[None]
    ffn_sample = jnp.concatenate([tail_gs[:, -1], tail_vs[:, -1]], axis=-1).reshape(
        FFN_KERNEL - 1, bs, 2 * D_FF).transpose(1, 0, 2)[None]

    return (y_prompt, y_sample, k_prompt, v_prompt, conv_prompt, ffn_prompt,
            k_sample, v_sample, conv_sample, ffn_sample)
```

```python
import functools
import math

import numpy as np
import jax
import jax.numpy as jnp
from jax import lax
from jax.experimental import pallas as pl
from jax.experimental.pallas import tpu as pltpu

F32 = jnp.float32
BF16 = jnp.bfloat16

D_MODEL = 1024
HEAD_DIM = 64
ATT_WIDTH = 512
N_HEADS = 8
N_PAIRS = N_HEADS // 2
CONV_WIDTH = 512
CONV_KERNEL = 31
MOBA_BLOCK = 256
MOBA_TOPK = 3
N_BUCKETS = 32
MAX_DISTANCE = 128
D_FF = 2816
FFN_KERNEL = 3
EPS = 1e-6
SCALE = HEAD_DIM ** -0.5
LOG2E = math.log2(math.e)
NEG = -1e30
LANES = 128
SUBLANES = 8
VMEM_LIMIT = 56 * 1024 * 1024
MXU_COLS = 256
ONES_ROWS = 16
V_ROWS = HEAD_DIM + ONES_ROWS


def _bucket_thresholds():
    n = np.arange(0, 4 * MAX_DISTANCE)
    max_exact = N_BUCKETS // 2
    nf = np.maximum(n, 1).astype(np.float64)
    large = max_exact + (np.log(nf / max_exact) / math.log(MAX_DISTANCE / max_exact)
                         * (N_BUCKETS - max_exact)).astype(np.int64)
    bucket = np.where(n < max_exact, n, np.minimum(large, N_BUCKETS - 1))
    return tuple(int(np.argmax(bucket >= k)) for k in range(1, N_BUCKETS))


_BUCKET_THR = _bucket_thresholds()


def _rms(x, g):
    return x * lax.rsqrt(jnp.mean(x * x, axis=-1, keepdims=True) + EPS) * g


def _bias_kernel(tbl_ref, near_t_ref, near_s_ref):
    p = pl.program_id(0)

    def bias_of(dist, h):
        b = jnp.full(dist.shape, tbl_ref[0, h], F32)
        for k in range(1, N_BUCKETS):
            b = jnp.where(dist >= _BUCKET_THR[k - 1], tbl_ref[k, h], b)
        return jnp.where(dist >= 0, b, NEG)

    jj = lax.broadcasted_iota(jnp.int32, (2 * MOBA_BLOCK, MOBA_BLOCK), 0)
    ii = lax.broadcasted_iota(jnp.int32, (2 * MOBA_BLOCK, MOBA_BLOCK), 1)
    dist = MOBA_BLOCK + ii - jj
    for half in range(2):
        h = 2 * p + half
        near_t_ref[0, :, half * MOBA_BLOCK:(half + 1) * MOBA_BLOCK] = (
            bias_of(dist, h) - tbl_ref[N_BUCKETS - 1, h]) * LOG2E

    @pl.when(p == 0)
    def _():
        tt = lax.broadcasted_iota(jnp.int32, (SUBLANES, 2 * MOBA_BLOCK), 0)
        j2 = lax.broadcasted_iota(jnp.int32, (SUBLANES, 2 * MOBA_BLOCK), 1)
        d2 = MOBA_BLOCK + tt - j2
        for h in range(N_HEADS):
            near_s_ref[h * SUBLANES:(h + 1) * SUBLANES, :] = bias_of(d2, h)


def _bias_tiles(rel_bias):
    return pl.pallas_call(
        _bias_kernel,
        grid=(N_PAIRS,),
        in_specs=[pl.BlockSpec(memory_space=pltpu.SMEM)],
        out_specs=[pl.BlockSpec((1, 2 * MOBA_BLOCK, 2 * MOBA_BLOCK), lambda p: (p, 0, 0)),
                   pl.BlockSpec((N_HEADS * SUBLANES, 2 * MOBA_BLOCK), lambda p: (0, 0))],
        out_shape=[jax.ShapeDtypeStruct((N_PAIRS, 2 * MOBA_BLOCK, 2 * MOBA_BLOCK), F32),
                   jax.ShapeDtypeStruct((N_HEADS * SUBLANES, 2 * MOBA_BLOCK), F32)],
        compiler_params=pltpu.CompilerParams(dimension_semantics=("arbitrary",)),
        name="bias_tiles",
    )(rel_bias)


def _proj_kernel(x_ref, g_ref, w_ref, *out_refs, prompt, tm):
    h = _rms(x_ref[0], g_ref[...])
    p = jnp.dot(h.astype(BF16), w_ref[...], preferred_element_type=F32)
    q = p[:, 0:ATT_WIDTH]
    k = p[:, ATT_WIDTH:2 * ATT_WIDTH]
    v = p[:, 2 * ATT_WIDTH:3 * ATT_WIDTH]
    ga = p[:, 3 * ATT_WIDTH:3 * ATT_WIDTH + CONV_WIDTH]
    gb = p[:, 3 * ATT_WIDTH + CONV_WIDTH:]
    glu = ga * jax.nn.sigmoid(gb)
    if prompt:
        qt_ref, kt_ref, vt_ref, kbf_ref, vtb_ref, km_ref, glu_ref = out_refs
        qt_ref[0] = q.T
        kt_ref[0] = k.T
        vt = v.T
        vt_ref[0] = vt
        row_blk = (pl.program_id(1) * (tm // MOBA_BLOCK)
                   + lax.broadcasted_iota(jnp.int32, (tm, LANES), 0) // MOBA_BLOCK)
        onehot = jnp.where(lax.broadcasted_iota(jnp.int32, (tm, LANES), 1) == row_blk, 1.0, 0.0).astype(BF16)
        kb = k.astype(BF16)
        for pr in range(N_PAIRS):
            kbf_ref[0, :, 2 * pr * LANES:(2 * pr + 1) * LANES] = kb[:, pr * LANES:(pr + 1) * LANES]
            kbf_ref[0, :, (2 * pr + 1) * LANES:(2 * pr + 2) * LANES] = onehot
        ones = jnp.ones((ONES_ROWS, MOBA_BLOCK), F32)
        for c in range(tm // MOBA_BLOCK):
            vc = vt[:, c * MOBA_BLOCK:(c + 1) * MOBA_BLOCK]
            parts = []
            for hd in range(N_HEADS):
                parts += [vc[hd * HEAD_DIM:(hd + 1) * HEAD_DIM], ones]
            vtb_ref[0, c] = jnp.concatenate(parts, axis=0).astype(BF16)
        km_ref[0] = jnp.mean(k.reshape(tm // MOBA_BLOCK, MOBA_BLOCK, ATT_WIDTH), axis=1)[:, None, :]
        glu_ref[0] = glu
    else:
        q_ref, k_ref, v_ref, glu_ref = out_refs
        q_ref[0] = q
        k_ref[0] = k
        v_ref[0] = v
        glu_ref[0] = glu


def _proj(x, g, w_bf, *, prompt, tm):
    n_seq, rows, _ = x.shape
    n_out = w_bf.shape[1]
    grid = (n_seq, rows // tm)
    row_spec = lambda width: pl.BlockSpec((1, tm, width), lambda s, i: (s, i, 0))
    col_spec = pl.BlockSpec((1, ATT_WIDTH, tm), lambda s, i: (s, 0, i))
    if prompt:
        nb = tm // MOBA_BLOCK
        out_specs = [col_spec, col_spec, col_spec, row_spec(2 * ATT_WIDTH),
                     pl.BlockSpec((1, nb, N_HEADS * V_ROWS, MOBA_BLOCK), lambda s, i: (s, i, 0, 0)),
                     pl.BlockSpec((1, nb, 1, ATT_WIDTH), lambda s, i: (s, i, 0, 0)),
                     row_spec(CONV_WIDTH)]
        t_shape = jax.ShapeDtypeStruct((n_seq, ATT_WIDTH, rows), F32)
        out_shape = [t_shape, t_shape, t_shape,
                     jax.ShapeDtypeStruct((n_seq, rows, 2 * ATT_WIDTH), BF16),
                     jax.ShapeDtypeStruct((n_seq, rows // MOBA_BLOCK, N_HEADS * V_ROWS, MOBA_BLOCK), BF16),
                     jax.ShapeDtypeStruct((n_seq, rows // MOBA_BLOCK, 1, ATT_WIDTH), F32),
                     jax.ShapeDtypeStruct((n_seq, rows, CONV_WIDTH), F32)]
    else:
        out_specs = [row_spec(ATT_WIDTH)] * 3 + [row_spec(CONV_WIDTH)]
        out_shape = [jax.ShapeDtypeStruct((n_seq, rows, ATT_WIDTH), F32)] * 3 + [
            jax.ShapeDtypeStruct((n_seq, rows, CONV_WIDTH), F32)]
    return pl.pallas_call(
        functools.partial(_proj_kernel, prompt=prompt, tm=tm),
        grid=grid,
        in_specs=[row_spec(D_MODEL),
                  pl.BlockSpec((1, D_MODEL), lambda s, i: (0, 0)),
                  pl.BlockSpec((D_MODEL, n_out), lambda s, i: (0, 0))],
        out_specs=out_specs,
        out_shape=out_shape,
        compiler_params=pltpu.CompilerParams(dimension_semantics=("arbitrary", "arbitrary"),
                                             vmem_limit_bytes=VMEM_LIMIT),
        name="proj_prompt" if prompt else "proj_sample",
    )(x, g, w_bf)


def _select_topk(gate, idx, axis, n_valid):
    big = jnp.int32(2 ** 30)
    g = jnp.where(idx < n_valid, gate, -jnp.inf)
    sel = jnp.zeros(gate.shape, jnp.bool_)
    for _ in range(MOBA_TOPK):
        mx = jnp.max(g, axis=axis, keepdims=True)
        first = jnp.min(jnp.where(g == mx, idx, big), axis=axis, keepdims=True)
        hit = idx == first
        sel = sel | (hit & (mx > -jnp.inf))
        g = jnp.where(hit, -jnp.inf, g)
    return sel


def _attn_prompt_kernel(qt_ref, qtn_ref, k_ref, vt_ref, km_ref, near_ref, o_ref, qaug_ref, mnext_ref,
                        sa_ref, sb_ref, pa_ref, pb_ref, *, nblk):
    i = pl.program_id(2)
    blk = MOBA_BLOCK

    def block_diag(qt):
        z = jnp.zeros((HEAD_DIM, blk), F32)
        return jnp.concatenate([jnp.concatenate([qt[:HEAD_DIM], z], axis=1),
                                jnp.concatenate([z, qt[HEAD_DIM:]], axis=1)], axis=0)

    def mask_rows(q2t, tile):
        gate = jnp.dot(km_ref[0], q2t, precision=lax.Precision.HIGHEST, preferred_element_type=F32)
        n_idx = lax.broadcasted_iota(jnp.int32, (nblk, 2 * blk), 0)
        sel = _select_topk(gate, n_idx, 0, tile)
        return jnp.concatenate([jnp.where(sel | (n_idx == tile), 0.0, NEG),
                                jnp.full((LANES - nblk, 2 * blk), NEG, F32)], axis=0).astype(BF16)

    qaug_ref[0:LANES, :] = (block_diag(qt_ref[0]) * (SCALE * LOG2E)).astype(BF16)

    @pl.when(i == 0)
    def _():
        row = lax.broadcasted_iota(jnp.int32, (LANES, 2 * blk), 0)
        qaug_ref[LANES:, :] = jnp.where(row == 0, 0.0, NEG).astype(BF16)

    @pl.when(i > 0)
    def _():
        qaug_ref[LANES:, :] = mnext_ref[...]

    def scores_into(dst_ref, j0, j1, bias):
        mx = None
        for half, j in enumerate((j0, j1)):
            kb = k_ref[0, pl.ds(pl.multiple_of(j * blk, blk), blk), :]
            s = jnp.dot(kb, qaug_ref[...], preferred_element_type=F32)
            if bias is not None:
                s = s + bias[half]
            dst_ref[half] = s
            smax = jnp.max(s, axis=0, keepdims=True)
            mx = smax if mx is None else jnp.maximum(mx, smax)
        return mx

    def probs(src_ref, dst_ref, m, mx):
        m_new = jnp.maximum(m, mx)
        dst_ref[0] = jnp.exp2(src_ref[0] - m_new).astype(BF16)
        dst_ref[1] = jnp.exp2(src_ref[1] - m_new).astype(BF16)
        return m_new, jnp.exp2(m - m_new)

    def accumulate(p_ref, j0, j1, alpha, a_a, a_b):
        p0, p1 = p_ref[0], p_ref[1]
        v0, v1 = vt_ref[0, j0], vt_ref[0, j1]
        d_a = (jnp.dot(v0[:V_ROWS], p0[:, :blk], preferred_element_type=F32)
               + jnp.dot(v1[:V_ROWS], p1[:, :blk], preferred_element_type=F32))
        d_b = (jnp.dot(v0[V_ROWS:], p0[:, blk:], preferred_element_type=F32)
               + jnp.dot(v1[V_ROWS:], p1[:, blk:], preferred_element_type=F32))
        return a_a * alpha[:, :blk] + d_a, a_b * alpha[:, blk:] + d_b

    none = nblk - 1
    prev = jnp.where(i > 0, i - 1, none)
    n_far = jnp.maximum(i - 1, 0)
    n_pairs = (n_far + 1) // 2

    mx = scores_into(sa_ref, i, prev, (near_ref[0, blk:, :], near_ref[0, :blk, :]))
    mnext_ref[...] = mask_rows(block_diag(qtn_ref[0]), i + 1)
    pb_ref[...] = jnp.zeros(pb_ref.shape, BF16)
    zero = jnp.zeros((V_ROWS, blk), F32)
    state = (jnp.full((1, 2 * blk), NEG, F32), mx, jnp.ones((1, 2 * blk), F32), zero, zero, i, prev, i, i)

    def stage(s_src, s_dst, p_dst, p_src, t, state):
        m, mx, alpha_p, a_a, a_b, js0, js1, jp0, jp1 = state
        j0 = 2 * t
        j1 = jnp.where(j0 + 1 < n_far, j0 + 1, none)
        mx_next = scores_into(s_dst, j0, j1, None)
        m_new, alpha = probs(s_src, p_dst, m, mx)
        a_a, a_b = accumulate(p_src, jp0, jp1, alpha_p, a_a, a_b)
        return m_new, mx_next, alpha, a_a, a_b, j0, j1, js0, js1

    def step(t, state):
        return lax.cond(t % 2 == 0,
                        lambda st: stage(sa_ref, sb_ref, pa_ref, pb_ref, t, st),
                        lambda st: stage(sb_ref, sa_ref, pb_ref, pa_ref, t, st), state)

    def drain(s_src, p_dst, p_src, state):
        m, mx, alpha_p, a_a, a_b, js0, js1, jp0, jp1 = state
        a_a, a_b = accumulate(p_src, jp0, jp1, alpha_p, a_a, a_b)
        _, alpha = probs(s_src, p_dst, m, mx)
        return accumulate(p_dst, js0, js1, alpha, a_a, a_b)

    state = lax.fori_loop(0, n_pairs, step, state)
    a_a, a_b = lax.cond(n_pairs % 2 == 0,
                        lambda st: drain(sa_ref, pa_ref, pb_ref, st),
                        lambda st: drain(sb_ref, pb_ref, pa_ref, st), state)
    out_t = jnp.concatenate([a_a[:HEAD_DIM] / a_a[HEAD_DIM:HEAD_DIM + 1],
                             a_b[:HEAD_DIM] / a_b[HEAD_DIM:HEAD_DIM + 1]], axis=0)
    o_ref[0] = out_t.T


def _attn_prompt(qt, kaug, vtb, kmean, near_t):
    n_seq, _, t = qt.shape
    nblk = t // MOBA_BLOCK
    assert nblk % 2 == 0 and nblk < LANES
    return pl.pallas_call(
        functools.partial(_attn_prompt_kernel, nblk=nblk),
        grid=(n_seq, N_PAIRS, nblk),
        in_specs=[pl.BlockSpec((1, LANES, MOBA_BLOCK), lambda b, p, i: (b, p, i)),
                  pl.BlockSpec((1, LANES, MOBA_BLOCK), lambda b, p, i: (b, p, jnp.minimum(i + 1, nblk - 1))),
                  pl.BlockSpec((1, t, 2 * LANES), lambda b, p, i: (b, 0, p)),
                  pl.BlockSpec((1, nblk, 2 * V_ROWS, MOBA_BLOCK), lambda b, p, i: (b, 0, p, 0)),
                  pl.BlockSpec((1, nblk, LANES), lambda b, p, i: (b, 0, p)),
                  pl.BlockSpec((1, 2 * MOBA_BLOCK, 2 * MOBA_BLOCK), lambda b, p, i: (p, 0, 0))],
        out_specs=pl.BlockSpec((1, MOBA_BLOCK, LANES), lambda b, p, i: (b, i, p)),
        out_shape=jax.ShapeDtypeStruct((n_seq, t, ATT_WIDTH), F32),
        scratch_shapes=[pltpu.VMEM((2 * LANES, 2 * MOBA_BLOCK), BF16),
                        pltpu.VMEM((LANES, 2 * MOBA_BLOCK), BF16),
                        pltpu.VMEM((2, MOBA_BLOCK, 2 * MOBA_BLOCK), F32),
                        pltpu.VMEM((2, MOBA_BLOCK, 2 * MOBA_BLOCK), F32),
                        pltpu.VMEM((2, MOBA_BLOCK, 2 * MOBA_BLOCK), BF16),
                        pltpu.VMEM((2, MOBA_BLOCK, 2 * MOBA_BLOCK), BF16)],
        compiler_params=pltpu.CompilerParams(dimension_semantics=("arbitrary",) * 3,
                                             vmem_limit_bytes=VMEM_LIMIT),
        name="attn_prompt",
    )(qt, qt, kaug, vtb, kmean, near_t)


def _attn_sample_kernel(pt_ref, tbl_ref, q_ref, kn_ref, vn_ref, near_ref, ck_hbm, cv_hbm, o_ref,
                        s_all, km_t, kbuf, vbuf, ksem, vsem, *, n_pages, nbuf, step, n_new):
    b = pl.program_id(0)
    n_seq = pl.num_programs(0)
    n_rows = N_HEADS * n_new
    page = LANES
    nblk = n_pages * page // MOBA_BLOCK

    def k_copy(seq, p, slot):
        return pltpu.make_async_copy(ck_hbm.at[pt_ref[seq, p]], kbuf.at[slot], ksem.at[slot])

    def v_copy(p, slot):
        return pltpu.make_async_copy(cv_hbm.at[pt_ref[b, p]], vbuf.at[slot], vsem.at[slot])

    @pl.when(b == 0)
    def _():
        for p in range(nbuf):
            k_copy(0, p, p).start()

    row_head = lax.broadcasted_iota(jnp.int32, (n_rows, ATT_WIDTH), 0) // n_new
    col_head = lax.broadcasted_iota(jnp.int32, (n_rows, ATT_WIDTH), 1) // HEAD_DIM
    diag = row_head == col_head
    q = q_ref[0]
    qbd_f = jnp.where(diag, jnp.concatenate([q] * N_HEADS, axis=0), 0.0)
    qbd_b = (qbd_f * SCALE).astype(BF16)
    km_t[...] = jnp.zeros(km_t.shape, F32)
    lane_k = lax.broadcasted_iota(jnp.int32, (ATT_WIDTH, LANES), 1)

    def k_step(it, _):
        p0 = it * step
        slot0 = p0 % nbuf
        for r in range(step):
            k_copy(b, p0 + r, slot0 + r).wait()
        km = km_t[...]
        for r in range(0, step, 2):
            kt0 = kbuf[slot0 + r]
            kt1 = kbuf[slot0 + r + 1]
            s_all[p0 + r] = jnp.dot(qbd_b, kt0.astype(BF16), preferred_element_type=F32)
            s_all[p0 + r + 1] = jnp.dot(qbd_b, kt1.astype(BF16), preferred_element_type=F32)
            mean = jnp.sum(kt0 + kt1, axis=1, keepdims=True) * (1.0 / MOBA_BLOCK)
            km = jnp.where(lane_k == (p0 + r) // 2, mean, km)
        km_t[...] = km

        @pl.when(p0 + nbuf < n_pages)
        def _():
            for r in range(step):
                k_copy(b, p0 + nbuf + r, slot0 + r).start()

        return 0

    lax.fori_loop(0, n_pages // step, k_step, 0)

    @pl.when(b + 1 < n_seq)
    def _():
        for p in range(nbuf):
            k_copy(b + 1, p, p).start()

    for p in range(nbuf):
        v_copy(p, p).start()

    gate = jnp.dot(qbd_f, km_t[...], precision=lax.Precision.HIGHEST,
                   preferred_element_type=F32)
    lane = lax.broadcasted_iota(jnp.int32, (n_rows, LANES), 1)
    sel = _select_topk(gate, lane, 1, nblk)
    rh = lax.broadcasted_iota(jnp.int32, (n_rows, 1), 0) // n_new
    far = jnp.zeros((n_rows, 1), F32)
    for h in range(N_HEADS):
        far = jnp.where(rh == h, tbl_ref[N_BUCKETS - 1, h], far)
    mask = jnp.where(sel, far, NEG)
    near = near_ref[...]
    for n in range(nblk):
        if n == nblk - 1:
            seln = jnp.broadcast_to(mask[:, n:n + 1], (n_rows, page)) > 0.5 * NEG
            s_all[2 * n] = s_all[2 * n] + jnp.where(seln, near[:, 0:page], NEG)
            s_all[2 * n + 1] = s_all[2 * n + 1] + jnp.where(seln, near[:, page:2 * page], NEG)
        else:
            add = jnp.broadcast_to(mask[:, n:n + 1], (n_rows, page))
            s_all[2 * n] = s_all[2 * n] + add
            s_all[2 * n + 1] = s_all[2 * n + 1] + add
    kn = jnp.concatenate([kn_ref[0], jnp.zeros((page - n_new, ATT_WIDTH), F32)], axis=0).astype(BF16)
    s_own = lax.dot_general(qbd_b, kn, (((1,), (1,)), ((), ())),
                            preferred_element_type=F32) + near[:, 2 * page:3 * page]
    mrun = lax.fori_loop(0, n_pages, lambda j, mm: jnp.maximum(mm, s_all[j]), s_own)
    m = jnp.max(mrun, axis=1, keepdims=True)
    p_own = jnp.exp(s_own - m)

    def exp_step(j, lsum):
        pj = jnp.exp(s_all[j] - m)
        s_all[j] = pj
        return lsum + pj

    lsum = lax.fori_loop(0, n_pages, exp_step, p_own)
    l = jnp.sum(lsum, axis=1, keepdims=True)
    vn = jnp.concatenate([vn_ref[0], jnp.zeros((page - n_new, ATT_WIDTH), F32)], axis=0).astype(BF16)
    acc = jnp.dot(p_own.astype(BF16), vn, preferred_element_type=F32)

    def v_step(it, acc):
        p0 = it * step
        slot0 = p0 % nbuf
        for r in range(step):
            v_copy(p0 + r, slot0 + r).wait()
        parts = []
        for r in range(step):
            pb = s_all[p0 + r].astype(BF16)
            vt = vbuf[slot0 + r].astype(BF16)
            parts.append(lax.dot_general(pb, vt, (((1,), (1,)), ((), ())), preferred_element_type=F32))
        while len(parts) > 1:
            parts = [x + y for x, y in zip(parts[0::2], parts[1::2])]

        @pl.when(p0 + nbuf < n_pages)
        def _():
            for r in range(step):
                v_copy(p0 + nbuf + r, slot0 + r).start()

        return acc + parts[0]

    acc = lax.fori_loop(0, n_pages // step, v_step, acc)
    o = jnp.where(diag, acc / l, 0.0)
    o_ref[0] = jnp.sum(o.reshape(N_HEADS, n_new, ATT_WIDTH), axis=0)


def _attn_sample(page_table, rel_bias, q, k_new, v_new, near_s, cache_kt, cache_vt, *, nbuf):
    n_seq, n_new, _ = q.shape
    n_pages = page_table.shape[1]
    n_rows = N_HEADS * n_new
    page = cache_kt.shape[2]
    step = min(8, nbuf)
    assert step & (step - 1) == 0 and step >= 2 and nbuf % step == 0 and n_pages % nbuf == 0
    new_spec = pl.BlockSpec((1, n_new, ATT_WIDTH), lambda b, pt: (b, 0, 0))
    in_specs = [pl.BlockSpec(memory_space=pltpu.SMEM), new_spec, new_spec, new_spec,
                pl.BlockSpec((n_rows, 2 * MOBA_BLOCK), lambda b, pt: (0, 0)),
                pl.BlockSpec(memory_space=pl.ANY), pl.BlockSpec(memory_space=pl.ANY)]
    return pl.pallas_call(
        functools.partial(_attn_sample_kernel, n_pages=n_pages, nbuf=nbuf, step=step, n_new=n_new),
        grid_spec=pltpu.PrefetchScalarGridSpec(
            num_scalar_prefetch=1,
            grid=(n_seq,),
            in_specs=in_specs,
            out_specs=new_spec,
            scratch_shapes=[pltpu.VMEM((n_pages, n_rows, page), F32),
                            pltpu.VMEM((ATT_WIDTH, LANES), F32),
                            pltpu.VMEM((nbuf, ATT_WIDTH, page), F32),
                            pltpu.VMEM((nbuf, ATT_WIDTH, page), F32),
                            pltpu.SemaphoreType.DMA((nbuf,)),
                            pltpu.SemaphoreType.DMA((nbuf,))]),
        out_shape=jax.ShapeDtypeStruct((n_seq, n_new, ATT_WIDTH), F32),
        compiler_params=pltpu.CompilerParams(dimension_semantics=("arbitrary",),
                                             vmem_limit_bytes=VMEM_LIMIT),
        name="attn_sample",
    )(page_table, rel_bias, q, k_new, v_new, near_s, cache_kt, cache_vt)


def _conv_offsets(halo, stride):
    off0 = halo - (CONV_KERNEL - 1) * stride
    return [off0 + j * stride for j in range(CONV_KERNEL)]


def _shift_classes(offs):
    return sorted({o % SUBLANES for o in offs} - {0})


def _mix_kernel(att_ref, glu_ref, x_ref, st_ref, cw_ref, cb_ref, lg_ref, lb_ref, wo_ref, gpm_ref, gpf_ref,
                x1_ref, hn_ref, gbuf, cvbuf, *shifted, tm, stride, halo, chunk):
    i = pl.program_id(1)

    @pl.when(i == 0)
    def _():
        gbuf[0:halo, :] = st_ref[0]

    gbuf[halo:halo + tm, :] = glu_ref[0]
    offs = _conv_offsets(halo, stride)
    for r, ref in zip(_shift_classes(offs), shifted):
        ref[...] = gbuf[r:r + ref.shape[0], :]
    src = dict(zip(_shift_classes(offs), shifted))
    src[0] = gbuf
    for c in range(tm // chunk):
        r0 = c * chunk
        acc = jnp.broadcast_to(cb_ref[...], (chunk, CONV_WIDTH))
        for j, o in enumerate(offs):
            a = o - o % SUBLANES + r0
            acc = acc + cw_ref[j:j + 1, :] * src[o % SUBLANES][a:a + chunk, :]
        mu = jnp.mean(acc, axis=-1, keepdims=True)
        d = acc - mu
        var = jnp.mean(d * d, axis=-1, keepdims=True)
        y = d * lax.rsqrt(var + EPS) * lg_ref[...] + lb_ref[...]
        cvbuf[r0:r0 + chunk, :] = (y * jax.nn.sigmoid(y)).astype(BF16)

    mix = (jnp.dot(att_ref[0].astype(BF16), wo_ref[0:ATT_WIDTH, :], preferred_element_type=F32)
           + jnp.dot(cvbuf[...], wo_ref[ATT_WIDTH:, :], preferred_element_type=F32))
    x1 = x_ref[0] + _rms(mix, gpm_ref[...])
    x1_ref[0] = x1
    hn_ref[0] = _rms(x1, gpf_ref[...]).astype(BF16)

    if tm >= halo:
        gbuf[0:halo, :] = gbuf[tm:tm + halo, :]


def _mix(att, glu, x, state, cw, cb, lg, lb, wo_bf, gpm, gpf, *, tm, stride):
    n_seq, rows, _ = x.shape
    halo = state.shape[1]
    assert rows == tm or tm >= halo
    offs = _conv_offsets(halo, stride)
    row_spec = lambda width: pl.BlockSpec((1, tm, width), lambda s, i: (s, i, 0))
    const = lambda shape: pl.BlockSpec(shape, lambda s, i: (0,) * len(shape))
    return pl.pallas_call(
        functools.partial(_mix_kernel, tm=tm, stride=stride, halo=halo, chunk=min(tm, 32)),
        grid=(n_seq, rows // tm),
        in_specs=[row_spec(ATT_WIDTH), row_spec(CONV_WIDTH), row_spec(D_MODEL),
                  pl.BlockSpec((1, halo, CONV_WIDTH), lambda s, i: (s, 0, 0)),
                  const((CONV_KERNEL, CONV_WIDTH)), const((1, CONV_WIDTH)), const((1, CONV_WIDTH)),
                  const((1, CONV_WIDTH)), const((D_MODEL, D_MODEL)), const((1, D_MODEL)), const((1, D_MODEL))],
        out_specs=[row_spec(D_MODEL), row_spec(D_MODEL)],
        out_shape=[jax.ShapeDtypeStruct((n_seq, rows, D_MODEL), F32),
                   jax.ShapeDtypeStruct((n_seq, rows, D_MODEL), BF16)],
        scratch_shapes=[pltpu.VMEM((halo + tm, CONV_WIDTH), F32), pltpu.VMEM((tm, CONV_WIDTH), BF16)] + [
            pltpu.VMEM((max(o - r for o in offs if o % SUBLANES == r) + tm, CONV_WIDTH), F32)
            for r in _shift_classes(offs)],
        compiler_params=pltpu.CompilerParams(dimension_semantics=("arbitrary", "arbitrary"),
                                             vmem_limit_bytes=VMEM_LIMIT),
        name="mix_s%d" % stride,
    )(att, glu, x, state, cw, cb, lg, lb, wo_bf, gpm, gpf)


def _ffn_kernel(hn_ref, x1_ref, wg_ref, wv_ref, wd_ref, dwg_ref, dwv_ref, dbg_ref, dbv_ref, stg_ref, stv_ref,
                gpo_ref, y_ref, tg_ref, tv_ref, ubuf_g, ubuf_v, car_g, car_v, acc_ref, *, tm, stride, halo, nc):
    i = pl.program_id(1)
    c = pl.program_id(2)
    hn = hn_ref[0]

    @pl.when(i == 0)
    def _():
        ubuf_g[0:halo, :] = stg_ref[0]
        ubuf_v[0:halo, :] = stv_ref[0]

    @pl.when(i > 0)
    def _():
        ubuf_g[0:halo, :] = car_g[c]
        ubuf_v[0:halo, :] = car_v[c]

    @pl.when(c == 0)
    def _():
        acc_ref[...] = jnp.zeros(acc_ref.shape, F32)

    def up(c0, c1):
        ubuf_g[halo:halo + tm, c0:c1] = jnp.dot(hn, wg_ref[:, c0:c1], preferred_element_type=F32)
        ubuf_v[halo:halo + tm, c0:c1] = jnp.dot(hn, wv_ref[:, c0:c1], preferred_element_type=F32)

    def conv(c0, c1, dw_ref, db_ref, ubuf):
        return (dw_ref[0:1, c0:c1] * ubuf[halo - 2 * stride:halo - 2 * stride + tm, c0:c1]
                + dw_ref[1:2, c0:c1] * ubuf[halo - stride:halo - stride + tm, c0:c1]
                + dw_ref[2:3, c0:c1] * ubuf[halo:halo + tm, c0:c1] + db_ref[:, c0:c1])

    ck = wg_ref.shape[1]
    bounds = [(c0, min(c0 + MXU_COLS, ck)) for c0 in range(0, ck, MXU_COLS)]
    def down(c0, c1, a):
        acc_ref[...] += jnp.dot(a, wd_ref[c0:c1, :], preferred_element_type=F32)

    up(*bounds[0])
    pending = None
    for n, (c0, c1) in enumerate(bounds):
        if n + 1 < len(bounds):
            up(*bounds[n + 1])
        if pending is not None:
            down(*pending)
        a = jax.nn.gelu(conv(c0, c1, dwg_ref, dbg_ref, ubuf_g), approximate=True) * conv(c0, c1, dwv_ref, dbv_ref,
                                                                                        ubuf_v)
        pending = (c0, c1, a.astype(BF16))
    down(*pending)

    for ubuf, car, tail_ref in ((ubuf_g, car_g, tg_ref), (ubuf_v, car_v, tv_ref)):
        last = ubuf[tm:tm + halo, :]
        car[c] = last
        tail_ref[0, 0] = last

    @pl.when(c == nc - 1)
    def _():
        y_ref[0] = x1_ref[0] + _rms(acc_ref[...], gpo_ref[...])


def _ffn(hn, x1, wup_bf, wdn_bf, dw, db, state, gpo, *, tm, stride, nc):
    n_seq, rows, _ = x1.shape
    halo = state.shape[1]
    ck = D_FF // nc
    row_spec = lambda width: pl.BlockSpec((1, tm, width), lambda s, i, c: (s, i, 0))
    gate_cols = lambda shape: pl.BlockSpec(shape, lambda s, i, c: (0, c))
    val_cols = lambda shape: pl.BlockSpec(shape, lambda s, i, c: (0, c + nc))
    return pl.pallas_call(
        functools.partial(_ffn_kernel, tm=tm, stride=stride, halo=halo, nc=nc),
        grid=(n_seq, rows // tm, nc),
        in_specs=[row_spec(D_MODEL), row_spec(D_MODEL),
                  gate_cols((D_MODEL, ck)), val_cols((D_MODEL, ck)),
                  pl.BlockSpec((ck, D_MODEL), lambda s, i, c: (c, 0)),
                  gate_cols((FFN_KERNEL, ck)), val_cols((FFN_KERNEL, ck)),
                  gate_cols((1, ck)), val_cols((1, ck)),
                  pl.BlockSpec((1, halo, ck), lambda s, i, c: (s, 0, c)),
                  pl.BlockSpec((1, halo, ck), lambda s, i, c: (s, 0, c + nc)),
                  pl.BlockSpec((1, D_MODEL), lambda s, i, c: (0, 0))],
        out_specs=[row_spec(D_MODEL),
                   pl.BlockSpec((1, 1, halo, ck), lambda s, i, c: (s, i, 0, c)),
                   pl.BlockSpec((1, 1, halo, ck), lambda s, i, c: (s, i, 0, c))],
        out_shape=[jax.ShapeDtypeStruct((n_seq, rows, D_MODEL), F32),
                   jax.ShapeDtypeStruct((n_seq, rows // tm, halo, D_FF), F32),
                   jax.ShapeDtypeStruct((n_seq, rows // tm, halo, D_FF), F32)],
        scratch_shapes=[pltpu.VMEM((halo + tm, ck), F32), pltpu.VMEM((halo + tm, ck), F32),
                        pltpu.VMEM((nc, halo, ck), F32), pltpu.VMEM((nc, halo, ck), F32),
                        pltpu.VMEM((tm, D_MODEL), F32)],
        compiler_params=pltpu.CompilerParams(dimension_semantics=("arbitrary",) * 3,
                                             vmem_limit_bytes=VMEM_LIMIT),
        name="ffn_s%d" % stride,
    )(hn, x1, wup_bf, wup_bf, wdn_bf, dw, dw, db, db, state, state, gpo)


def _tile(rows, want):
    return want if rows % want == 0 else rows


def kernel(x_prompt, x_sample, cache_k, cache_v, state_conv, state_ffn, page_table, rel_bias, g_pre_mix, w_in,
           conv_dw_w, conv_dw_b, conv_ln_g, conv_ln_b, w_out, g_post_mix, g_pre_ffn, w_ffn_up, ffn_dw_w,
           ffn_dw_b, w_ffn_down, g_post_ffn):
    depth = w_in.shape[0]
    assert depth == 1, "single-layer trunk"
    bp, seq, _ = x_prompt.shape
    bs, n_new, _ = x_sample.shape
    n_pool, page = cache_k.shape[1], cache_k.shape[2]
    n_pages = page_table.shape[1]
    assert seq % MOBA_BLOCK == 0 and page == LANES and (n_pages * page) % MOBA_BLOCK == 0
    assert n_new == SUBLANES and n_pages * page // MOBA_BLOCK <= LANES

    w_in_bf = w_in[0].astype(BF16)
    w_out_bf = w_out[0].astype(BF16)
    w_up_bf = w_ffn_up[0].astype(BF16)
    w_dn_bf = w_ffn_down[0].astype(BF16)
    cw, cb = conv_dw_w[0], conv_dw_b
    near_t, near_s = _bias_tiles(rel_bias)
    conv_halo_p = 32
    ffn_halo_p = SUBLANES

    tm_p = _tile(seq, 512)
    qt, kt, vt, kbf, vtb, kmean, glu_p = _proj(x_prompt, g_pre_mix, w_in_bf, prompt=True, tm=tm_p)
    att_p = _attn_prompt(qt, kbf, vtb, kmean.reshape(bp, seq // MOBA_BLOCK, ATT_WIDTH), near_t)
    x1_p, hn_p = _mix(att_p, glu_p, x_prompt, jnp.zeros((bp, conv_halo_p, CONV_WIDTH), F32), cw, cb,
                      conv_ln_g, conv_ln_b, w_out_bf, g_post_mix, g_pre_ffn, tm=tm_p, stride=1)
    y_prompt, tail_g, tail_v = _ffn(hn_p, x1_p, w_up_bf, w_dn_bf, ffn_dw_w[0], ffn_dw_b,
                                    jnp.zeros((bp, ffn_halo_p, 2 * D_FF), F32), g_post_ffn,
                                    tm=tm_p, stride=1, nc=2)
    k_prompt = kt.reshape(1, bp, N_HEADS, HEAD_DIM, seq).transpose(0, 1, 4, 2, 3)
    v_prompt = vt.reshape(1, bp, N_HEADS, HEAD_DIM, seq).transpose(0, 1, 4, 2, 3)
    conv_prompt = glu_p[:, seq - (CONV_KERNEL - 1):, :][None]
    ffn_prompt = jnp.concatenate([tail_g[:, -1], tail_v[:, -1]],
                                 axis=-1)[:, ffn_halo_p - (FFN_KERNEL - 1):, :][None]

    rows_s = n_new * bs
    to_tb = lambda a: a.transpose(1, 0, 2).reshape(1, rows_s, a.shape[-1])
    to_bt = lambda a: a.reshape(n_new, bs, a.shape[-1]).transpose(1, 0, 2)
    xs = to_tb(x_sample)
    q_s, k_s, v_s, glu_s = _proj(xs, g_pre_mix, w_in_bf, prompt=False, tm=rows_s)
    q_b, k_b, v_b = to_bt(q_s), to_bt(k_s), to_bt(v_s)
    cache_kt = cache_k[0].transpose(0, 2, 3, 1).reshape(n_pool, ATT_WIDTH, page)
    cache_vt = cache_v[0].transpose(0, 2, 3, 1).reshape(n_pool, ATT_WIDTH, page)
    att_b = _attn_sample(page_table, rel_bias, q_b, k_b, v_b, near_s, cache_kt, cache_vt,
                         nbuf=min(64, n_pages))
    conv_state = state_conv[0].transpose(1, 0, 2).reshape(1, (CONV_KERNEL - 1) * bs, CONV_WIDTH)
    x1_s, hn_s = _mix(to_tb(att_b), glu_s, xs, conv_state, cw, cb, conv_ln_g, conv_ln_b, w_out_bf,
                      g_post_mix, g_pre_ffn, tm=rows_s, stride=bs)
    ffn_state = state_ffn[0].transpose(1, 0, 2).reshape(1, (FFN_KERNEL - 1) * bs, 2 * D_FF)
    y_s, tail_gs, tail_vs = _ffn(hn_s, x1_s, w_up_bf, w_dn_bf, ffn_dw_w[0], ffn_dw_b, ffn_state, g_post_ffn,
                                 tm=rows_s, stride=bs, nc=2)
    y_sample = to_bt(y_s)
    k_sample = k_b.reshape(1, bs, n_new, N_HEADS, HEAD_DIM)
    v_sample = v_b.reshape(1, bs, n_new, N_HEADS, HEAD_DIM)
    conv_all = jnp.concatenate([conv_state[0], glu_s[0]], axis=0)[n_new * bs:]
    conv_sample = conv_all.reshape(CONV_KERNEL - 1, bs, CONV_WIDTH).transpose(1, 0, 2)[None]
    ffn_sample = jnp.concatenate([tail_gs[:, -1], tail_vs[:, -1]], axis=-1).reshape(
        FFN_KERNEL - 1, bs, 2 * D_FF).transpose(1, 0, 2)[None]

    return (y_prompt, y_sample, k_prompt, v_prompt, conv_prompt, ffn_prompt,
            k_sample, v_sample, conv_sample, ffn_sample)
```

```python
import functools
import math

import numpy as np
import jax
import jax.numpy as jnp
from jax import lax
from jax.experimental import pallas as pl
from jax.experimental.pallas import tpu as pltpu

F32 = jnp.float32
BF16 = jnp.bfloat16

D_MODEL = 1024
HEAD_DIM = 64
ATT_WIDTH = 512
N_HEADS = 8
N_PAIRS = N_HEADS // 2
CONV_WIDTH = 512
CONV_KERNEL = 31
MOBA_BLOCK = 256
MOBA_TOPK = 3
N_BUCKETS = 32
MAX_DISTANCE = 128
D_FF = 2816
FFN_KERNEL = 3
EPS = 1e-6
SCALE = HEAD_DIM ** -0.5
LOG2E = math.log2(math.e)
NEG = -1e30
LANES = 128
SUBLANES = 8
VMEM_LIMIT = 56 * 1024 * 1024
MXU_COLS = 256
ONES_ROWS = 16
V_ROWS = HEAD_DIM + ONES_ROWS


def _bucket_thresholds():
    n = np.arange(0, 4 * MAX_DISTANCE)
    max_exact = N_BUCKETS // 2
    nf = np.maximum(n, 1).astype(np.float64)
    large = max_exact + (np.log(nf / max_exact) / math.log(MAX_DISTANCE / max_exact)
                         * (N_BUCKETS - max_exact)).astype(np.int64)
    bucket = np.where(n < max_exact, n, np.minimum(large, N_BUCKETS - 1))
    return tuple(int(np.argmax(bucket >= k)) for k in range(1, N_BUCKETS))


_BUCKET_THR = _bucket_thresholds()


def _rms(x, g):
    return x * lax.rsqrt(jnp.mean(x * x, axis=-1, keepdims=True) + EPS) * g


def _bias_kernel(tbl_ref, near_t_ref, near_s_ref):
    p = pl.program_id(0)

    def bias_of(dist, h):
        b = jnp.full(dist.shape, tbl_ref[0, h], F32)
        for k in range(1, N_BUCKETS):
            b = jnp.where(dist >= _BUCKET_THR[k - 1], tbl_ref[k, h], b)
        return jnp.where(dist >= 0, b, NEG)

    jj = lax.broadcasted_iota(jnp.int32, (2 * MOBA_BLOCK, MOBA_BLOCK), 0)
    ii = lax.broadcasted_iota(jnp.int32, (2 * MOBA_BLOCK, MOBA_BLOCK), 1)
    dist = MOBA_BLOCK + ii - jj
    for half in range(2):
        h = 2 * p + half
        near_t_ref[0, :, half * MOBA_BLOCK:(half + 1) * MOBA_BLOCK] = (
            bias_of(dist, h) - tbl_ref[N_BUCKETS - 1, h]) * LOG2E

    @pl.when(p == 0)
    def _():
        tt = lax.broadcasted_iota(jnp.int32, (SUBLANES, 2 * MOBA_BLOCK), 0)
        j2 = lax.broadcasted_iota(jnp.int32, (SUBLANES, 2 * MOBA_BLOCK), 1)
        d2 = MOBA_BLOCK + tt - j2
        for h in range(N_HEADS):
            near_s_ref[h * SUBLANES:(h + 1) * SUBLANES, :] = bias_of(d2, h)


def _bias_tiles(rel_bias):
    return pl.pallas_call(
        _bias_kernel,
        grid=(N_PAIRS,),
        in_specs=[pl.BlockSpec(memory_space=pltpu.SMEM)],
        out_specs=[pl.BlockSpec((1, 2 * MOBA_BLOCK, 2 * MOBA_BLOCK), lambda p: (p, 0, 0)),
                   pl.BlockSpec((N_HEADS * SUBLANES, 2 * MOBA_BLOCK), lambda p: (0, 0))],
        out_shape=[jax.ShapeDtypeStruct((N_PAIRS, 2 * MOBA_BLOCK, 2 * MOBA_BLOCK), F32),
                   jax.ShapeDtypeStruct((N_HEADS * SUBLANES, 2 * MOBA_BLOCK), F32)],
        compiler_params=pltpu.CompilerParams(dimension_semantics=("arbitrary",)),
        name="bias_tiles",
    )(rel_bias)


def _proj_kernel(x_ref, g_ref, w_ref, *out_refs, prompt, tm):
    h = _rms(x_ref[0], g_ref[...])
    p = jnp.dot(h.astype(BF16), w_ref[...], preferred_element_type=F32)
    q = p[:, 0:ATT_WIDTH]
    k = p[:, ATT_WIDTH:2 * ATT_WIDTH]
    v = p[:, 2 * ATT_WIDTH:3 * ATT_WIDTH]
    ga = p[:, 3 * ATT_WIDTH:3 * ATT_WIDTH + CONV_WIDTH]
    gb = p[:, 3 * ATT_WIDTH + CONV_WIDTH:]
    glu = ga * jax.nn.sigmoid(gb)
    if prompt:
        qt_ref, kt_ref, vt_ref, kbf_ref, vtb_ref, km_ref, glu_ref = out_refs
        qt_ref[0] = q.T
        kt_ref[0] = k.T
        vt = v.T
        vt_ref[0] = vt
        row_blk = (pl.program_id(1) * (tm // MOBA_BLOCK)
                   + lax.broadcasted_iota(jnp.int32, (tm, LANES), 0) // MOBA_BLOCK)
        onehot = jnp.where(lax.broadcasted_iota(jnp.int32, (tm, LANES), 1) == row_blk, 1.0, 0.0).astype(BF16)
        kb = k.astype(BF16)
        for pr in range(N_PAIRS):
            kbf_ref[0, :, 2 * pr * LANES:(2 * pr + 1) * LANES] = kb[:, pr * LANES:(pr + 1) * LANES]
            kbf_ref[0, :, (2 * pr + 1) * LANES:(2 * pr + 2) * LANES] = onehot
        ones = jnp.ones((ONES_ROWS, MOBA_BLOCK), F32)
        for c in range(tm // MOBA_BLOCK):
            vc = vt[:, c * MOBA_BLOCK:(c + 1) * MOBA_BLOCK]
            parts = []
            for hd in range(N_HEADS):
                parts += [vc[hd * HEAD_DIM:(hd + 1) * HEAD_DIM], ones]
            vtb_ref[0, c] = jnp.concatenate(parts, axis=0).astype(BF16)
        km_ref[0] = jnp.mean(k.reshape(tm // MOBA_BLOCK, MOBA_BLOCK, ATT_WIDTH), axis=1)[:, None, :]
        glu_ref[0] = glu
    else:
        q_ref, k_ref, v_ref, glu_ref = out_refs
        q_ref[0] = q
        k_ref[0] = k
        v_ref[0] = v
        glu_ref[0] = glu


def _proj(x, g, w_bf, *, prompt, tm):
    n_seq, rows, _ = x.shape
    n_out = w_bf.shape[1]
    grid = (n_seq, rows // tm)
    row_spec = lambda width: pl.BlockSpec((1, tm, width), lambda s, i: (s, i, 0))
    col_spec = pl.BlockSpec((1, ATT_WIDTH, tm), lambda s, i: (s, 0, i))
    if prompt:
        nb = tm // MOBA_BLOCK
        out_specs = [col_spec, col_spec, col_spec, row_spec(2 * ATT_WIDTH),
                     pl.BlockSpec((1, nb, N_HEADS * V_ROWS, MOBA_BLOCK), lambda s, i: (s, i, 0, 0)),
                     pl.BlockSpec((1, nb, 1, ATT_WIDTH), lambda s, i: (s, i, 0, 0)),
                     row_spec(CONV_WIDTH)]
        t_shape = jax.ShapeDtypeStruct((n_seq, ATT_WIDTH, rows), F32)
        out_shape = [t_shape, t_shape, t_shape,
                     jax.ShapeDtypeStruct((n_seq, rows, 2 * ATT_WIDTH), BF16),
                     jax.ShapeDtypeStruct((n_seq, rows // MOBA_BLOCK, N_HEADS * V_ROWS, MOBA_BLOCK), BF16),
                     jax.ShapeDtypeStruct((n_seq, rows // MOBA_BLOCK, 1, ATT_WIDTH), F32),
                     jax.ShapeDtypeStruct((n_seq, rows, CONV_WIDTH), F32)]
    else:
        out_specs = [row_spec(ATT_WIDTH)] * 3 + [row_spec(CONV_WIDTH)]
        out_shape = [jax.ShapeDtypeStruct((n_seq, rows, ATT_WIDTH), F32)] * 3 + [
            jax.ShapeDtypeStruct((n_seq, rows, CONV_WIDTH), F32)]
    return pl.pallas_call(
        functools.partial(_proj_kernel, prompt=prompt, tm=tm),
        grid=grid,
        in_specs=[row_spec(D_MODEL),
                  pl.BlockSpec((1, D_MODEL), lambda s, i: (0, 0)),
                  pl.BlockSpec((D_MODEL, n_out), lambda s, i: (0, 0))],
        out_specs=out_specs,
        out_shape=out_shape,
        compiler_params=pltpu.CompilerParams(dimension_semantics=("arbitrary", "arbitrary"),
                                             vmem_limit_bytes=VMEM_LIMIT),
        name="proj_prompt" if prompt else "proj_sample",
    )(x, g, w_bf)


def _select_topk(gate, idx, axis, n_valid):
    big = jnp.int32(2 ** 30)
    g = jnp.where(idx < n_valid, gate, -jnp.inf)
    sel = jnp.zeros(gate.shape, jnp.bool_)
    for _ in range(MOBA_TOPK):
        mx = jnp.max(g, axis=axis, keepdims=True)
        first = jnp.min(jnp.where(g == mx, idx, big), axis=axis, keepdims=True)
        hit = idx == first
        sel = sel | (hit & (mx > -jnp.inf))
        g = jnp.where(hit, -jnp.inf, g)
    return sel


def _attn_prompt_kernel(qt_ref, qtn_ref, k_ref, vt_ref, km_ref, near_ref, o_ref, qaug_ref, mnext_ref,
                        sa_ref, sb_ref, pa_ref, pb_ref, *, nblk):
    i = pl.program_id(2)
    blk = MOBA_BLOCK

    def block_diag(qt):
        z = jnp.zeros((HEAD_DIM, blk), F32)
        return jnp.concatenate([jnp.concatenate([qt[:HEAD_DIM], z], axis=1),
                                jnp.concatenate([z, qt[HEAD_DIM:]], axis=1)], axis=0)

    def mask_rows(q2t, tile):
        gate = jnp.dot(km_ref[0], q2t, precision=lax.Precision.HIGHEST, preferred_element_type=F32)
        n_idx = lax.broadcasted_iota(jnp.int32, (nblk, 2 * blk), 0)
        sel = _select_topk(gate, n_idx, 0, tile)
        return jnp.concatenate([jnp.where(sel | (n_idx == tile), 0.0, NEG),
                                jnp.full((LANES - nblk, 2 * blk), NEG, F32)], axis=0).astype(BF16)

    qaug_ref[0:LANES, :] = (block_diag(qt_ref[0]) * (SCALE * LOG2E)).astype(BF16)

    @pl.when(i == 0)
    def _():
        row = lax.broadcasted_iota(jnp.int32, (LANES, 2 * blk), 0)
        qaug_ref[LANES:, :] = jnp.where(row == 0, 0.0, NEG).astype(BF16)

    @pl.when(i > 0)
    def _():
        qaug_ref[LANES:, :] = mnext_ref[...]

    def scores_into(dst_ref, j0, j1, bias):
        mx = None
        for half, j in enumerate((j0, j1)):
            kb = k_ref[0, pl.ds(pl.multiple_of(j * blk, blk), blk), :]
            s = jnp.dot(kb, qaug_ref[...], preferred_element_type=F32)
            if bias is not None:
                s = s + bias[half]
            dst_ref[half] = s
            smax = jnp.max(s, axis=0, keepdims=True)
            mx = smax if mx is None else jnp.maximum(mx, smax)
        return mx

    def probs(src_ref, dst_ref, m, mx):
        m_new = jnp.maximum(m, mx)
        dst_ref[0] = jnp.exp2(src_ref[0] - m_new).astype(BF16)
        dst_ref[1] = jnp.exp2(src_ref[1] - m_new).astype(BF16)
        return m_new, jnp.exp2(m - m_new)

    def accumulate(p_ref, j0, j1, alpha, a_a, a_b):
        p0, p1 = p_ref[0], p_ref[1]
        v0, v1 = vt_ref[0, j0], vt_ref[0, j1]
        d_a = (jnp.dot(v0[:V_ROWS], p0[:, :blk], preferred_element_type=F32)
               + jnp.dot(v1[:V_ROWS], p1[:, :blk], preferred_element_type=F32))
        d_b = (jnp.dot(v0[V_ROWS:], p0[:, blk:], preferred_element_type=F32)
               + jnp.dot(v1[V_ROWS:], p1[:, blk:], preferred_element_type=F32))
        return a_a * alpha[:, :blk] + d_a, a_b * alpha[:, blk:] + d_b

    none = nblk - 1
    prev = jnp.where(i > 0, i - 1, none)
    n_far = jnp.maximum(i - 1, 0)
    n_pairs = (n_far + 1) // 2

    mx = scores_into(sa_ref, i, prev, (near_ref[0, blk:, :], near_ref[0, :blk, :]))
    mnext_ref[...] = mask_rows(block_diag(qtn_ref[0]), i + 1)
    pb_ref[...] = jnp.zeros(pb_ref.shape, BF16)
    zero = jnp.zeros((V_ROWS, blk), F32)
    state = (jnp.full((1, 2 * blk), NEG, F32), mx, jnp.ones((1, 2 * blk), F32), zero, zero, i, prev, i, i)

    def stage(s_src, s_dst, p_dst, p_src, t, state):
        m, mx, alpha_p, a_a, a_b, js0, js1, jp0, jp1 = state
        j0 = 2 * t
        j1 = jnp.where(j0 + 1 < n_far, j0 + 1, none)
        mx_next = scores_into(s_dst, j0, j1, None)
        m_new, alpha = probs(s_src, p_dst, m, mx)
        a_a, a_b = accumulate(p_src, jp0, jp1, alpha_p, a_a, a_b)
        return m_new, mx_next, alpha, a_a, a_b, j0, j1, js0, js1

    def step(t, state):
        return lax.cond(t % 2 == 0,
                        lambda st: stage(sa_ref, sb_ref, pa_ref, pb_ref, t, st),
                        lambda st: stage(sb_ref, sa_ref, pb_ref, pa_ref, t, st), state)

    def drain(s_src, p_dst, p_src, state):
        m, mx, alpha_p, a_a, a_b, js0, js1, jp0, jp1 = state
        a_a, a_b = accumulate(p_src, jp0, jp1, alpha_p, a_a, a_b)
        _, alpha = probs(s_src, p_dst, m, mx)
        return accumulate(p_dst, js0, js1, alpha, a_a, a_b)

    state = lax.fori_loop(0, n_pairs, step, state)
    a_a, a_b = lax.cond(n_pairs % 2 == 0,
                        lambda st: drain(sa_ref, pa_ref, pb_ref, st),
                        lambda st: drain(sb_ref, pb_ref, pa_ref, st), state)
    out_t = jnp.concatenate([a_a[:HEAD_DIM] / a_a[HEAD_DIM:HEAD_DIM + 1],
                             a_b[:HEAD_DIM] / a_b[HEAD_DIM:HEAD_DIM + 1]], axis=0)
    o_ref[0] = out_t.T


def _attn_prompt(qt, kaug, vtb, kmean, near_t):
    n_seq, _, t = qt.shape
    nblk = t // MOBA_BLOCK
    assert nblk % 2 == 0 and nblk < LANES
    return pl.pallas_call(
        functools.partial(_attn_prompt_kernel, nblk=nblk),
        grid=(n_seq, N_PAIRS, nblk),
        in_specs=[pl.BlockSpec((1, LANES, MOBA_BLOCK), lambda b, p, i: (b, p, i)),
                  pl.BlockSpec((1, LANES, MOBA_BLOCK), lambda b, p, i: (b, p, jnp.minimum(i + 1, nblk - 1))),
                  pl.BlockSpec((1, t, 2 * LANES), lambda b, p, i: (b, 0, p)),
                  pl.BlockSpec((1, nblk, 2 * V_ROWS, MOBA_BLOCK), lambda b, p, i: (b, 0, p, 0)),
                  pl.BlockSpec((1, nblk, LANES), lambda b, p, i: (b, 0, p)),
                  pl.BlockSpec((1, 2 * MOBA_BLOCK, 2 * MOBA_BLOCK), lambda b, p, i: (p, 0, 0))],
        out_specs=pl.BlockSpec((1, MOBA_BLOCK, LANES), lambda b, p, i: (b, i, p)),
        out_shape=jax.ShapeDtypeStruct((n_seq, t, ATT_WIDTH), F32),
        scratch_shapes=[pltpu.VMEM((2 * LANES, 2 * MOBA_BLOCK), BF16),
                        pltpu.VMEM((LANES, 2 * MOBA_BLOCK), BF16),
                        pltpu.VMEM((2, MOBA_BLOCK, 2 * MOBA_BLOCK), F32),
                        pltpu.VMEM((2, MOBA_BLOCK, 2 * MOBA_BLOCK), F32),
                        pltpu.VMEM((2, MOBA_BLOCK, 2 * MOBA_BLOCK), BF16),
                        pltpu.VMEM((2, MOBA_BLOCK, 2 * MOBA_BLOCK), BF16)],
        compiler_params=pltpu.CompilerParams(dimension_semantics=("arbitrary",) * 3,
                                             vmem_limit_bytes=VMEM_LIMIT),
        name="attn_prompt",
    )(qt, qt, kaug, vtb, kmean, near_t)


def _attn_sample_kernel(pt_ref, tbl_ref, q_ref, kn_ref, vn_ref, near_ref, ck_hbm, cv_hbm, o_ref,
                        s_all, km_t, kbuf, vbuf, ksem, vsem, *, n_pages, nbuf, step, n_new):
    b = pl.program_id(0)
    n_seq = pl.num_programs(0)
    n_rows = N_HEADS * n_new
    page = LANES
    nblk = n_pages * page // MOBA_BLOCK

    def k_copy(seq, p, slot):
        return pltpu.make_async_copy(ck_hbm.at[pt_ref[seq, p]], kbuf.at[slot], ksem.at[slot])

    def v_copy(p, slot):
        return pltpu.make_async_copy(cv_hbm.at[pt_ref[b, p]], vbuf.at[slot], vsem.at[slot])

    @pl.when(b == 0)
    def _():
        for p in range(nbuf):
            k_copy(0, p, p).start()

    row_head = lax.broadcasted_iota(jnp.int32, (n_rows, ATT_WIDTH), 0) // n_new
    col_head = lax.broadcasted_iota(jnp.int32, (n_rows, ATT_WIDTH), 1) // HEAD_DIM
    diag = row_head == col_head
    q = q_ref[0]
    qbd_f = jnp.where(diag, jnp.concatenate([q] * N_HEADS, axis=0), 0.0)
    qbd_b = (qbd_f * SCALE).astype(BF16)
    km_t[...] = jnp.zeros(km_t.shape, F32)
    lane_k = lax.broadcasted_iota(jnp.int32, (ATT_WIDTH, LANES), 1)

    def k_step(it, _):
        p0 = it * step
        slot0 = p0 % nbuf
        for r in range(step):
            k_copy(b, p0 + r, slot0 + r).wait()
        km = km_t[...]
        for r in range(0, step, 2):
            kt0 = kbuf[slot0 + r]
            kt1 = kbuf[slot0 + r + 1]
            s_all[p0 + r] = jnp.dot(qbd_b, kt0.astype(BF16), preferred_element_type=F32)
            s_all[p0 + r + 1] = jnp.dot(qbd_b, kt1.astype(BF16), preferred_element_type=F32)
            mean = jnp.sum(kt0 + kt1, axis=1, keepdims=True) * (1.0 / MOBA_BLOCK)
            km = jnp.where(lane_k == (p0 + r) // 2, mean, km)
        km_t[...] = km

        @pl.when(p0 + nbuf < n_pages)
        def _():
            for r in range(step):
                k_copy(b, p0 + nbuf + r, slot0 + r).start()

        return 0

    lax.fori_loop(0, n_pages // step, k_step, 0)

    @pl.when(b + 1 < n_seq)
    def _():
        for p in range(nbuf):
            k_copy(b + 1, p, p).start()

    for p in range(nbuf):
        v_copy(p, p).start()

    gate = jnp.dot(qbd_f, km_t[...], precision=lax.Precision.HIGHEST,
                   preferred_element_type=F32)
    lane = lax.broadcasted_iota(jnp.int32, (n_rows, LANES), 1)
    sel = _select_topk(gate, lane, 1, nblk)
    rh = lax.broadcasted_iota(jnp.int32, (n_rows, 1), 0) // n_new
    far = jnp.zeros((n_rows, 1), F32)
    for h in range(N_HEADS):
        far = jnp.where(rh == h, tbl_ref[N_BUCKETS - 1, h], far)
    mask = jnp.where(sel, far, NEG)
    near = near_ref[...]
    for n in range(nblk):
        if n == nblk - 1:
            seln = jnp.broadcast_to(mask[:, n:n + 1], (n_rows, page)) > 0.5 * NEG
            s_all[2 * n] = s_all[2 * n] + jnp.where(seln, near[:, 0:page], NEG)
            s_all[2 * n + 1] = s_all[2 * n + 1] + jnp.where(seln, near[:, page:2 * page], NEG)
        else:
            add = jnp.broadcast_to(mask[:, n:n + 1], (n_rows, page))
            s_all[2 * n] = s_all[2 * n] + add
            s_all[2 * n + 1] = s_all[2 * n + 1] + add
    kn = jnp.concatenate([kn_ref[0], jnp.zeros((page - n_new, ATT_WIDTH), F32)], axis=0).astype(BF16)
    s_own = lax.dot_general(qbd_b, kn, (((1,), (1,)), ((), ())),
                            preferred_element_type=F32) + near[:, 2 * page:3 * page]
    mrun = lax.fori_loop(0, n_pages, lambda j, mm: jnp.maximum(mm, s_all[j]), s_own)
    m = jnp.max(mrun, axis=1, keepdims=True)
    p_own = jnp.exp(s_own - m)

    def exp_step(j, lsum):
        pj = jnp.exp(s_all[j] - m)
        s_all[j] = pj
        return lsum + pj

    lsum = lax.fori_loop(0, n_pages, exp_step, p_own)
    l = jnp.sum(lsum, axis=1, keepdims=True)
    vn = jnp.concatenate([vn_ref[0], jnp.zeros((page - n_new, ATT_WIDTH), F32)], axis=0).astype(BF16)
    acc = jnp.dot(p_own.astype(BF16), vn, preferred_element_type=F32)

    def v_step(it, acc):
        p0 = it * step
        slot0 = p0 % nbuf
        for r in range(step):
            v_copy(p0 + r, slot0 + r).wait()
        parts = []
        for r in range(step):
            pb = s_all[p0 + r].astype(BF16)
            vt = vbuf[slot0 + r].astype(BF16)
            parts.append(lax.dot_general(pb, vt, (((1,), (1,)), ((), ())), preferred_element_type=F32))
        while len(parts) > 1:
            parts = [x + y for x, y in zip(parts[0::2], parts[1::2])]

        @pl.when(p0 + nbuf < n_pages)
        def _():
            for r in range(step):
                v_copy(p0 + nbuf + r, slot0 + r).start()

        return acc + parts[0]

    acc = lax.fori_loop(0, n_pages // step, v_step, acc)
    o = jnp.where(diag, acc / l, 0.0)
    o_ref[0] = jnp.sum(o.reshape(N_HEADS, n_new, ATT_WIDTH), axis=0)


def _attn_sample(page_table, rel_bias, q, k_new, v_new, near_s, cache_kt, cache_vt, *, nbuf):
    n_seq, n_new, _ = q.shape
    n_pages = page_table.shape[1]
    n_rows = N_HEADS * n_new
    page = cache_kt.shape[2]
    step = min(8, nbuf)
    assert step & (step - 1) == 0 and step >= 2 and nbuf % step == 0 and n_pages % nbuf == 0
    new_spec = pl.BlockSpec((1, n_new, ATT_WIDTH), lambda b, pt: (b, 0, 0))
    in_specs = [pl.BlockSpec(memory_space=pltpu.SMEM), new_spec, new_spec, new_spec,
                pl.BlockSpec((n_rows, 2 * MOBA_BLOCK), lambda b, pt: (0, 0)),
                pl.BlockSpec(memory_space=pl.ANY), pl.BlockSpec(memory_space=pl.ANY)]
    return pl.pallas_call(
        functools.partial(_attn_sample_kernel, n_pages=n_pages, nbuf=nbuf, step=step, n_new=n_new),
        grid_spec=pltpu.PrefetchScalarGridSpec(
            num_scalar_prefetch=1,
            grid=(n_seq,),
            in_specs=in_specs,
            out_specs=new_spec,
            scratch_shapes=[pltpu.VMEM((n_pages, n_rows, page), F32),
                            pltpu.VMEM((ATT_WIDTH, LANES), F32),
                            pltpu.VMEM((nbuf, ATT_WIDTH, page), F32),
                            pltpu.VMEM((nbuf, ATT_WIDTH, page), F32),
                            pltpu.SemaphoreType.DMA((nbuf,)),
                            pltpu.SemaphoreType.DMA((nbuf,))]),
        out_shape=jax.ShapeDtypeStruct((n_seq, n_new, ATT_WIDTH), F32),
        compiler_params=pltpu.CompilerParams(dimension_semantics=("arbitrary",),
                                             vmem_limit_bytes=VMEM_LIMIT),
        name="attn_sample",
    )(page_table, rel_bias, q, k_new, v_new, near_s, cache_kt, cache_vt)


def _conv_offsets(halo, stride):
    off0 = halo - (CONV_KERNEL - 1) * stride
    return [off0 + j * stride for j in range(CONV_KERNEL)]


def _shift_classes(offs):
    return sorted({o % SUBLANES for o in offs} - {0})


def _mix_kernel(att_ref, glu_ref, x_ref, st_ref, cw_ref, cb_ref, lg_ref, lb_ref, wo_ref, gpm_ref, gpf_ref,
                x1_ref, hn_ref, gbuf, cvbuf, *shifted, tm, stride, halo, chunk):
    i = pl.program_id(1)
    group = min(tm, 128)

    @pl.when(i == 0)
    def _():
        gbuf[0:halo, :] = st_ref[0]

    gbuf[halo:halo + tm, :] = glu_ref[0]
    offs = _conv_offsets(halo, stride)
    for r, ref in zip(_shift_classes(offs), shifted):
        ref[...] = gbuf[r:r + ref.shape[0], :]
    src = dict(zip(_shift_classes(offs), shifted))
    src[0] = gbuf
    for c in range(tm // chunk):
        r0 = c * chunk
        acc = jnp.broadcast_to(cb_ref[...], (chunk, CONV_WIDTH))
        for j, o in enumerate(offs):
            a = o - o % SUBLANES + r0
            acc = acc + cw_ref[j:j + 1, :] * src[o % SUBLANES][a:a + chunk, :]
        mu = jnp.mean(acc, axis=-1, keepdims=True)
        d = acc - mu
        var = jnp.mean(d * d, axis=-1, keepdims=True)
        y = d * lax.rsqrt(var + EPS) * lg_ref[...] + lb_ref[...]
        cvbuf[r0:r0 + chunk, :] = (y * jax.nn.sigmoid(y)).astype(BF16)

        g1 = r0 + chunk
        if g1 % group == 0 or g1 == tm:
            g0 = (g1 - 1) // group * group
            mix = (jnp.dot(att_ref[0, g0:g1, :].astype(BF16), wo_ref[0:ATT_WIDTH, :], preferred_element_type=F32)
                   + jnp.dot(cvbuf[g0:g1, :], wo_ref[ATT_WIDTH:, :], preferred_element_type=F32))
            x1 = x_ref[0, g0:g1, :] + _rms(mix, gpm_ref[...])
            x1_ref[0, g0:g1, :] = x1
            hn_ref[0, g0:g1, :] = _rms(x1, gpf_ref[...]).astype(BF16)

    if tm >= halo:
        gbuf[0:halo, :] = gbuf[tm:tm + halo, :]


def _mix(att, glu, x, state, cw, cb, lg, lb, wo_bf, gpm, gpf, *, tm, stride):
    n_seq, rows, _ = x.shape
    halo = state.shape[1]
    assert rows == tm or tm >= halo
    offs = _conv_offsets(halo, stride)
    row_spec = lambda width: pl.BlockSpec((1, tm, width), lambda s, i: (s, i, 0))
    const = lambda shape: pl.BlockSpec(shape, lambda s, i: (0,) * len(shape))
    return pl.pallas_call(
        functools.partial(_mix_kernel, tm=tm, stride=stride, halo=halo, chunk=min(tm, 32)),
        grid=(n_seq, rows // tm),
        in_specs=[row_spec(ATT_WIDTH), row_spec(CONV_WIDTH), row_spec(D_MODEL),
                  pl.BlockSpec((1, halo, CONV_WIDTH), lambda s, i: (s, 0, 0)),
                  const((CONV_KERNEL, CONV_WIDTH)), const((1, CONV_WIDTH)), const((1, CONV_WIDTH)),
                  const((1, CONV_WIDTH)), const((D_MODEL, D_MODEL)), const((1, D_MODEL)), const((1, D_MODEL))],
        out_specs=[row_spec(D_MODEL), row_spec(D_MODEL)],
        out_shape=[jax.ShapeDtypeStruct((n_seq, rows, D_MODEL), F32),
                   jax.ShapeDtypeStruct((n_seq, rows, D_MODEL), BF16)],
        scratch_shapes=[pltpu.VMEM((halo + tm, CONV_WIDTH), F32), pltpu.VMEM((tm, CONV_WIDTH), BF16)] + [
            pltpu.VMEM((max(o - r for o in offs if o % SUBLANES == r) + tm, CONV_WIDTH), F32)
            for r in _shift_classes(offs)],
        compiler_params=pltpu.CompilerParams(dimension_semantics=("arbitrary", "arbitrary"),
                                             vmem_limit_bytes=VMEM_LIMIT),
        name="mix_s%d" % stride,
    )(att, glu, x, state, cw, cb, lg, lb, wo_bf, gpm, gpf)


def _ffn_kernel(hn_ref, x1_ref, wg_ref, wv_ref, wd_ref, dwg_ref, dwv_ref, dbg_ref, dbv_ref, stg_ref, stv_ref,
                gpo_ref, y_ref, tg_ref, tv_ref, ubuf_g, ubuf_v, car_g, car_v, acc_ref, *, tm, stride, halo, nc):
    i = pl.program_id(1)
    c = pl.program_id(2)
    hn = hn_ref[0]

    @pl.when(i == 0)
    def _():
        ubuf_g[0:halo, :] = stg_ref[0]
        ubuf_v[0:halo, :] = stv_ref[0]

    @pl.when(i > 0)
    def _():
        ubuf_g[0:halo, :] = car_g[c]
        ubuf_v[0:halo, :] = car_v[c]

    @pl.when(c == 0)
    def _():
        acc_ref[...] = jnp.zeros(acc_ref.shape, F32)

    def up(c0, c1):
        ubuf_g[halo:halo + tm, c0:c1] = jnp.dot(hn, wg_ref[:, c0:c1], preferred_element_type=F32)
        ubuf_v[halo:halo + tm, c0:c1] = jnp.dot(hn, wv_ref[:, c0:c1], preferred_element_type=F32)

    def conv(c0, c1, dw_ref, db_ref, ubuf):
        return (dw_ref[0:1, c0:c1] * ubuf[halo - 2 * stride:halo - 2 * stride + tm, c0:c1]
                + dw_ref[1:2, c0:c1] * ubuf[halo - stride:halo - stride + tm, c0:c1]
                + dw_ref[2:3, c0:c1] * ubuf[halo:halo + tm, c0:c1] + db_ref[:, c0:c1])

    ck = wg_ref.shape[1]
    bounds = [(c0, min(c0 + MXU_COLS, ck)) for c0 in range(0, ck, MXU_COLS)]
    def down(c0, c1, a):
        acc_ref[...] += jnp.dot(a, wd_ref[c0:c1, :], preferred_element_type=F32)

    up(*bounds[0])
    pending = None
    for n, (c0, c1) in enumerate(bounds):
        if n + 1 < len(bounds):
            up(*bounds[n + 1])
        if pending is not None:
            down(*pending)
        a = jax.nn.gelu(conv(c0, c1, dwg_ref, dbg_ref, ubuf_g), approximate=True) * conv(c0, c1, dwv_ref, dbv_ref,
                                                                                        ubuf_v)
        pending = (c0, c1, a.astype(BF16))
    down(*pending)

    for ubuf, car, tail_ref in ((ubuf_g, car_g, tg_ref), (ubuf_v, car_v, tv_ref)):
        last = ubuf[tm:tm + halo, :]
        car[c] = last
        tail_ref[0, 0] = last

    @pl.when(c == nc - 1)
    def _():
        y_ref[0] = x1_ref[0] + _rms(acc_ref[...], gpo_ref[...])


def _ffn(hn, x1, wup_bf, wdn_bf, dw, db, state, gpo, *, tm, stride, nc):
    n_seq, rows, _ = x1.shape
    halo = state.shape[1]
    ck = D_FF // nc
    row_spec = lambda width: pl.BlockSpec((1, tm, width), lambda s, i, c: (s, i, 0))
    mode = dict(pipeline_mode=pl.Buffered(1)) if nc == 1 else {}
    gate_cols = lambda shape, **kw: pl.BlockSpec(shape, lambda s, i, c: (0, c), **kw)
    val_cols = lambda shape, **kw: pl.BlockSpec(shape, lambda s, i, c: (0, c + nc), **kw)
    return pl.pallas_call(
        functools.partial(_ffn_kernel, tm=tm, stride=stride, halo=halo, nc=nc),
        grid=(n_seq, rows // tm, nc),
        in_specs=[row_spec(D_MODEL), row_spec(D_MODEL),
                  gate_cols((D_MODEL, ck), **mode), val_cols((D_MODEL, ck), **mode),
                  pl.BlockSpec((ck, D_MODEL), lambda s, i, c: (c, 0), **mode),
                  gate_cols((FFN_KERNEL, ck)), val_cols((FFN_KERNEL, ck)),
                  gate_cols((1, ck)), val_cols((1, ck)),
                  pl.BlockSpec((1, halo, ck), lambda s, i, c: (s, 0, c)),
                  pl.BlockSpec((1, halo, ck), lambda s, i, c: (s, 0, c + nc)),
                  pl.BlockSpec((1, D_MODEL), lambda s, i, c: (0, 0))],
        out_specs=[row_spec(D_MODEL),
                   pl.BlockSpec((1, 1, halo, ck), lambda s, i, c: (s, i, 0, c)),
                   pl.BlockSpec((1, 1, halo, ck), lambda s, i, c: (s, i, 0, c))],
        out_shape=[jax.ShapeDtypeStruct((n_seq, rows, D_MODEL), F32),
                   jax.ShapeDtypeStruct((n_seq, rows // tm, halo, D_FF), F32),
                   jax.ShapeDtypeStruct((n_seq, rows // tm, halo, D_FF), F32)],
        scratch_shapes=[pltpu.VMEM((halo + tm, ck), F32), pltpu.VMEM((halo + tm, ck), F32),
                        pltpu.VMEM((nc, halo, ck), F32), pltpu.VMEM((nc, halo, ck), F32),
                        pltpu.VMEM((tm, D_MODEL), F32)],
        compiler_params=pltpu.CompilerParams(dimension_semantics=("arbitrary",) * 3,
                                             vmem_limit_bytes=VMEM_LIMIT),
        name="ffn_s%d" % stride,
    )(hn, x1, wup_bf, wup_bf, wdn_bf, dw, dw, db, db, state, state, gpo)


def _tile(rows, want):
    return want if rows % want == 0 else rows


def kernel(x_prompt, x_sample, cache_k, cache_v, state_conv, state_ffn, page_table, rel_bias, g_pre_mix, w_in,
           conv_dw_w, conv_dw_b, conv_ln_g, conv_ln_b, w_out, g_post_mix, g_pre_ffn, w_ffn_up, ffn_dw_w,
           ffn_dw_b, w_ffn_down, g_post_ffn):
    depth = w_in.shape[0]
    assert depth == 1, "single-layer trunk"
    bp, seq, _ = x_prompt.shape
    bs, n_new, _ = x_sample.shape
    n_pool, page = cache_k.shape[1], cache_k.shape[2]
    n_pages = page_table.shape[1]
    assert seq % MOBA_BLOCK == 0 and page == LANES and (n_pages * page) % MOBA_BLOCK == 0
    assert n_new == SUBLANES and n_pages * page // MOBA_BLOCK <= LANES

    w_in_bf = w_in[0].astype(BF16)
    w_out_bf = w_out[0].astype(BF16)
    w_up_bf = w_ffn_up[0].astype(BF16)
    w_dn_bf = w_ffn_down[0].astype(BF16)
    cw, cb = conv_dw_w[0], conv_dw_b
    near_t, near_s = _bias_tiles(rel_bias)
    conv_halo_p = 32
    ffn_halo_p = SUBLANES

    tm_p = _tile(seq, 512)
    qt, kt, vt, kbf, vtb, kmean, glu_p = _proj(x_prompt, g_pre_mix, w_in_bf, prompt=True, tm=tm_p)
    att_p = _attn_prompt(qt, kbf, vtb, kmean.reshape(bp, seq // MOBA_BLOCK, ATT_WIDTH), near_t)
    x1_p, hn_p = _mix(att_p, glu_p, x_prompt, jnp.zeros((bp, conv_halo_p, CONV_WIDTH), F32), cw, cb,
                      conv_ln_g, conv_ln_b, w_out_bf, g_post_mix, g_pre_ffn, tm=tm_p, stride=1)
    y_prompt, tail_g, tail_v = _ffn(hn_p, x1_p, w_up_bf, w_dn_bf, ffn_dw_w[0], ffn_dw_b,
                                    jnp.zeros((bp, ffn_halo_p, 2 * D_FF), F32), g_post_ffn,
                                    tm=tm_p, stride=1, nc=1)
    k_prompt = kt.reshape(1, bp, N_HEADS, HEAD_DIM, seq).transpose(0, 1, 4, 2, 3)
    v_prompt = vt.reshape(1, bp, N_HEADS, HEAD_DIM, seq).transpose(0, 1, 4, 2, 3)
    conv_prompt = glu_p[:, seq - (CONV_KERNEL - 1):, :][None]
    ffn_prompt = jnp.concatenate([tail_g[:, -1], tail_v[:, -1]],
                                 axis=-1)[:, ffn_halo_p - (FFN_KERNEL - 1):, :][None]

    rows_s = n_new * bs
    to_tb = lambda a: a.transpose(1, 0, 2).reshape(1, rows_s, a.shape[-1])
    to_bt = lambda a: a.reshape(n_new, bs, a.shape[-1]).transpose(1, 0, 2)
    xs = to_tb(x_sample)
    q_s, k_s, v_s, glu_s = _proj(xs, g_pre_mix, w_in_bf, prompt=False, tm=rows_s)
    q_b, k_b, v_b = to_bt(q_s), to_bt(k_s), to_bt(v_s)
    cache_kt = cache_k[0].transpose(0, 2, 3, 1).reshape(n_pool, ATT_WIDTH, page)
    cache_vt = cache_v[0].transpose(0, 2, 3, 1).reshape(n_pool, ATT_WIDTH, page)
    att_b = _attn_sample(page_table, rel_bias, q_b, k_b, v_b, near_s, cache_kt, cache_vt,
                         nbuf=min(32, n_pages))
    conv_state = state_conv[0].transpose(1, 0, 2).reshape(1, (CONV_KERNEL - 1) * bs, CONV_WIDTH)
    x1_s, hn_s = _mix(to_tb(att_b), glu_s, xs, conv_state, cw, cb, conv_ln_g, conv_ln_b, w_out_bf,
                      g_post_mix, g_pre_ffn, tm=rows_s, stride=bs)
    ffn_state = state_ffn[0].transpose(1, 0, 2).reshape(1, (FFN_KERNEL - 1) * bs, 2 * D_FF)
    y_s, tail_gs, tail_vs = _ffn(hn_s, x1_s, w_up_bf, w_dn_bf, ffn_dw_w[0], ffn_dw_b, ffn_state, g_post_ffn,
                                 tm=rows_s, stride=bs, nc=2)
    y_sample = to_bt(y_s)
    k_sample = k_b.reshape(1, bs, n_new, N_HEADS, HEAD_DIM)
    v_sample = v_b.reshape(1, bs, n_new, N_HEADS, HEAD_DIM)
    conv_all = jnp.concatenate([conv_state[0], glu_s[0]], axis=0)[n_new * bs:]
    conv_sample = conv_all.reshape(CONV_KERNEL - 1, bs, CONV_WIDTH).transpose(1, 0, 2)[None]
    ffn_sample = jnp.concatenate([tail_gs[:, -1], tail_vs[:, -1]], axis=-1).reshape(
        FFN_KERNEL - 1, bs, 2 * D_FF).transpose(1, 0, 2)[None]

    return (y_prompt, y_sample, k_prompt, v_prompt, conv_prompt, ffn_prompt,
            k_sample, v_sample, conv_sample, ffn_sample)
```

```python
import functools
import math

import numpy as np
import jax
import jax.numpy as jnp
from jax import lax
from jax.experimental import pallas as pl
from jax.experimental.pallas import tpu as pltpu

F32 = jnp.float32
BF16 = jnp.bfloat16

D_MODEL = 1024
HEAD_DIM = 64
ATT_WIDTH = 512
N_HEADS = 8
N_PAIRS = N_HEADS // 2
CONV_WIDTH = 512
CONV_KERNEL = 31
MOBA_BLOCK = 256
MOBA_TOPK = 3
N_BUCKETS = 32
MAX_DISTANCE = 128
D_FF = 2816
FFN_KERNEL = 3
EPS = 1e-6
SCALE = HEAD_DIM ** -0.5
LOG2E = math.log2(math.e)
NEG = -1e30
LANES = 128
SUBLANES = 8
VMEM_LIMIT = 56 * 1024 * 1024
MXU_COLS = 256
ONES_ROWS = 16
V_ROWS = HEAD_DIM + ONES_ROWS


def _bucket_thresholds():
    n = np.arange(0, 4 * MAX_DISTANCE)
    max_exact = N_BUCKETS // 2
    nf = np.maximum(n, 1).astype(np.float64)
    large = max_exact + (np.log(nf / max_exact) / math.log(MAX_DISTANCE / max_exact)
                         * (N_BUCKETS - max_exact)).astype(np.int64)
    bucket = np.where(n < max_exact, n, np.minimum(large, N_BUCKETS - 1))
    return tuple(int(np.argmax(bucket >= k)) for k in range(1, N_BUCKETS))


_BUCKET_THR = _bucket_thresholds()


def _rms(x, g):
    return x * lax.rsqrt(jnp.mean(x * x, axis=-1, keepdims=True) + EPS) * g


def _bias_kernel(tbl_ref, near_t_ref, near_s_ref):
    p = pl.program_id(0)

    def bias_of(dist, h):
        b = jnp.full(dist.shape, tbl_ref[0, h], F32)
        for k in range(1, N_BUCKETS):
            b = jnp.where(dist >= _BUCKET_THR[k - 1], tbl_ref[k, h], b)
        return jnp.where(dist >= 0, b, NEG)

    jj = lax.broadcasted_iota(jnp.int32, (2 * MOBA_BLOCK, MOBA_BLOCK), 0)
    ii = lax.broadcasted_iota(jnp.int32, (2 * MOBA_BLOCK, MOBA_BLOCK), 1)
    dist = MOBA_BLOCK + ii - jj
    for half in range(2):
        h = 2 * p + half
        near_t_ref[0, :, half * MOBA_BLOCK:(half + 1) * MOBA_BLOCK] = (
            bias_of(dist, h) - tbl_ref[N_BUCKETS - 1, h]) * LOG2E

    @pl.when(p == 0)
    def _():
        tt = lax.broadcasted_iota(jnp.int32, (SUBLANES, 2 * MOBA_BLOCK), 0)
        j2 = lax.broadcasted_iota(jnp.int32, (SUBLANES, 2 * MOBA_BLOCK), 1)
        d2 = MOBA_BLOCK + tt - j2
        for h in range(N_HEADS):
            near_s_ref[h * SUBLANES:(h + 1) * SUBLANES, :] = bias_of(d2, h)


def _bias_tiles(rel_bias):
    return pl.pallas_call(
        _bias_kernel,
        grid=(N_PAIRS,),
        in_specs=[pl.BlockSpec(memory_space=pltpu.SMEM)],
        out_specs=[pl.BlockSpec((1, 2 * MOBA_BLOCK, 2 * MOBA_BLOCK), lambda p: (p, 0, 0)),
                   pl.BlockSpec((N_HEADS * SUBLANES, 2 * MOBA_BLOCK), lambda p: (0, 0))],
        out_shape=[jax.ShapeDtypeStruct((N_PAIRS, 2 * MOBA_BLOCK, 2 * MOBA_BLOCK), F32),
                   jax.ShapeDtypeStruct((N_HEADS * SUBLANES, 2 * MOBA_BLOCK), F32)],
        compiler_params=pltpu.CompilerParams(dimension_semantics=("arbitrary",)),
        name="bias_tiles",
    )(rel_bias)


def _proj_kernel(x_ref, g_ref, w_ref, *out_refs, prompt, tm):
    h = _rms(x_ref[0], g_ref[...])
    p = jnp.dot(h.astype(BF16), w_ref[...], preferred_element_type=F32)
    q = p[:, 0:ATT_WIDTH]
    k = p[:, ATT_WIDTH:2 * ATT_WIDTH]
    v = p[:, 2 * ATT_WIDTH:3 * ATT_WIDTH]
    ga = p[:, 3 * ATT_WIDTH:3 * ATT_WIDTH + CONV_WIDTH]
    gb = p[:, 3 * ATT_WIDTH + CONV_WIDTH:]
    glu = ga * jax.nn.sigmoid(gb)
    if prompt:
        qt_ref, kt_ref, vt_ref, kbf_ref, vtb_ref, km_ref, glu_ref = out_refs
        qt_ref[0] = q.T
        kt_ref[0] = k.T
        vt = v.T
        vt_ref[0] = vt
        row_blk = (pl.program_id(1) * (tm // MOBA_BLOCK)
                   + lax.broadcasted_iota(jnp.int32, (tm, LANES), 0) // MOBA_BLOCK)
        onehot = jnp.where(lax.broadcasted_iota(jnp.int32, (tm, LANES), 1) == row_blk, 1.0, 0.0).astype(BF16)
        kb = k.astype(BF16)
        for pr in range(N_PAIRS):
            kbf_ref[0, :, 2 * pr * LANES:(2 * pr + 1) * LANES] = kb[:, pr * LANES:(pr + 1) * LANES]
            kbf_ref[0, :, (2 * pr + 1) * LANES:(2 * pr + 2) * LANES] = onehot
        ones = jnp.ones((ONES_ROWS, MOBA_BLOCK), F32)
        for c in range(tm // MOBA_BLOCK):
            vc = vt[:, c * MOBA_BLOCK:(c + 1) * MOBA_BLOCK]
            parts = []
            for hd in range(N_HEADS):
                parts += [vc[hd * HEAD_DIM:(hd + 1) * HEAD_DIM], ones]
            vtb_ref[0, c] = jnp.concatenate(parts, axis=0).astype(BF16)
        km_ref[0] = jnp.mean(k.reshape(tm // MOBA_BLOCK, MOBA_BLOCK, ATT_WIDTH), axis=1)[:, None, :]
        glu_ref[0] = glu
    else:
        q_ref, k_ref, v_ref, glu_ref = out_refs
        q_ref[0] = q
        k_ref[0] = k
        v_ref[0] = v
        glu_ref[0] = glu


def _proj(x, g, w_bf, *, prompt, tm):
    n_seq, rows, _ = x.shape
    n_out = w_bf.shape[1]
    grid = (n_seq, rows // tm)
    row_spec = lambda width: pl.BlockSpec((1, tm, width), lambda s, i: (s, i, 0))
    col_spec = pl.BlockSpec((1, ATT_WIDTH, tm), lambda s, i: (s, 0, i))
    if prompt:
        nb = tm // MOBA_BLOCK
        out_specs = [col_spec, col_spec, col_spec, row_spec(2 * ATT_WIDTH),
                     pl.BlockSpec((1, nb, N_HEADS * V_ROWS, MOBA_BLOCK), lambda s, i: (s, i, 0, 0)),
                     pl.BlockSpec((1, nb, 1, ATT_WIDTH), lambda s, i: (s, i, 0, 0)),
                     row_spec(CONV_WIDTH)]
        t_shape = jax.ShapeDtypeStruct((n_seq, ATT_WIDTH, rows), F32)
        out_shape = [t_shape, t_shape, t_shape,
                     jax.ShapeDtypeStruct((n_seq, rows, 2 * ATT_WIDTH), BF16),
                     jax.ShapeDtypeStruct((n_seq, rows // MOBA_BLOCK, N_HEADS * V_ROWS, MOBA_BLOCK), BF16),
                     jax.ShapeDtypeStruct((n_seq, rows // MOBA_BLOCK, 1, ATT_WIDTH), F32),
                     jax.ShapeDtypeStruct((n_seq, rows, CONV_WIDTH), F32)]
    else:
        out_specs = [row_spec(ATT_WIDTH)] * 3 + [row_spec(CONV_WIDTH)]
        out_shape = [jax.ShapeDtypeStruct((n_seq, rows, ATT_WIDTH), F32)] * 3 + [
            jax.ShapeDtypeStruct((n_seq, rows, CONV_WIDTH), F32)]
    return pl.pallas_call(
        functools.partial(_proj_kernel, prompt=prompt, tm=tm),
        grid=grid,
        in_specs=[row_spec(D_MODEL),
                  pl.BlockSpec((1, D_MODEL), lambda s, i: (0, 0)),
                  pl.BlockSpec((D_MODEL, n_out), lambda s, i: (0, 0))],
        out_specs=out_specs,
        out_shape=out_shape,
        compiler_params=pltpu.CompilerParams(dimension_semantics=("arbitrary", "arbitrary"),
                                             vmem_limit_bytes=VMEM_LIMIT),
        name="proj_prompt" if prompt else "proj_sample",
    )(x, g, w_bf)


def _select_topk(gate, idx, axis, n_valid):
    big = jnp.int32(2 ** 30)
    g = jnp.where(idx < n_valid, gate, -jnp.inf)
    sel = jnp.zeros(gate.shape, jnp.bool_)
    for _ in range(MOBA_TOPK):
        mx = jnp.max(g, axis=axis, keepdims=True)
        first = jnp.min(jnp.where(g == mx, idx, big), axis=axis, keepdims=True)
        hit = idx == first
        sel = sel | (hit & (mx > -jnp.inf))
        g = jnp.where(hit, -jnp.inf, g)
    return sel


def _attn_stream(i, nblk, s, qt_ref, qtn_ref, k_ref, vt_ref, km_ref, near_ref, o_ref, qaug_ref, mnext_ref,
                 sa_ref, sb_ref, pa_ref, pb_ref):
    blk = MOBA_BLOCK

    def block_diag(qt):
        z = jnp.zeros((HEAD_DIM, blk), F32)
        return jnp.concatenate([jnp.concatenate([qt[:HEAD_DIM], z], axis=1),
                                jnp.concatenate([z, qt[HEAD_DIM:]], axis=1)], axis=0)

    def mask_rows(q2t, tile):
        gate = jnp.dot(km_ref[0], q2t, precision=lax.Precision.HIGHEST, preferred_element_type=F32)
        n_idx = lax.broadcasted_iota(jnp.int32, (nblk, 2 * blk), 0)
        sel = _select_topk(gate, n_idx, 0, tile)
        return jnp.concatenate([jnp.where(sel | (n_idx == tile), 0.0, NEG),
                                jnp.full((LANES - nblk, 2 * blk), NEG, F32)], axis=0).astype(BF16)

    def scores_into(dst_ref, j0, j1, bias):
        mx = None
        for half, j in enumerate((j0, j1)):
            kb = k_ref[0, pl.ds(pl.multiple_of(j * blk, blk), blk), :]
            s = jnp.dot(kb, qaug_ref[...], preferred_element_type=F32)
            if bias is not None:
                s = s + bias[half]
            dst_ref[half] = s
            smax = jnp.max(s, axis=0, keepdims=True)
            mx = smax if mx is None else jnp.maximum(mx, smax)
        return mx

    def probs(src_ref, dst_ref, m, mx):
        m_new = jnp.maximum(m, mx)
        dst_ref[0] = jnp.exp2(src_ref[0] - m_new).astype(BF16)
        dst_ref[1] = jnp.exp2(src_ref[1] - m_new).astype(BF16)
        return m_new, jnp.exp2(m - m_new)

    def accumulate(p_ref, j0, j1, alpha, a_a, a_b):
        p0, p1 = p_ref[0], p_ref[1]
        v0, v1 = vt_ref[0, j0], vt_ref[0, j1]
        d_a = (jnp.dot(v0[:V_ROWS], p0[:, :blk], preferred_element_type=F32)
               + jnp.dot(v1[:V_ROWS], p1[:, :blk], preferred_element_type=F32))
        d_b = (jnp.dot(v0[V_ROWS:], p0[:, blk:], preferred_element_type=F32)
               + jnp.dot(v1[V_ROWS:], p1[:, blk:], preferred_element_type=F32))
        return a_a * alpha[:, :blk] + d_a, a_b * alpha[:, blk:] + d_b

    none = nblk - 1
    prev = jnp.where(i > 0, i - 1, none)
    n_far = jnp.maximum(i - 1, 0)

    def prologue():
        qaug_ref[0:LANES, :] = (block_diag(qt_ref[0]) * (SCALE * LOG2E)).astype(BF16)

        @pl.when(i == 0)
        def _():
            row = lax.broadcasted_iota(jnp.int32, (LANES, 2 * blk), 0)
            qaug_ref[LANES:, :] = jnp.where(row == 0, 0.0, NEG).astype(BF16)

        @pl.when(i > 0)
        def _():
            qaug_ref[LANES:, :] = mnext_ref[...]

        mx = scores_into(sa_ref, i, prev, (near_ref[0, blk:, :], near_ref[0, :blk, :]))
        mnext_ref[...] = mask_rows(block_diag(qtn_ref[0]), i + 1)
        pb_ref[...] = jnp.zeros(pb_ref.shape, BF16)
        zero = jnp.zeros((V_ROWS, blk), F32)
        return (jnp.full((1, 2 * blk), NEG, F32), mx, jnp.ones((1, 2 * blk), F32), zero, zero, i, prev, i, i)

    def stage(even, t, state):
        s_src, s_dst, p_dst, p_src = (sa_ref, sb_ref, pa_ref, pb_ref) if even else (sb_ref, sa_ref, pb_ref, pa_ref)
        m, mx, alpha_p, a_a, a_b, js0, js1, jp0, jp1 = state
        j0 = 2 * t
        j1 = jnp.where(j0 + 1 < n_far, j0 + 1, none)
        mx_next = scores_into(s_dst, j0, j1, None)
        m_new, alpha = probs(s_src, p_dst, m, mx)
        a_a, a_b = accumulate(p_src, jp0, jp1, alpha_p, a_a, a_b)
        return m_new, mx_next, alpha, a_a, a_b, j0, j1, js0, js1

    def finish(even, state):
        s_src, p_dst, p_src = (sa_ref, pa_ref, pb_ref) if even else (sb_ref, pb_ref, pa_ref)
        m, mx, alpha_p, a_a, a_b, js0, js1, jp0, jp1 = state
        a_a, a_b = accumulate(p_src, jp0, jp1, alpha_p, a_a, a_b)
        _, alpha = probs(s_src, p_dst, m, mx)
        a_a, a_b = accumulate(p_dst, js0, js1, alpha, a_a, a_b)
        out_t = jnp.concatenate([a_a[:HEAD_DIM] / a_a[HEAD_DIM:HEAD_DIM + 1],
                                 a_b[:HEAD_DIM] / a_b[HEAD_DIM:HEAD_DIM + 1]], axis=0)
        o_ref[0, :, s * LANES:(s + 1) * LANES] = out_t.T

    return prologue, stage, finish


ATTN_STREAMS = 2
_STREAM_INPUTS = 6
_STREAM_SCRATCH = 6


def _attn_prompt_kernel(*refs, nblk):
    i = pl.program_id(2)
    n_in = ATTN_STREAMS * _STREAM_INPUTS
    ins, o_ref, scr = refs[:n_in], refs[n_in], refs[n_in + 1:]
    streams = [_attn_stream(i, nblk, s, *ins[s * _STREAM_INPUTS:(s + 1) * _STREAM_INPUTS], o_ref,
                            *scr[s * _STREAM_SCRATCH:(s + 1) * _STREAM_SCRATCH]) for s in range(ATTN_STREAMS)]
    n_pairs = jnp.maximum(i, 1) // 2
    states = tuple(prologue() for prologue, _, _ in streams)

    def step(t, states):
        def run(even):
            return lambda sts: tuple(stage(even, t, st) for (_, stage, _), st in zip(streams, sts))
        return lax.cond(t % 2 == 0, run(True), run(False), states)

    states = lax.fori_loop(0, n_pairs, step, states)
    for even in (True, False):
        @pl.when(n_pairs % 2 == (0 if even else 1))
        def _():
            for (_, _, finish), st in zip(streams, states):
                finish(even, st)


def _attn_prompt(qt, kaug, vtb, kmean, near_t):
    n_seq, _, t = qt.shape
    nblk = t // MOBA_BLOCK
    assert nblk % 2 == 0 and nblk < LANES and N_PAIRS % ATTN_STREAMS == 0
    ns = ATTN_STREAMS

    def stream_specs(s):
        pair = lambda g: ns * g + s
        return [pl.BlockSpec((1, LANES, MOBA_BLOCK), lambda b, g, i: (b, pair(g), i)),
                pl.BlockSpec((1, LANES, MOBA_BLOCK), lambda b, g, i: (b, pair(g), jnp.minimum(i + 1, nblk - 1))),
                pl.BlockSpec((1, t, 2 * LANES), lambda b, g, i: (b, 0, pair(g))),
                pl.BlockSpec((1, nblk, 2 * V_ROWS, MOBA_BLOCK), lambda b, g, i: (b, 0, pair(g), 0)),
                pl.BlockSpec((1, nblk, LANES), lambda b, g, i: (b, 0, pair(g))),
                pl.BlockSpec((1, 2 * MOBA_BLOCK, 2 * MOBA_BLOCK), lambda b, g, i: (pair(g), 0, 0))]

    stream_scratch = [pltpu.VMEM((2 * LANES, 2 * MOBA_BLOCK), BF16),
                      pltpu.VMEM((LANES, 2 * MOBA_BLOCK), BF16),
                      pltpu.VMEM((2, MOBA_BLOCK, 2 * MOBA_BLOCK), F32),
                      pltpu.VMEM((2, MOBA_BLOCK, 2 * MOBA_BLOCK), F32),
                      pltpu.VMEM((2, MOBA_BLOCK, 2 * MOBA_BLOCK), BF16),
                      pltpu.VMEM((2, MOBA_BLOCK, 2 * MOBA_BLOCK), BF16)]
    assert len(stream_specs(0)) == _STREAM_INPUTS and len(stream_scratch) == _STREAM_SCRATCH
    return pl.pallas_call(
        functools.partial(_attn_prompt_kernel, nblk=nblk),
        grid=(n_seq, N_PAIRS // ns, nblk),
        in_specs=[spec for s in range(ns) for spec in stream_specs(s)],
        out_specs=pl.BlockSpec((1, MOBA_BLOCK, ns * LANES), lambda b, g, i: (b, i, g)),
        out_shape=jax.ShapeDtypeStruct((n_seq, t, ATT_WIDTH), F32),
        scratch_shapes=stream_scratch * ns,
        compiler_params=pltpu.CompilerParams(dimension_semantics=("arbitrary",) * 3,
                                             vmem_limit_bytes=VMEM_LIMIT),
        name="attn_prompt",
    )(*([qt, qt, kaug, vtb, kmean, near_t] * ns))


def _attn_sample_kernel(pt_ref, tbl_ref, q_ref, kn_ref, vn_ref, near_ref, ck_hbm, cv_hbm, o_ref,
                        s_all, km_t, kbuf, vbuf, ksem, vsem, *, n_pages, nbuf, step, n_new):
    b = pl.program_id(0)
    n_seq = pl.num_programs(0)
    n_rows = N_HEADS * n_new
    page = LANES
    nblk = n_pages * page // MOBA_BLOCK

    def k_copy(seq, p, slot):
        return pltpu.make_async_copy(ck_hbm.at[pt_ref[seq, p]], kbuf.at[slot], ksem.at[slot])

    def v_copy(p, slot):
        return pltpu.make_async_copy(cv_hbm.at[pt_ref[b, p]], vbuf.at[slot], vsem.at[slot])

    @pl.when(b == 0)
    def _():
        for p in range(nbuf):
            k_copy(0, p, p).start()

    row_head = lax.broadcasted_iota(jnp.int32, (n_rows, ATT_WIDTH), 0) // n_new
    col_head = lax.broadcasted_iota(jnp.int32, (n_rows, ATT_WIDTH), 1) // HEAD_DIM
    diag = row_head == col_head
    q = q_ref[0]
    qbd_f = jnp.where(diag, jnp.concatenate([q] * N_HEADS, axis=0), 0.0)
    qbd_b = (qbd_f * SCALE).astype(BF16)
    km_t[...] = jnp.zeros(km_t.shape, F32)
    lane_k = lax.broadcasted_iota(jnp.int32, (ATT_WIDTH, LANES), 1)

    def k_step(it, _):
        p0 = it * step
        slot0 = p0 % nbuf
        for r in range(step):
            k_copy(b, p0 + r, slot0 + r).wait()
        km = km_t[...]
        for r in range(0, step, 2):
            kt0 = kbuf[slot0 + r]
            kt1 = kbuf[slot0 + r + 1]
            s_all[p0 + r] = jnp.dot(qbd_b, kt0.astype(BF16), preferred_element_type=F32)
            s_all[p0 + r + 1] = jnp.dot(qbd_b, kt1.astype(BF16), preferred_element_type=F32)
            mean = jnp.sum(kt0 + kt1, axis=1, keepdims=True) * (1.0 / MOBA_BLOCK)
            km = jnp.where(lane_k == (p0 + r) // 2, mean, km)
        km_t[...] = km

        @pl.when(p0 + nbuf < n_pages)
        def _():
            for r in range(step):
                k_copy(b, p0 + nbuf + r, slot0 + r).start()

        return 0

    lax.fori_loop(0, n_pages // step, k_step, 0)

    @pl.when(b + 1 < n_seq)
    def _():
        for p in range(nbuf):
            k_copy(b + 1, p, p).start()

    for p in range(nbuf):
        v_copy(p, p).start()

    gate = jnp.dot(qbd_f, km_t[...], precision=lax.Precision.HIGHEST,
                   preferred_element_type=F32)
    lane = lax.broadcasted_iota(jnp.int32, (n_rows, LANES), 1)
    sel = _select_topk(gate, lane, 1, nblk)
    rh = lax.broadcasted_iota(jnp.int32, (n_rows, 1), 0) // n_new
    far = jnp.zeros((n_rows, 1), F32)
    for h in range(N_HEADS):
        far = jnp.where(rh == h, tbl_ref[N_BUCKETS - 1, h], far)
    mask = jnp.where(sel, far, NEG)
    near = near_ref[...]
    for n in range(nblk):
        if n == nblk - 1:
            seln = jnp.broadcast_to(mask[:, n:n + 1], (n_rows, page)) > 0.5 * NEG
            s_all[2 * n] = s_all[2 * n] + jnp.where(seln, near[:, 0:page], NEG)
            s_all[2 * n + 1] = s_all[2 * n + 1] + jnp.where(seln, near[:, page:2 * page], NEG)
        else:
            add = jnp.broadcast_to(mask[:, n:n + 1], (n_rows, page))
            s_all[2 * n] = s_all[2 * n] + add
            s_all[2 * n + 1] = s_all[2 * n + 1] + add
    kn = jnp.concatenate([kn_ref[0], jnp.zeros((page - n_new, ATT_WIDTH), F32)], axis=0).astype(BF16)
    s_own = lax.dot_general(qbd_b, kn, (((1,), (1,)), ((), ())),
                            preferred_element_type=F32) + near[:, 2 * page:3 * page]
    mrun = lax.fori_loop(0, n_pages, lambda j, mm: jnp.maximum(mm, s_all[j]), s_own)
    m = jnp.max(mrun, axis=1, keepdims=True)
    p_own = jnp.exp(s_own - m)

    def exp_step(j, lsum):
        pj = jnp.exp(s_all[j] - m)
        s_all[j] = pj
        return lsum + pj

    lsum = lax.fori_loop(0, n_pages, exp_step, p_own)
    l = jnp.sum(lsum, axis=1, keepdims=True)
    vn = jnp.concatenate([vn_ref[0], jnp.zeros((page - n_new, ATT_WIDTH), F32)], axis=0).astype(BF16)
    acc = jnp.dot(p_own.astype(BF16), vn, preferred_element_type=F32)

    def v_step(it, acc):
        p0 = it * step
        slot0 = p0 % nbuf
        for r in range(step):
            v_copy(p0 + r, slot0 + r).wait()
        parts = []
        for r in range(step):
            pb = s_all[p0 + r].astype(BF16)
            vt = vbuf[slot0 + r].astype(BF16)
            parts.append(lax.dot_general(pb, vt, (((1,), (1,)), ((), ())), preferred_element_type=F32))
        while len(parts) > 1:
            parts = [x + y for x, y in zip(parts[0::2], parts[1::2])]

        @pl.when(p0 + nbuf < n_pages)
        def _():
            for r in range(step):
                v_copy(p0 + nbuf + r, slot0 + r).start()

        return acc + parts[0]

    acc = lax.fori_loop(0, n_pages // step, v_step, acc)
    o = jnp.where(diag, acc / l, 0.0)
    o_ref[0] = jnp.sum(o.reshape(N_HEADS, n_new, ATT_WIDTH), axis=0)


def _attn_sample(page_table, rel_bias, q, k_new, v_new, near_s, cache_kt, cache_vt, *, nbuf):
    n_seq, n_new, _ = q.shape
    n_pages = page_table.shape[1]
    n_rows = N_HEADS * n_new
    page = cache_kt.shape[2]
    step = min(8, nbuf)
    assert step & (step - 1) == 0 and step >= 2 and nbuf % step == 0 and n_pages % nbuf == 0
    new_spec = pl.BlockSpec((1, n_new, ATT_WIDTH), lambda b, pt: (b, 0, 0))
    in_specs = [pl.BlockSpec(memory_space=pltpu.SMEM), new_spec, new_spec, new_spec,
                pl.BlockSpec((n_rows, 2 * MOBA_BLOCK), lambda b, pt: (0, 0)),
                pl.BlockSpec(memory_space=pl.ANY), pl.BlockSpec(memory_space=pl.ANY)]
    return pl.pallas_call(
        functools.partial(_attn_sample_kernel, n_pages=n_pages, nbuf=nbuf, step=step, n_new=n_new),
        grid_spec=pltpu.PrefetchScalarGridSpec(
            num_scalar_prefetch=1,
            grid=(n_seq,),
            in_specs=in_specs,
            out_specs=new_spec,
            scratch_shapes=[pltpu.VMEM((n_pages, n_rows, page), F32),
                            pltpu.VMEM((ATT_WIDTH, LANES), F32),
                            pltpu.VMEM((nbuf, ATT_WIDTH, page), F32),
                            pltpu.VMEM((nbuf, ATT_WIDTH, page), F32),
                            pltpu.SemaphoreType.DMA((nbuf,)),
                            pltpu.SemaphoreType.DMA((nbuf,))]),
        out_shape=jax.ShapeDtypeStruct((n_seq, n_new, ATT_WIDTH), F32),
        compiler_params=pltpu.CompilerParams(dimension_semantics=("arbitrary",),
                                             vmem_limit_bytes=VMEM_LIMIT),
        name="attn_sample",
    )(page_table, rel_bias, q, k_new, v_new, near_s, cache_kt, cache_vt)


def _conv_offsets(halo, stride):
    off0 = halo - (CONV_KERNEL - 1) * stride
    return [off0 + j * stride for j in range(CONV_KERNEL)]


def _shift_classes(offs):
    return sorted({o % SUBLANES for o in offs} - {0})


def _mix_kernel(att_ref, glu_ref, x_ref, st_ref, cw_ref, cb_ref, lg_ref, lb_ref, wo_ref, gpm_ref, gpf_ref,
                x1_ref, hn_ref, gbuf, cvbuf, *shifted, tm, stride, halo, chunk):
    i = pl.program_id(1)
    group = min(tm, 128)

    @pl.when(i == 0)
    def _():
        gbuf[0:halo, :] = st_ref[0]

    gbuf[halo:halo + tm, :] = glu_ref[0]
    offs = _conv_offsets(halo, stride)
    for r, ref in zip(_shift_classes(offs), shifted):
        ref[...] = gbuf[r:r + ref.shape[0], :]
    src = dict(zip(_shift_classes(offs), shifted))
    src[0] = gbuf
    for c in range(tm // chunk):
        r0 = c * chunk
        acc = jnp.broadcast_to(cb_ref[...], (chunk, CONV_WIDTH))
        for j, o in enumerate(offs):
            a = o - o % SUBLANES + r0
            acc = acc + cw_ref[j:j + 1, :] * src[o % SUBLANES][a:a + chunk, :]
        mu = jnp.mean(acc, axis=-1, keepdims=True)
        d = acc - mu
        var = jnp.mean(d * d, axis=-1, keepdims=True)
        y = d * lax.rsqrt(var + EPS) * lg_ref[...] + lb_ref[...]
        cvbuf[r0:r0 + chunk, :] = (y * jax.nn.sigmoid(y)).astype(BF16)

        g1 = r0 + chunk
        if g1 % group == 0 or g1 == tm:
            g0 = (g1 - 1) // group * group
            mix = (jnp.dot(att_ref[0, g0:g1, :].astype(BF16), wo_ref[0:ATT_WIDTH, :], preferred_element_type=F32)
                   + jnp.dot(cvbuf[g0:g1, :], wo_ref[ATT_WIDTH:, :], preferred_element_type=F32))
            x1 = x_ref[0, g0:g1, :] + _rms(mix, gpm_ref[...])
            x1_ref[0, g0:g1, :] = x1
            hn_ref[0, g0:g1, :] = _rms(x1, gpf_ref[...]).astype(BF16)

    if tm >= halo:
        gbuf[0:halo, :] = gbuf[tm:tm + halo, :]


def _mix(att, glu, x, state, cw, cb, lg, lb, wo_bf, gpm, gpf, *, tm, stride):
    n_seq, rows, _ = x.shape
    halo = state.shape[1]
    assert rows == tm or tm >= halo
    offs = _conv_offsets(halo, stride)
    row_spec = lambda width: pl.BlockSpec((1, tm, width), lambda s, i: (s, i, 0))
    const = lambda shape: pl.BlockSpec(shape, lambda s, i: (0,) * len(shape))
    return pl.pallas_call(
        functools.partial(_mix_kernel, tm=tm, stride=stride, halo=halo, chunk=min(tm, 32)),
        grid=(n_seq, rows // tm),
        in_specs=[row_spec(ATT_WIDTH), row_spec(CONV_WIDTH), row_spec(D_MODEL),
                  pl.BlockSpec((1, halo, CONV_WIDTH), lambda s, i: (s, 0, 0)),
                  const((CONV_KERNEL, CONV_WIDTH)), const((1, CONV_WIDTH)), const((1, CONV_WIDTH)),
                  const((1, CONV_WIDTH)), const((D_MODEL, D_MODEL)), const((1, D_MODEL)), const((1, D_MODEL))],
        out_specs=[row_spec(D_MODEL), row_spec(D_MODEL)],
        out_shape=[jax.ShapeDtypeStruct((n_seq, rows, D_MODEL), F32),
                   jax.ShapeDtypeStruct((n_seq, rows, D_MODEL), BF16)],
        scratch_shapes=[pltpu.VMEM((halo + tm, CONV_WIDTH), F32), pltpu.VMEM((tm, CONV_WIDTH), BF16)] + [
            pltpu.VMEM((max(o - r for o in offs if o % SUBLANES == r) + tm, CONV_WIDTH), F32)
            for r in _shift_classes(offs)],
        compiler_params=pltpu.CompilerParams(dimension_semantics=("arbitrary", "arbitrary"),
                                             vmem_limit_bytes=VMEM_LIMIT),
        name="mix_s%d" % stride,
    )(att, glu, x, state, cw, cb, lg, lb, wo_bf, gpm, gpf)


def _ffn_kernel(hn_ref, x1_ref, wg_ref, wv_ref, wd_ref, dwg_ref, dwv_ref, dbg_ref, dbv_ref, stg_ref, stv_ref,
                gpo_ref, y_ref, tg_ref, tv_ref, ubuf_g, ubuf_v, car_g, car_v, acc_ref, *, tm, stride, halo, nc):
    i = pl.program_id(1)
    c = pl.program_id(2)
    hn = hn_ref[0]

    @pl.when(i == 0)
    def _():
        ubuf_g[0:halo, :] = stg_ref[0]
        ubuf_v[0:halo, :] = stv_ref[0]

    @pl.when(i > 0)
    def _():
        ubuf_g[0:halo, :] = car_g[c]
        ubuf_v[0:halo, :] = car_v[c]

    @pl.when(c == 0)
    def _():
        acc_ref[...] = jnp.zeros(acc_ref.shape, F32)

    def up(c0, c1):
        ubuf_g[halo:halo + tm, c0:c1] = jnp.dot(hn, wg_ref[:, c0:c1], preferred_element_type=F32)
        ubuf_v[halo:halo + tm, c0:c1] = jnp.dot(hn, wv_ref[:, c0:c1], preferred_element_type=F32)

    def conv(c0, c1, dw_ref, db_ref, ubuf):
        return (dw_ref[0:1, c0:c1] * ubuf[halo - 2 * stride:halo - 2 * stride + tm, c0:c1]
                + dw_ref[1:2, c0:c1] * ubuf[halo - stride:halo - stride + tm, c0:c1]
                + dw_ref[2:3, c0:c1] * ubuf[halo:halo + tm, c0:c1] + db_ref[:, c0:c1])

    ck = wg_ref.shape[1]
    bounds = [(c0, min(c0 + MXU_COLS, ck)) for c0 in range(0, ck, MXU_COLS)]
    def down(c0, c1, a):
        acc_ref[...] += jnp.dot(a, wd_ref[c0:c1, :], preferred_element_type=F32)

    up(*bounds[0])
    pending = None
    for n, (c0, c1) in enumerate(bounds):
        if n + 1 < len(bounds):
            up(*bounds[n + 1])
        if pending is not None:
            down(*pending)
        a = jax.nn.gelu(conv(c0, c1, dwg_ref, dbg_ref, ubuf_g), approximate=True) * conv(c0, c1, dwv_ref, dbv_ref,
                                                                                        ubuf_v)
        pending = (c0, c1, a.astype(BF16))
    down(*pending)

    for ubuf, car, tail_ref in ((ubuf_g, car_g, tg_ref), (ubuf_v, car_v, tv_ref)):
        last = ubuf[tm:tm + halo, :]
        car[c] = last
        tail_ref[0, 0] = last

    @pl.when(c == nc - 1)
    def _():
        y_ref[0] = x1_ref[0] + _rms(acc_ref[...], gpo_ref[...])


def _ffn(hn, x1, wup_bf, wdn_bf, dw, db, state, gpo, *, tm, stride, nc):
    n_seq, rows, _ = x1.shape
    halo = state.shape[1]
    ck = D_FF // nc
    row_spec = lambda width: pl.BlockSpec((1, tm, width), lambda s, i, c: (s, i, 0))
    mode = dict(pipeline_mode=pl.Buffered(1)) if nc == 1 else {}
    gate_cols = lambda shape, **kw: pl.BlockSpec(shape, lambda s, i, c: (0, c), **kw)
    val_cols = lambda shape, **kw: pl.BlockSpec(shape, lambda s, i, c: (0, c + nc), **kw)
    return pl.pallas_call(
        functools.partial(_ffn_kernel, tm=tm, stride=stride, halo=halo, nc=nc),
        grid=(n_seq, rows // tm, nc),
        in_specs=[row_spec(D_MODEL), row_spec(D_MODEL),
                  gate_cols((D_MODEL, ck), **mode), val_cols((D_MODEL, ck), **mode),
                  pl.BlockSpec((ck, D_MODEL), lambda s, i, c: (c, 0), **mode),
                  gate_cols((FFN_KERNEL, ck)), val_cols((FFN_KERNEL, ck)),
                  gate_cols((1, ck)), val_cols((1, ck)),
                  pl.BlockSpec((1, halo, ck), lambda s, i, c: (s, 0, c)),
                  pl.BlockSpec((1, halo, ck), lambda s, i, c: (s, 0, c + nc)),
                  pl.BlockSpec((1, D_MODEL), lambda s, i, c: (0, 0))],
        out_specs=[row_spec(D_MODEL),
                   pl.BlockSpec((1, 1, halo, ck), lambda s, i, c: (s, i, 0, c)),
                   pl.BlockSpec((1, 1, halo, ck), lambda s, i, c: (s, i, 0, c))],
        out_shape=[jax.ShapeDtypeStruct((n_seq, rows, D_MODEL), F32),
                   jax.ShapeDtypeStruct((n_seq, rows // tm, halo, D_FF), F32),
                   jax.ShapeDtypeStruct((n_seq, rows // tm, halo, D_FF), F32)],
        scratch_shapes=[pltpu.VMEM((halo + tm, ck), F32), pltpu.VMEM((halo + tm, ck), F32),
                        pltpu.VMEM((nc, halo, ck), F32), pltpu.VMEM((nc, halo, ck), F32),
                        pltpu.VMEM((tm, D_MODEL), F32)],
        compiler_params=pltpu.CompilerParams(dimension_semantics=("arbitrary",) * 3,
                                             vmem_limit_bytes=VMEM_LIMIT),
        name="ffn_s%d" % stride,
    )(hn, x1, wup_bf, wup_bf, wdn_bf, dw, dw, db, db, state, state, gpo)


def _tile(rows, want):
    return want if rows % want == 0 else rows


def kernel(x_prompt, x_sample, cache_k, cache_v, state_conv, state_ffn, page_table, rel_bias, g_pre_mix, w_in,
           conv_dw_w, conv_dw_b, conv_ln_g, conv_ln_b, w_out, g_post_mix, g_pre_ffn, w_ffn_up, ffn_dw_w,
           ffn_dw_b, w_ffn_down, g_post_ffn):
    depth = w_in.shape[0]
    assert depth == 1, "single-layer trunk"
    bp, seq, _ = x_prompt.shape
    bs, n_new, _ = x_sample.shape
    n_pool, page = cache_k.shape[1], cache_k.shape[2]
    n_pages = page_table.shape[1]
    assert seq % MOBA_BLOCK == 0 and page == LANES and (n_pages * page) % MOBA_BLOCK == 0
    assert n_new == SUBLANES and n_pages * page // MOBA_BLOCK <= LANES

    w_in_bf = w_in[0].astype(BF16)
    w_out_bf = w_out[0].astype(BF16)
    w_up_bf = w_ffn_up[0].astype(BF16)
    w_dn_bf = w_ffn_down[0].astype(BF16)
    cw, cb = conv_dw_w[0], conv_dw_b
    near_t, near_s = _bias_tiles(rel_bias)
    conv_halo_p = 32
    ffn_halo_p = SUBLANES

    tm_p = _tile(seq, 512)
    qt, kt, vt, kbf, vtb, kmean, glu_p = _proj(x_prompt, g_pre_mix, w_in_bf, prompt=True, tm=tm_p)
    att_p = _attn_prompt(qt, kbf, vtb, kmean.reshape(bp, seq // MOBA_BLOCK, ATT_WIDTH), near_t)
    x1_p, hn_p = _mix(att_p, glu_p, x_prompt, jnp.zeros((bp, conv_halo_p, CONV_WIDTH), F32), cw, cb,
                      conv_ln_g, conv_ln_b, w_out_bf, g_post_mix, g_pre_ffn, tm=tm_p, stride=1)
    y_prompt, tail_g, tail_v = _ffn(hn_p, x1_p, w_up_bf, w_dn_bf, ffn_dw_w[0], ffn_dw_b,
                                    jnp.zeros((bp, ffn_halo_p, 2 * D_FF), F32), g_post_ffn,
                                    tm=tm_p, stride=1, nc=1)
    k_prompt = kt.reshape(1, bp, N_HEADS, HEAD_DIM, seq).transpose(0, 1, 4, 2, 3)
    v_prompt = vt.reshape(1, bp, N_HEADS, HEAD_DIM, seq).transpose(0, 1, 4, 2, 3)
    conv_prompt = glu_p[:, seq - (CONV_KERNEL - 1):, :][None]
    ffn_prompt = jnp.concatenate([tail_g[:, -1], tail_v[:, -1]],
                                 axis=-1)[:, ffn_halo_p - (FFN_KERNEL - 1):, :][None]

    rows_s = n_new * bs
    to_tb = lambda a: a.transpose(1, 0, 2).reshape(1, rows_s, a.shape[-1])
    to_bt = lambda a: a.reshape(n_new, bs, a.shape[-1]).transpose(1, 0, 2)
    xs = to_tb(x_sample)
    q_s, k_s, v_s, glu_s = _proj(xs, g_pre_mix, w_in_bf, prompt=False, tm=rows_s)
    q_b, k_b, v_b = to_bt(q_s), to_bt(k_s), to_bt(v_s)
    cache_kt = cache_k[0].transpose(0, 2, 3, 1).reshape(n_pool, ATT_WIDTH, page)
    cache_vt = cache_v[0].transpose(0, 2, 3, 1).reshape(n_pool, ATT_WIDTH, page)
    att_b = _attn_sample(page_table, rel_bias, q_b, k_b, v_b, near_s, cache_kt, cache_vt,
                         nbuf=min(32, n_pages))
    conv_state = state_conv[0].transpose(1, 0, 2).reshape(1, (CONV_KERNEL - 1) * bs, CONV_WIDTH)
    x1_s, hn_s = _mix(to_tb(att_b), glu_s, xs, conv_state, cw, cb, conv_ln_g, conv_ln_b, w_out_bf,
                      g_post_mix, g_pre_ffn, tm=rows_s, stride=bs)
    ffn_state = state_ffn[0].transpose(1, 0, 2).reshape(1, (FFN_KERNEL - 1) * bs, 2 * D_FF)
    y_s, tail_gs, tail_vs = _ffn(hn_s, x1_s, w_up_bf, w_dn_bf, ffn_dw_w[0], ffn_dw_b, ffn_state, g_post_ffn,
                                 tm=rows_s, stride=bs, nc=2)
    y_sample = to_bt(y_s)
    k_sample = k_b.reshape(1, bs, n_new, N_HEADS, HEAD_DIM)
    v_sample = v_b.reshape(1, bs, n_new, N_HEADS, HEAD_DIM)
    conv_all = jnp.concatenate([conv_state[0], glu_s[0]], axis=0)[n_new * bs:]
    conv_sample = conv_all.reshape(CONV_KERNEL - 1, bs, CONV_WIDTH).transpose(1, 0, 2)[None]
    ffn_sample = jnp.concatenate([tail_gs[:, -1], tail_vs[:, -1]], axis=-1).reshape(
        FFN_KERNEL - 1, bs, 2 * D_FF).transpose(1, 0, 2)[None]

    return (y_prompt, y_sample, k_prompt, v_prompt, conv_prompt, ffn_prompt,
            k_sample, v_sample, conv_sample, ffn_sample)
```

```python
import functools
import math

import numpy as np
import jax
import jax.numpy as jnp
from jax import lax
from jax.experimental import pallas as pl
from jax.experimental.pallas import tpu as pltpu

F32 = jnp.float32
BF16 = jnp.bfloat16

D_MODEL = 1024
HEAD_DIM = 64
ATT_WIDTH = 512
N_HEADS = 8
N_PAIRS = N_HEADS // 2
CONV_WIDTH = 512
CONV_KERNEL = 31
MOBA_BLOCK = 256
MOBA_TOPK = 3
N_BUCKETS = 32
MAX_DISTANCE = 128
D_FF = 2816
FFN_KERNEL = 3
EPS = 1e-6
SCALE = HEAD_DIM ** -0.5
LOG2E = math.log2(math.e)
NEG = -1e30
LANES = 128
SUBLANES = 8
VMEM_LIMIT = 56 * 1024 * 1024
MXU_COLS = 256
ONES_ROWS = 16
V_ROWS = HEAD_DIM + ONES_ROWS


def _bucket_thresholds():
    n = np.arange(0, 4 * MAX_DISTANCE)
    max_exact = N_BUCKETS // 2
    nf = np.maximum(n, 1).astype(np.float64)
    large = max_exact + (np.log(nf / max_exact) / math.log(MAX_DISTANCE / max_exact)
                         * (N_BUCKETS - max_exact)).astype(np.int64)
    bucket = np.where(n < max_exact, n, np.minimum(large, N_BUCKETS - 1))
    return tuple(int(np.argmax(bucket >= k)) for k in range(1, N_BUCKETS))


_BUCKET_THR = _bucket_thresholds()


def _rms(x, g):
    return x * lax.rsqrt(jnp.mean(x * x, axis=-1, keepdims=True) + EPS) * g


def _bias_kernel(tbl_ref, near_t_ref, near_s_ref):
    p = pl.program_id(0)

    def bias_of(dist, h):
        b = jnp.full(dist.shape, tbl_ref[0, h], F32)
        for k in range(1, N_BUCKETS):
            b = jnp.where(dist >= _BUCKET_THR[k - 1], tbl_ref[k, h], b)
        return jnp.where(dist >= 0, b, NEG)

    jj = lax.broadcasted_iota(jnp.int32, (2 * MOBA_BLOCK, MOBA_BLOCK), 0)
    ii = lax.broadcasted_iota(jnp.int32, (2 * MOBA_BLOCK, MOBA_BLOCK), 1)
    dist = MOBA_BLOCK + ii - jj
    for half in range(2):
        h = 2 * p + half
        near_t_ref[0, :, half * MOBA_BLOCK:(half + 1) * MOBA_BLOCK] = (
            bias_of(dist, h) - tbl_ref[N_BUCKETS - 1, h]) * LOG2E

    @pl.when(p == 0)
    def _():
        tt = lax.broadcasted_iota(jnp.int32, (SUBLANES, 2 * MOBA_BLOCK), 0)
        j2 = lax.broadcasted_iota(jnp.int32, (SUBLANES, 2 * MOBA_BLOCK), 1)
        d2 = MOBA_BLOCK + tt - j2
        for h in range(N_HEADS):
            near_s_ref[h * SUBLANES:(h + 1) * SUBLANES, :] = bias_of(d2, h)


def _bias_tiles(rel_bias):
    return pl.pallas_call(
        _bias_kernel,
        grid=(N_PAIRS,),
        in_specs=[pl.BlockSpec(memory_space=pltpu.SMEM)],
        out_specs=[pl.BlockSpec((1, 2 * MOBA_BLOCK, 2 * MOBA_BLOCK), lambda p: (p, 0, 0)),
                   pl.BlockSpec((N_HEADS * SUBLANES, 2 * MOBA_BLOCK), lambda p: (0, 0))],
        out_shape=[jax.ShapeDtypeStruct((N_PAIRS, 2 * MOBA_BLOCK, 2 * MOBA_BLOCK), F32),
                   jax.ShapeDtypeStruct((N_HEADS * SUBLANES, 2 * MOBA_BLOCK), F32)],
        compiler_params=pltpu.CompilerParams(dimension_semantics=("arbitrary",)),
        name="bias_tiles",
    )(rel_bias)


def _proj_kernel(x_ref, g_ref, w_ref, *out_refs, prompt, tm):
    h = _rms(x_ref[0], g_ref[...])
    p = jnp.dot(h.astype(BF16), w_ref[...], preferred_element_type=F32)
    q = p[:, 0:ATT_WIDTH]
    k = p[:, ATT_WIDTH:2 * ATT_WIDTH]
    v = p[:, 2 * ATT_WIDTH:3 * ATT_WIDTH]
    ga = p[:, 3 * ATT_WIDTH:3 * ATT_WIDTH + CONV_WIDTH]
    gb = p[:, 3 * ATT_WIDTH + CONV_WIDTH:]
    glu = ga * jax.nn.sigmoid(gb)
    if prompt:
        qt_ref, kt_ref, vt_ref, kbf_ref, vtb_ref, km_ref, glu_ref = out_refs
        qt_ref[0] = q.T
        kt_ref[0] = k.T
        vt = v.T
        vt_ref[0] = vt
        row_blk = (pl.program_id(1) * (tm // MOBA_BLOCK)
                   + lax.broadcasted_iota(jnp.int32, (tm, LANES), 0) // MOBA_BLOCK)
        onehot = jnp.where(lax.broadcasted_iota(jnp.int32, (tm, LANES), 1) == row_blk, 1.0, 0.0).astype(BF16)
        kb = k.astype(BF16)
        for pr in range(N_PAIRS):
            kbf_ref[0, :, 2 * pr * LANES:(2 * pr + 1) * LANES] = kb[:, pr * LANES:(pr + 1) * LANES]
            kbf_ref[0, :, (2 * pr + 1) * LANES:(2 * pr + 2) * LANES] = onehot
        ones = jnp.ones((ONES_ROWS, MOBA_BLOCK), F32)
        for c in range(tm // MOBA_BLOCK):
            vc = vt[:, c * MOBA_BLOCK:(c + 1) * MOBA_BLOCK]
            parts = []
            for hd in range(N_HEADS):
                parts += [vc[hd * HEAD_DIM:(hd + 1) * HEAD_DIM], ones]
            vtb_ref[0, c] = jnp.concatenate(parts, axis=0).astype(BF16)
        km_ref[0] = jnp.mean(k.reshape(tm // MOBA_BLOCK, MOBA_BLOCK, ATT_WIDTH), axis=1)[:, None, :]
        glu_ref[0] = glu
    else:
        q_ref, k_ref, v_ref, glu_ref = out_refs
        q_ref[0] = q
        k_ref[0] = k
        v_ref[0] = v
        glu_ref[0] = glu


def _proj(x, g, w_bf, *, prompt, tm):
    n_seq, rows, _ = x.shape
    n_out = w_bf.shape[1]
    grid = (n_seq, rows // tm)
    row_spec = lambda width: pl.BlockSpec((1, tm, width), lambda s, i: (s, i, 0))
    col_spec = pl.BlockSpec((1, ATT_WIDTH, tm), lambda s, i: (s, 0, i))
    if prompt:
        nb = tm // MOBA_BLOCK
        out_specs = [col_spec, col_spec, col_spec, row_spec(2 * ATT_WIDTH),
                     pl.BlockSpec((1, nb, N_HEADS * V_ROWS, MOBA_BLOCK), lambda s, i: (s, i, 0, 0)),
                     pl.BlockSpec((1, nb, 1, ATT_WIDTH), lambda s, i: (s, i, 0, 0)),
                     row_spec(CONV_WIDTH)]
        t_shape = jax.ShapeDtypeStruct((n_seq, ATT_WIDTH, rows), F32)
        out_shape = [t_shape, t_shape, t_shape,
                     jax.ShapeDtypeStruct((n_seq, rows, 2 * ATT_WIDTH), BF16),
                     jax.ShapeDtypeStruct((n_seq, rows // MOBA_BLOCK, N_HEADS * V_ROWS, MOBA_BLOCK), BF16),
                     jax.ShapeDtypeStruct((n_seq, rows // MOBA_BLOCK, 1, ATT_WIDTH), F32),
                     jax.ShapeDtypeStruct((n_seq, rows, CONV_WIDTH), F32)]
    else:
        out_specs = [row_spec(ATT_WIDTH)] * 3 + [row_spec(CONV_WIDTH)]
        out_shape = [jax.ShapeDtypeStruct((n_seq, rows, ATT_WIDTH), F32)] * 3 + [
            jax.ShapeDtypeStruct((n_seq, rows, CONV_WIDTH), F32)]
    return pl.pallas_call(
        functools.partial(_proj_kernel, prompt=prompt, tm=tm),
        grid=grid,
        in_specs=[row_spec(D_MODEL),
                  pl.BlockSpec((1, D_MODEL), lambda s, i: (0, 0)),
                  pl.BlockSpec((D_MODEL, n_out), lambda s, i: (0, 0))],
        out_specs=out_specs,
        out_shape=out_shape,
        compiler_params=pltpu.CompilerParams(dimension_semantics=("arbitrary", "arbitrary"),
                                             vmem_limit_bytes=VMEM_LIMIT),
        name="proj_prompt" if prompt else "proj_sample",
    )(x, g, w_bf)


def _select_topk(gate, idx, axis, n_valid):
    big = jnp.int32(2 ** 30)
    g = jnp.where(idx < n_valid, gate, -jnp.inf)
    sel = jnp.zeros(gate.shape, jnp.bool_)
    for _ in range(MOBA_TOPK):
        mx = jnp.max(g, axis=axis, keepdims=True)
        first = jnp.min(jnp.where(g == mx, idx, big), axis=axis, keepdims=True)
        hit = idx == first
        sel = sel | (hit & (mx > -jnp.inf))
        g = jnp.where(hit, -jnp.inf, g)
    return sel


def _attn_stream(i, nblk, s, qt_ref, qtn_ref, k_ref, vt_ref, km_ref, near_ref, o_ref, qaug_ref, mnext_ref,
                 sa_ref, sb_ref, pa_ref, pb_ref):
    blk = MOBA_BLOCK

    def block_diag(qt):
        z = jnp.zeros((HEAD_DIM, blk), F32)
        return jnp.concatenate([jnp.concatenate([qt[:HEAD_DIM], z], axis=1),
                                jnp.concatenate([z, qt[HEAD_DIM:]], axis=1)], axis=0)

    def mask_rows(q2t, tile):
        gate = jnp.dot(km_ref[0], q2t, precision=lax.Precision.HIGHEST, preferred_element_type=F32)
        n_idx = lax.broadcasted_iota(jnp.int32, (nblk, 2 * blk), 0)
        sel = _select_topk(gate, n_idx, 0, tile)
        return jnp.concatenate([jnp.where(sel | (n_idx == tile), 0.0, NEG),
                                jnp.full((LANES - nblk, 2 * blk), NEG, F32)], axis=0).astype(BF16)

    def scores_into(dst_ref, j0, j1, bias):
        mx = None
        for half, j in enumerate((j0, j1)):
            kb = k_ref[0, pl.ds(pl.multiple_of(j * blk, blk), blk), :]
            s = jnp.dot(kb, qaug_ref[...], preferred_element_type=F32)
            if bias is not None:
                s = s + bias[half]
            dst_ref[half] = s
            smax = jnp.max(s, axis=0, keepdims=True)
            mx = smax if mx is None else jnp.maximum(mx, smax)
        return mx

    def probs(src_ref, dst_ref, m, mx):
        m_new = jnp.maximum(m, mx)
        dst_ref[0] = jnp.exp2(src_ref[0] - m_new).astype(BF16)
        dst_ref[1] = jnp.exp2(src_ref[1] - m_new).astype(BF16)
        return m_new, jnp.exp2(m - m_new)

    def accumulate(p_ref, j0, j1, alpha, a_a, a_b):
        p0, p1 = p_ref[0], p_ref[1]
        v0, v1 = vt_ref[0, j0], vt_ref[0, j1]
        d_a = (jnp.dot(v0[:V_ROWS], p0[:, :blk], preferred_element_type=F32)
               + jnp.dot(v1[:V_ROWS], p1[:, :blk], preferred_element_type=F32))
        d_b = (jnp.dot(v0[V_ROWS:], p0[:, blk:], preferred_element_type=F32)
               + jnp.dot(v1[V_ROWS:], p1[:, blk:], preferred_element_type=F32))
        return a_a * alpha[:, :blk] + d_a, a_b * alpha[:, blk:] + d_b

    none = nblk - 1
    prev = jnp.where(i > 0, i - 1, none)
    n_far = jnp.maximum(i - 1, 0)

    def prologue():
        qaug_ref[0:LANES, :] = (block_diag(qt_ref[0]) * (SCALE * LOG2E)).astype(BF16)

        @pl.when(i == 0)
        def _():
            row = lax.broadcasted_iota(jnp.int32, (LANES, 2 * blk), 0)
            qaug_ref[LANES:, :] = jnp.where(row == 0, 0.0, NEG).astype(BF16)

        @pl.when(i > 0)
        def _():
            qaug_ref[LANES:, :] = mnext_ref[...]

        mx = scores_into(sa_ref, i, prev, (near_ref[0, blk:, :], near_ref[0, :blk, :]))
        mnext_ref[...] = mask_rows(block_diag(qtn_ref[0]), i + 1)
        pb_ref[...] = jnp.zeros(pb_ref.shape, BF16)
        zero = jnp.zeros((V_ROWS, blk), F32)
        return (jnp.full((1, 2 * blk), NEG, F32), mx, jnp.ones((1, 2 * blk), F32), zero, zero, i, prev, i, i)

    def stage(even, t, state):
        s_src, s_dst, p_dst, p_src = (sa_ref, sb_ref, pa_ref, pb_ref) if even else (sb_ref, sa_ref, pb_ref, pa_ref)
        m, mx, alpha_p, a_a, a_b, js0, js1, jp0, jp1 = state
        j0 = 2 * t
        j1 = jnp.where(j0 + 1 < n_far, j0 + 1, none)
        mx_next = scores_into(s_dst, j0, j1, None)
        m_new, alpha = probs(s_src, p_dst, m, mx)
        a_a, a_b = accumulate(p_src, jp0, jp1, alpha_p, a_a, a_b)
        return m_new, mx_next, alpha, a_a, a_b, j0, j1, js0, js1

    def finish(even, state):
        s_src, p_dst, p_src = (sa_ref, pa_ref, pb_ref) if even else (sb_ref, pb_ref, pa_ref)
        m, mx, alpha_p, a_a, a_b, js0, js1, jp0, jp1 = state
        a_a, a_b = accumulate(p_src, jp0, jp1, alpha_p, a_a, a_b)
        _, alpha = probs(s_src, p_dst, m, mx)
        a_a, a_b = accumulate(p_dst, js0, js1, alpha, a_a, a_b)
        out_t = jnp.concatenate([a_a[:HEAD_DIM] / a_a[HEAD_DIM:HEAD_DIM + 1],
                                 a_b[:HEAD_DIM] / a_b[HEAD_DIM:HEAD_DIM + 1]], axis=0)
        o_ref[0, :, s * LANES:(s + 1) * LANES] = out_t.T

    return prologue, stage, finish


ATTN_STREAMS = 4
_STREAM_INPUTS = 6
_STREAM_SCRATCH = 6


def _attn_prompt_kernel(*refs, nblk):
    i = pl.program_id(2)
    n_in = ATTN_STREAMS * _STREAM_INPUTS
    ins, o_ref, scr = refs[:n_in], refs[n_in], refs[n_in + 1:]
    streams = [_attn_stream(i, nblk, s, *ins[s * _STREAM_INPUTS:(s + 1) * _STREAM_INPUTS], o_ref,
                            *scr[s * _STREAM_SCRATCH:(s + 1) * _STREAM_SCRATCH]) for s in range(ATTN_STREAMS)]
    n_pairs = jnp.maximum(i, 1) // 2
    states = tuple(prologue() for prologue, _, _ in streams)

    def step(t, states):
        def run(even):
            return lambda sts: tuple(stage(even, t, st) for (_, stage, _), st in zip(streams, sts))
        return lax.cond(t % 2 == 0, run(True), run(False), states)

    states = lax.fori_loop(0, n_pairs, step, states)
    for even in (True, False):
        @pl.when(n_pairs % 2 == (0 if even else 1))
        def _():
            for (_, _, finish), st in zip(streams, states):
                finish(even, st)


def _attn_prompt(qt, kaug, vtb, kmean, near_t):
    n_seq, _, t = qt.shape
    nblk = t // MOBA_BLOCK
    assert nblk % 2 == 0 and nblk < LANES and N_PAIRS % ATTN_STREAMS == 0
    ns = ATTN_STREAMS
    once = dict(pipeline_mode=pl.Buffered(1))

    def stream_specs(s):
        pair = lambda g: ns * g + s
        return [pl.BlockSpec((1, LANES, MOBA_BLOCK), lambda b, g, i: (b, pair(g), i)),
                pl.BlockSpec((1, LANES, MOBA_BLOCK), lambda b, g, i: (b, pair(g), jnp.minimum(i + 1, nblk - 1))),
                pl.BlockSpec((1, t, 2 * LANES), lambda b, g, i: (b, 0, pair(g)), **once),
                pl.BlockSpec((1, nblk, 2 * V_ROWS, MOBA_BLOCK), lambda b, g, i: (b, 0, pair(g), 0), **once),
                pl.BlockSpec((1, nblk, LANES), lambda b, g, i: (b, 0, pair(g))),
                pl.BlockSpec((1, 2 * MOBA_BLOCK, 2 * MOBA_BLOCK), lambda b, g, i: (pair(g), 0, 0), **once)]

    stream_scratch = [pltpu.VMEM((2 * LANES, 2 * MOBA_BLOCK), BF16),
                      pltpu.VMEM((LANES, 2 * MOBA_BLOCK), BF16),
                      pltpu.VMEM((2, MOBA_BLOCK, 2 * MOBA_BLOCK), F32),
                      pltpu.VMEM((2, MOBA_BLOCK, 2 * MOBA_BLOCK), F32),
                      pltpu.VMEM((2, MOBA_BLOCK, 2 * MOBA_BLOCK), BF16),
                      pltpu.VMEM((2, MOBA_BLOCK, 2 * MOBA_BLOCK), BF16)]
    assert len(stream_specs(0)) == _STREAM_INPUTS and len(stream_scratch) == _STREAM_SCRATCH
    return pl.pallas_call(
        functools.partial(_attn_prompt_kernel, nblk=nblk),
        grid=(n_seq, N_PAIRS // ns, nblk),
        in_specs=[spec for s in range(ns) for spec in stream_specs(s)],
        out_specs=pl.BlockSpec((1, MOBA_BLOCK, ns * LANES), lambda b, g, i: (b, i, g)),
        out_shape=jax.ShapeDtypeStruct((n_seq, t, ATT_WIDTH), F32),
        scratch_shapes=stream_scratch * ns,
        compiler_params=pltpu.CompilerParams(dimension_semantics=("arbitrary",) * 3,
                                             vmem_limit_bytes=VMEM_LIMIT),
        name="attn_prompt",
    )(*([qt, qt, kaug, vtb, kmean, near_t] * ns))


def _attn_sample_kernel(pt_ref, tbl_ref, q_ref, kn_ref, vn_ref, near_ref, ck_hbm, cv_hbm, o_ref,
                        s_all, km_t, kbuf, vbuf, ksem, vsem, *, n_pages, nbuf, step, n_new):
    b = pl.program_id(0)
    n_seq = pl.num_programs(0)
    n_rows = N_HEADS * n_new
    page = LANES
    nblk = n_pages * page // MOBA_BLOCK

    def k_copy(seq, p, slot):
        return pltpu.make_async_copy(ck_hbm.at[pt_ref[seq, p]], kbuf.at[slot], ksem.at[slot])

    def v_copy(p, slot):
        return pltpu.make_async_copy(cv_hbm.at[pt_ref[b, p]], vbuf.at[slot], vsem.at[slot])

    @pl.when(b == 0)
    def _():
        for p in range(nbuf):
            k_copy(0, p, p).start()

    row_head = lax.broadcasted_iota(jnp.int32, (n_rows, ATT_WIDTH), 0) // n_new
    col_head = lax.broadcasted_iota(jnp.int32, (n_rows, ATT_WIDTH), 1) // HEAD_DIM
    diag = row_head == col_head
    q = q_ref[0]
    qbd_f = jnp.where(diag, jnp.concatenate([q] * N_HEADS, axis=0), 0.0)
    qbd_b = (qbd_f * SCALE).astype(BF16)
    km_t[...] = jnp.zeros(km_t.shape, F32)
    lane_k = lax.broadcasted_iota(jnp.int32, (ATT_WIDTH, LANES), 1)

    def k_step(it, _):
        p0 = it * step
        slot0 = p0 % nbuf
        for r in range(step):
            k_copy(b, p0 + r, slot0 + r).wait()
        km = km_t[...]
        for r in range(0, step, 2):
            kt0 = kbuf[slot0 + r]
            kt1 = kbuf[slot0 + r + 1]
            s_all[p0 + r] = jnp.dot(qbd_b, kt0.astype(BF16), preferred_element_type=F32)
            s_all[p0 + r + 1] = jnp.dot(qbd_b, kt1.astype(BF16), preferred_element_type=F32)
            mean = jnp.sum(kt0 + kt1, axis=1, keepdims=True) * (1.0 / MOBA_BLOCK)
            km = jnp.where(lane_k == (p0 + r) // 2, mean, km)
        km_t[...] = km

        @pl.when(p0 + nbuf < n_pages)
        def _():
            for r in range(step):
                k_copy(b, p0 + nbuf + r, slot0 + r).start()

        return 0

    lax.fori_loop(0, n_pages // step, k_step, 0)

    @pl.when(b + 1 < n_seq)
    def _():
        for p in range(nbuf):
            k_copy(b + 1, p, p).start()

    for p in range(nbuf):
        v_copy(p, p).start()

    gate = jnp.dot(qbd_f, km_t[...], precision=lax.Precision.HIGHEST,
                   preferred_element_type=F32)
    lane = lax.broadcasted_iota(jnp.int32, (n_rows, LANES), 1)
    sel = _select_topk(gate, lane, 1, nblk)
    rh = lax.broadcasted_iota(jnp.int32, (n_rows, 1), 0) // n_new
    far = jnp.zeros((n_rows, 1), F32)
    for h in range(N_HEADS):
        far = jnp.where(rh == h, tbl_ref[N_BUCKETS - 1, h], far)
    mask = jnp.where(sel, far, NEG)
    near = near_ref[...]
    for n in range(nblk):
        if n == nblk - 1:
            seln = jnp.broadcast_to(mask[:, n:n + 1], (n_rows, page)) > 0.5 * NEG
            s_all[2 * n] = s_all[2 * n] + jnp.where(seln, near[:, 0:page], NEG)
            s_all[2 * n + 1] = s_all[2 * n + 1] + jnp.where(seln, near[:, page:2 * page], NEG)
        else:
            add = jnp.broadcast_to(mask[:, n:n + 1], (n_rows, page))
            s_all[2 * n] = s_all[2 * n] + add
            s_all[2 * n + 1] = s_all[2 * n + 1] + add
    kn = jnp.concatenate([kn_ref[0], jnp.zeros((page - n_new, ATT_WIDTH), F32)], axis=0).astype(BF16)
    s_own = lax.dot_general(qbd_b, kn, (((1,), (1,)), ((), ())),
                            preferred_element_type=F32) + near[:, 2 * page:3 * page]
    mrun = lax.fori_loop(0, n_pages, lambda j, mm: jnp.maximum(mm, s_all[j]), s_own)
    m = jnp.max(mrun, axis=1, keepdims=True)
    p_own = jnp.exp(s_own - m)

    def exp_step(j, lsum):
        pj = jnp.exp(s_all[j] - m)
        s_all[j] = pj
        return lsum + pj

    lsum = lax.fori_loop(0, n_pages, exp_step, p_own)
    l = jnp.sum(lsum, axis=1, keepdims=True)
    vn = jnp.concatenate([vn_ref[0], jnp.zeros((page - n_new, ATT_WIDTH), F32)], axis=0).astype(BF16)
    acc = jnp.dot(p_own.astype(BF16), vn, preferred_element_type=F32)

    def v_step(it, acc):
        p0 = it * step
        slot0 = p0 % nbuf
        for r in range(step):
            v_copy(p0 + r, slot0 + r).wait()
        parts = []
        for r in range(step):
            pb = s_all[p0 + r].astype(BF16)
            vt = vbuf[slot0 + r].astype(BF16)
            parts.append(lax.dot_general(pb, vt, (((1,), (1,)), ((), ())), preferred_element_type=F32))
        while len(parts) > 1:
            parts = [x + y for x, y in zip(parts[0::2], parts[1::2])]

        @pl.when(p0 + nbuf < n_pages)
        def _():
            for r in range(step):
                v_copy(p0 + nbuf + r, slot0 + r).start()

        return acc + parts[0]

    acc = lax.fori_loop(0, n_pages // step, v_step, acc)
    o = jnp.where(diag, acc / l, 0.0)
    o_ref[0] = jnp.sum(o.reshape(N_HEADS, n_new, ATT_WIDTH), axis=0)


def _attn_sample(page_table, rel_bias, q, k_new, v_new, near_s, cache_kt, cache_vt, *, nbuf):
    n_seq, n_new, _ = q.shape
    n_pages = page_table.shape[1]
    n_rows = N_HEADS * n_new
    page = cache_kt.shape[2]
    step = min(8, nbuf)
    assert step & (step - 1) == 0 and step >= 2 and nbuf % step == 0 and n_pages % nbuf == 0
    new_spec = pl.BlockSpec((1, n_new, ATT_WIDTH), lambda b, pt: (b, 0, 0))
    in_specs = [pl.BlockSpec(memory_space=pltpu.SMEM), new_spec, new_spec, new_spec,
                pl.BlockSpec((n_rows, 2 * MOBA_BLOCK), lambda b, pt: (0, 0)),
                pl.BlockSpec(memory_space=pl.ANY), pl.BlockSpec(memory_space=pl.ANY)]
    return pl.pallas_call(
        functools.partial(_attn_sample_kernel, n_pages=n_pages, nbuf=nbuf, step=step, n_new=n_new),
        grid_spec=pltpu.PrefetchScalarGridSpec(
            num_scalar_prefetch=1,
            grid=(n_seq,),
            in_specs=in_specs,
            out_specs=new_spec,
            scratch_shapes=[pltpu.VMEM((n_pages, n_rows, page), F32),
                            pltpu.VMEM((ATT_WIDTH, LANES), F32),
                            pltpu.VMEM((nbuf, ATT_WIDTH, page), F32),
                            pltpu.VMEM((nbuf, ATT_WIDTH, page), F32),
                            pltpu.SemaphoreType.DMA((nbuf,)),
                            pltpu.SemaphoreType.DMA((nbuf,))]),
        out_shape=jax.ShapeDtypeStruct((n_seq, n_new, ATT_WIDTH), F32),
        compiler_params=pltpu.CompilerParams(dimension_semantics=("arbitrary",),
                                             vmem_limit_bytes=VMEM_LIMIT),
        name="attn_sample",
    )(page_table, rel_bias, q, k_new, v_new, near_s, cache_kt, cache_vt)


def _conv_offsets(halo, stride):
    off0 = halo - (CONV_KERNEL - 1) * stride
    return [off0 + j * stride for j in range(CONV_KERNEL)]


def _shift_classes(offs):
    return sorted({o % SUBLANES for o in offs} - {0})


def _mix_kernel(att_ref, glu_ref, x_ref, st_ref, cw_ref, cb_ref, lg_ref, lb_ref, wo_ref, gpm_ref, gpf_ref,
                x1_ref, hn_ref, gbuf, cvbuf, *shifted, tm, stride, halo, chunk):
    i = pl.program_id(1)
    group = min(tm, 128)

    @pl.when(i == 0)
    def _():
        gbuf[0:halo, :] = st_ref[0]

    gbuf[halo:halo + tm, :] = glu_ref[0]
    offs = _conv_offsets(halo, stride)
    for r, ref in zip(_shift_classes(offs), shifted):
        ref[...] = gbuf[r:r + ref.shape[0], :]
    src = dict(zip(_shift_classes(offs), shifted))
    src[0] = gbuf
    for c in range(tm // chunk):
        r0 = c * chunk
        acc = jnp.broadcast_to(cb_ref[...], (chunk, CONV_WIDTH))
        for j, o in enumerate(offs):
            a = o - o % SUBLANES + r0
            acc = acc + cw_ref[j:j + 1, :] * src[o % SUBLANES][a:a + chunk, :]
        mu = jnp.mean(acc, axis=-1, keepdims=True)
        d = acc - mu
        var = jnp.mean(d * d, axis=-1, keepdims=True)
        y = d * lax.rsqrt(var + EPS) * lg_ref[...] + lb_ref[...]
        cvbuf[r0:r0 + chunk, :] = (y * jax.nn.sigmoid(y)).astype(BF16)

        g1 = r0 + chunk
        if g1 % group == 0 or g1 == tm:
            g0 = (g1 - 1) // group * group
            mix = (jnp.dot(att_ref[0, g0:g1, :].astype(BF16), wo_ref[0:ATT_WIDTH, :], preferred_element_type=F32)
                   + jnp.dot(cvbuf[g0:g1, :], wo_ref[ATT_WIDTH:, :], preferred_element_type=F32))
            x1 = x_ref[0, g0:g1, :] + _rms(mix, gpm_ref[...])
            x1_ref[0, g0:g1, :] = x1
            hn_ref[0, g0:g1, :] = _rms(x1, gpf_ref[...]).astype(BF16)

    if tm >= halo:
        gbuf[0:halo, :] = gbuf[tm:tm + halo, :]


def _mix(att, glu, x, state, cw, cb, lg, lb, wo_bf, gpm, gpf, *, tm, stride):
    n_seq, rows, _ = x.shape
    halo = state.shape[1]
    assert rows == tm or tm >= halo
    offs = _conv_offsets(halo, stride)
    row_spec = lambda width: pl.BlockSpec((1, tm, width), lambda s, i: (s, i, 0))
    const = lambda shape: pl.BlockSpec(shape, lambda s, i: (0,) * len(shape))
    return pl.pallas_call(
        functools.partial(_mix_kernel, tm=tm, stride=stride, halo=halo, chunk=min(tm, 32)),
        grid=(n_seq, rows // tm),
        in_specs=[row_spec(ATT_WIDTH), row_spec(CONV_WIDTH), row_spec(D_MODEL),
                  pl.BlockSpec((1, halo, CONV_WIDTH), lambda s, i: (s, 0, 0)),
                  const((CONV_KERNEL, CONV_WIDTH)), const((1, CONV_WIDTH)), const((1, CONV_WIDTH)),
                  const((1, CONV_WIDTH)), const((D_MODEL, D_MODEL)), const((1, D_MODEL)), const((1, D_MODEL))],
        out_specs=[row_spec(D_MODEL), row_spec(D_MODEL)],
        out_shape=[jax.ShapeDtypeStruct((n_seq, rows, D_MODEL), F32),
                   jax.ShapeDtypeStruct((n_seq, rows, D_MODEL), BF16)],
        scratch_shapes=[pltpu.VMEM((halo + tm, CONV_WIDTH), F32), pltpu.VMEM((tm, CONV_WIDTH), BF16)] + [
            pltpu.VMEM((max(o - r for o in offs if o % SUBLANES == r) + tm, CONV_WIDTH), F32)
            for r in _shift_classes(offs)],
        compiler_params=pltpu.CompilerParams(dimension_semantics=("arbitrary", "arbitrary"),
                                             vmem_limit_bytes=VMEM_LIMIT),
        name="mix_s%d" % stride,
    )(att, glu, x, state, cw, cb, lg, lb, wo_bf, gpm, gpf)


def _ffn_kernel(hn_ref, x1_ref, wg_ref, wv_ref, wd_ref, dwg_ref, dwv_ref, dbg_ref, dbv_ref, stg_ref, stv_ref,
                gpo_ref, y_ref, tg_ref, tv_ref, ubuf_g, ubuf_v, car_g, car_v, acc_ref, *, tm, stride, halo, nc):
    i = pl.program_id(1)
    c = pl.program_id(2)
    hn = hn_ref[0]

    @pl.when(i == 0)
    def _():
        ubuf_g[0:halo, :] = stg_ref[0]
        ubuf_v[0:halo, :] = stv_ref[0]

    @pl.when(i > 0)
    def _():
        ubuf_g[0:halo, :] = car_g[c]
        ubuf_v[0:halo, :] = car_v[c]

    @pl.when(c == 0)
    def _():
        acc_ref[...] = jnp.zeros(acc_ref.shape, F32)

    def up(c0, c1):
        ubuf_g[halo:halo + tm, c0:c1] = jnp.dot(hn, wg_ref[:, c0:c1], preferred_element_type=F32)
        ubuf_v[halo:halo + tm, c0:c1] = jnp.dot(hn, wv_ref[:, c0:c1], preferred_element_type=F32)

    def conv(c0, c1, dw_ref, db_ref, ubuf):
        return (dw_ref[0:1, c0:c1] * ubuf[halo - 2 * stride:halo - 2 * stride + tm, c0:c1]
                + dw_ref[1:2, c0:c1] * ubuf[halo - stride:halo - stride + tm, c0:c1]
                + dw_ref[2:3, c0:c1] * ubuf[halo:halo + tm, c0:c1] + db_ref[:, c0:c1])

    ck = wg_ref.shape[1]
    bounds = [(c0, min(c0 + MXU_COLS, ck)) for c0 in range(0, ck, MXU_COLS)]
    def down(c0, c1, a):
        acc_ref[...] += jnp.dot(a, wd_ref[c0:c1, :], preferred_element_type=F32)

    up(*bounds[0])
    pending = None
    for n, (c0, c1) in enumerate(bounds):
        if n + 1 < len(bounds):
            up(*bounds[n + 1])
        if pending is not None:
            down(*pending)
        a = jax.nn.gelu(conv(c0, c1, dwg_ref, dbg_ref, ubuf_g), approximate=True) * conv(c0, c1, dwv_ref, dbv_ref,
                                                                                        ubuf_v)
        pending = (c0, c1, a.astype(BF16))
    down(*pending)

    for ubuf, car, tail_ref in ((ubuf_g, car_g, tg_ref), (ubuf_v, car_v, tv_ref)):
        last = ubuf[tm:tm + halo, :]
        car[c] = last
        tail_ref[0, 0] = last

    @pl.when(c == nc - 1)
    def _():
        y_ref[0] = x1_ref[0] + _rms(acc_ref[...], gpo_ref[...])


def _ffn(hn, x1, wup_bf, wdn_bf, dw, db, state, gpo, *, tm, stride, nc):
    n_seq, rows, _ = x1.shape
    halo = state.shape[1]
    ck = D_FF // nc
    row_spec = lambda width: pl.BlockSpec((1, tm, width), lambda s, i, c: (s, i, 0))
    mode = dict(pipeline_mode=pl.Buffered(1)) if nc == 1 else {}
    gate_cols = lambda shape, **kw: pl.BlockSpec(shape, lambda s, i, c: (0, c), **kw)
    val_cols = lambda shape, **kw: pl.BlockSpec(shape, lambda s, i, c: (0, c + nc), **kw)
    return pl.pallas_call(
        functools.partial(_ffn_kernel, tm=tm, stride=stride, halo=halo, nc=nc),
        grid=(n_seq, rows // tm, nc),
        in_specs=[row_spec(D_MODEL), row_spec(D_MODEL),
                  gate_cols((D_MODEL, ck), **mode), val_cols((D_MODEL, ck), **mode),
                  pl.BlockSpec((ck, D_MODEL), lambda s, i, c: (c, 0), **mode),
                  gate_cols((FFN_KERNEL, ck)), val_cols((FFN_KERNEL, ck)),
                  gate_cols((1, ck)), val_cols((1, ck)),
                  pl.BlockSpec((1, halo, ck), lambda s, i, c: (s, 0, c)),
                  pl.BlockSpec((1, halo, ck), lambda s, i, c: (s, 0, c + nc)),
                  pl.BlockSpec((1, D_MODEL), lambda s, i, c: (0, 0))],
        out_specs=[row_spec(D_MODEL),
                   pl.BlockSpec((1, 1, halo, ck), lambda s, i, c: (s, i, 0, c)),
                   pl.BlockSpec((1, 1, halo, ck), lambda s, i, c: (s, i, 0, c))],
        out_shape=[jax.ShapeDtypeStruct((n_seq, rows, D_MODEL), F32),
                   jax.ShapeDtypeStruct((n_seq, rows // tm, halo, D_FF), F32),
                   jax.ShapeDtypeStruct((n_seq, rows // tm, halo, D_FF), F32)],
        scratch_shapes=[pltpu.VMEM((halo + tm, ck), F32), pltpu.VMEM((halo + tm, ck), F32),
                        pltpu.VMEM((nc, halo, ck), F32), pltpu.VMEM((nc, halo, ck), F32),
                        pltpu.VMEM((tm, D_MODEL), F32)],
        compiler_params=pltpu.CompilerParams(dimension_semantics=("arbitrary",) * 3,
                                             vmem_limit_bytes=VMEM_LIMIT),
        name="ffn_s%d" % stride,
    )(hn, x1, wup_bf, wup_bf, wdn_bf, dw, dw, db, db, state, state, gpo)


def _tile(rows, want):
    return want if rows % want == 0 else rows


def kernel(x_prompt, x_sample, cache_k, cache_v, state_conv, state_ffn, page_table, rel_bias, g_pre_mix, w_in,
           conv_dw_w, conv_dw_b, conv_ln_g, conv_ln_b, w_out, g_post_mix, g_pre_ffn, w_ffn_up, ffn_dw_w,
           ffn_dw_b, w_ffn_down, g_post_ffn):
    depth = w_in.shape[0]
    assert depth == 1, "single-layer trunk"
    bp, seq, _ = x_prompt.shape
    bs, n_new, _ = x_sample.shape
    n_pool, page = cache_k.shape[1], cache_k.shape[2]
    n_pages = page_table.shape[1]
    assert seq % MOBA_BLOCK == 0 and page == LANES and (n_pages * page) % MOBA_BLOCK == 0
    assert n_new == SUBLANES and n_pages * page // MOBA_BLOCK <= LANES

    w_in_bf = w_in[0].astype(BF16)
    w_out_bf = w_out[0].astype(BF16)
    w_up_bf = w_ffn_up[0].astype(BF16)
    w_dn_bf = w_ffn_down[0].astype(BF16)
    cw, cb = conv_dw_w[0], conv_dw_b
    near_t, near_s = _bias_tiles(rel_bias)
    conv_halo_p = 32
    ffn_halo_p = SUBLANES

    tm_p = _tile(seq, 512)
    qt, kt, vt, kbf, vtb, kmean, glu_p = _proj(x_prompt, g_pre_mix, w_in_bf, prompt=True, tm=tm_p)
    att_p = _attn_prompt(qt, kbf, vtb, kmean.reshape(bp, seq // MOBA_BLOCK, ATT_WIDTH), near_t)
    x1_p, hn_p = _mix(att_p, glu_p, x_prompt, jnp.zeros((bp, conv_halo_p, CONV_WIDTH), F32), cw, cb,
                      conv_ln_g, conv_ln_b, w_out_bf, g_post_mix, g_pre_ffn, tm=tm_p, stride=1)
    y_prompt, tail_g, tail_v = _ffn(hn_p, x1_p, w_up_bf, w_dn_bf, ffn_dw_w[0], ffn_dw_b,
                                    jnp.zeros((bp, ffn_halo_p, 2 * D_FF), F32), g_post_ffn,
                                    tm=tm_p, stride=1, nc=1)
    k_prompt = kt.reshape(1, bp, N_HEADS, HEAD_DIM, seq).transpose(0, 1, 4, 2, 3)
    v_prompt = vt.reshape(1, bp, N_HEADS, HEAD_DIM, seq).transpose(0, 1, 4, 2, 3)
    conv_prompt = glu_p[:, seq - (CONV_KERNEL - 1):, :][None]
    ffn_prompt = jnp.concatenate([tail_g[:, -1], tail_v[:, -1]],
                                 axis=-1)[:, ffn_halo_p - (FFN_KERNEL - 1):, :][None]

    rows_s = n_new * bs
    to_tb = lambda a: a.transpose(1, 0, 2).reshape(1, rows_s, a.shape[-1])
    to_bt = lambda a: a.reshape(n_new, bs, a.shape[-1]).transpose(1, 0, 2)
    xs = to_tb(x_sample)
    q_s, k_s, v_s, glu_s = _proj(xs, g_pre_mix, w_in_bf, prompt=False, tm=rows_s)
    q_b, k_b, v_b = to_bt(q_s), to_bt(k_s), to_bt(v_s)
    cache_kt = cache_k[0].transpose(0, 2, 3, 1).reshape(n_pool, ATT_WIDTH, page)
    cache_vt = cache_v[0].transpose(0, 2, 3, 1).reshape(n_pool, ATT_WIDTH, page)
    att_b = _attn_sample(page_table, rel_bias, q_b, k_b, v_b, near_s, cache_kt, cache_vt,
                         nbuf=min(32, n_pages))
    conv_state = state_conv[0].transpose(1, 0, 2).reshape(1, (CONV_KERNEL - 1) * bs, CONV_WIDTH)
    x1_s, hn_s = _mix(to_tb(att_b), glu_s, xs, conv_state, cw, cb, conv_ln_g, conv_ln_b, w_out_bf,
                      g_post_mix, g_pre_ffn, tm=rows_s, stride=bs)
    ffn_state = state_ffn[0].transpose(1, 0, 2).reshape(1, (FFN_KERNEL - 1) * bs, 2 * D_FF)
    y_s, tail_gs, tail_vs = _ffn(hn_s, x1_s, w_up_bf, w_dn_bf, ffn_dw_w[0], ffn_dw_b, ffn_state, g_post_ffn,
                                 tm=rows_s, stride=bs, nc=2)
    y_sample = to_bt(y_s)
    k_sample = k_b.reshape(1, bs, n_new, N_HEADS, HEAD_DIM)
    v_sample = v_b.reshape(1, bs, n_new, N_HEADS, HEAD_DIM)
    conv_all = jnp.concatenate([conv_state[0], glu_s[0]], axis=0)[n_new * bs:]
    conv_sample = conv_all.reshape(CONV_KERNEL - 1, bs, CONV_WIDTH).transpose(1, 0, 2)[None]
    ffn_sample = jnp.concatenate([tail_gs[:, -1], tail_vs[:, -1]], axis=-1).reshape(
        FFN_KERNEL - 1, bs, 2 * D_FF).transpose(1, 0, 2)[None]

    return (y_prompt, y_sample, k_prompt, v_prompt, conv_prompt, ffn_prompt,
            k_sample, v_sample, conv_sample, ffn_sample)
```

```python
import functools
import math

import numpy as np
import jax
import jax.numpy as jnp
from jax import lax
from jax.experimental import pallas as pl
from jax.experimental.pallas import tpu as pltpu

F32 = jnp.float32
BF16 = jnp.bfloat16

D_MODEL = 1024
HEAD_DIM = 64
ATT_WIDTH = 512
N_HEADS = 8
N_PAIRS = N_HEADS // 2
CONV_WIDTH = 512
CONV_KERNEL = 31
MOBA_BLOCK = 256
MOBA_TOPK = 3
N_BUCKETS = 32
MAX_DISTANCE = 128
D_FF = 2816
FFN_KERNEL = 3
EPS = 1e-6
SCALE = HEAD_DIM ** -0.5
LOG2E = math.log2(math.e)
NEG = -1e30
LANES = 128
SUBLANES = 8
VMEM_LIMIT = 56 * 1024 * 1024
MXU_COLS = 256
ONES_ROWS = 16
V_ROWS = HEAD_DIM + ONES_ROWS


def _bucket_thresholds():
    n = np.arange(0, 4 * MAX_DISTANCE)
    max_exact = N_BUCKETS // 2
    nf = np.maximum(n, 1).astype(np.float64)
    large = max_exact + (np.log(nf / max_exact) / math.log(MAX_DISTANCE / max_exact)
                         * (N_BUCKETS - max_exact)).astype(np.int64)
    bucket = np.where(n < max_exact, n, np.minimum(large, N_BUCKETS - 1))
    return tuple(int(np.argmax(bucket >= k)) for k in range(1, N_BUCKETS))


_BUCKET_THR = _bucket_thresholds()


def _rms(x, g):
    return x * lax.rsqrt(jnp.mean(x * x, axis=-1, keepdims=True) + EPS) * g


def _bias_kernel(tbl_ref, near_t_ref, near_s_ref):
    p = pl.program_id(0)

    def bias_of(dist, h):
        b = jnp.full(dist.shape, tbl_ref[0, h], F32)
        for k in range(1, N_BUCKETS):
            b = jnp.where(dist >= _BUCKET_THR[k - 1], tbl_ref[k, h], b)
        return jnp.where(dist >= 0, b, NEG)

    jj = lax.broadcasted_iota(jnp.int32, (2 * MOBA_BLOCK, MOBA_BLOCK), 0)
    ii = lax.broadcasted_iota(jnp.int32, (2 * MOBA_BLOCK, MOBA_BLOCK), 1)
    dist = MOBA_BLOCK + ii - jj
    for half in range(2):
        h = 2 * p + half
        near_t_ref[0, :, half * MOBA_BLOCK:(half + 1) * MOBA_BLOCK] = (
            bias_of(dist, h) - tbl_ref[N_BUCKETS - 1, h]) * LOG2E

    @pl.when(p == 0)
    def _():
        tt = lax.broadcasted_iota(jnp.int32, (SUBLANES, 2 * MOBA_BLOCK), 0)
        j2 = lax.broadcasted_iota(jnp.int32, (SUBLANES, 2 * MOBA_BLOCK), 1)
        d2 = MOBA_BLOCK + tt - j2
        for h in range(N_HEADS):
            near_s_ref[h * SUBLANES:(h + 1) * SUBLANES, :] = bias_of(d2, h)


def _bias_tiles(rel_bias):
    return pl.pallas_call(
        _bias_kernel,
        grid=(N_PAIRS,),
        in_specs=[pl.BlockSpec(memory_space=pltpu.SMEM)],
        out_specs=[pl.BlockSpec((1, 2 * MOBA_BLOCK, 2 * MOBA_BLOCK), lambda p: (p, 0, 0)),
                   pl.BlockSpec((N_HEADS * SUBLANES, 2 * MOBA_BLOCK), lambda p: (0, 0))],
        out_shape=[jax.ShapeDtypeStruct((N_PAIRS, 2 * MOBA_BLOCK, 2 * MOBA_BLOCK), F32),
                   jax.ShapeDtypeStruct((N_HEADS * SUBLANES, 2 * MOBA_BLOCK), F32)],
        compiler_params=pltpu.CompilerParams(dimension_semantics=("arbitrary",)),
        name="bias_tiles",
    )(rel_bias)


def _proj_kernel(x_ref, g_ref, w_ref, *out_refs, prompt, tm):
    h = _rms(x_ref[0], g_ref[...])
    p = jnp.dot(h.astype(BF16), w_ref[...], preferred_element_type=F32)
    q = p[:, 0:ATT_WIDTH]
    k = p[:, ATT_WIDTH:2 * ATT_WIDTH]
    v = p[:, 2 * ATT_WIDTH:3 * ATT_WIDTH]
    ga = p[:, 3 * ATT_WIDTH:3 * ATT_WIDTH + CONV_WIDTH]
    gb = p[:, 3 * ATT_WIDTH + CONV_WIDTH:]
    glu = ga * jax.nn.sigmoid(gb)
    if prompt:
        qt_ref, kt_ref, vt_ref, kbf_ref, vtb_ref, km_ref, glu_ref = out_refs
        qt_ref[0] = q.T
        kt_ref[0] = k.T
        vt = v.T
        vt_ref[0] = vt
        row_blk = (pl.program_id(1) * (tm // MOBA_BLOCK)
                   + lax.broadcasted_iota(jnp.int32, (tm, LANES), 0) // MOBA_BLOCK)
        onehot = jnp.where(lax.broadcasted_iota(jnp.int32, (tm, LANES), 1) == row_blk, 1.0, 0.0).astype(BF16)
        kb = k.astype(BF16)
        for pr in range(N_PAIRS):
            kbf_ref[0, :, 2 * pr * LANES:(2 * pr + 1) * LANES] = kb[:, pr * LANES:(pr + 1) * LANES]
            kbf_ref[0, :, (2 * pr + 1) * LANES:(2 * pr + 2) * LANES] = onehot
        ones = jnp.ones((ONES_ROWS, MOBA_BLOCK), F32)
        for c in range(tm // MOBA_BLOCK):
            vc = vt[:, c * MOBA_BLOCK:(c + 1) * MOBA_BLOCK]
            parts = []
            for hd in range(N_HEADS):
                parts += [vc[hd * HEAD_DIM:(hd + 1) * HEAD_DIM], ones]
            vtb_ref[0, c] = jnp.concatenate(parts, axis=0).astype(BF16)
        km_ref[0] = jnp.mean(k.reshape(tm // MOBA_BLOCK, MOBA_BLOCK, ATT_WIDTH), axis=1)[:, None, :]
        glu_ref[0] = glu
    else:
        q_ref, k_ref, v_ref, glu_ref = out_refs
        q_ref[0] = q
        k_ref[0] = k
        v_ref[0] = v
        glu_ref[0] = glu


def _proj(x, g, w_bf, *, prompt, tm):
    n_seq, rows, _ = x.shape
    n_out = w_bf.shape[1]
    grid = (n_seq, rows // tm)
    row_spec = lambda width: pl.BlockSpec((1, tm, width), lambda s, i: (s, i, 0))
    col_spec = pl.BlockSpec((1, ATT_WIDTH, tm), lambda s, i: (s, 0, i))
    if prompt:
        nb = tm // MOBA_BLOCK
        out_specs = [col_spec, col_spec, col_spec, row_spec(2 * ATT_WIDTH),
                     pl.BlockSpec((1, nb, N_HEADS * V_ROWS, MOBA_BLOCK), lambda s, i: (s, i, 0, 0)),
                     pl.BlockSpec((1, nb, 1, ATT_WIDTH), lambda s, i: (s, i, 0, 0)),
                     row_spec(CONV_WIDTH)]
        t_shape = jax.ShapeDtypeStruct((n_seq, ATT_WIDTH, rows), F32)
        out_shape = [t_shape, t_shape, t_shape,
                     jax.ShapeDtypeStruct((n_seq, rows, 2 * ATT_WIDTH), BF16),
                     jax.ShapeDtypeStruct((n_seq, rows // MOBA_BLOCK, N_HEADS * V_ROWS, MOBA_BLOCK), BF16),
                     jax.ShapeDtypeStruct((n_seq, rows // MOBA_BLOCK, 1, ATT_WIDTH), F32),
                     jax.ShapeDtypeStruct((n_seq, rows, CONV_WIDTH), F32)]
    else:
        out_specs = [row_spec(ATT_WIDTH)] * 3 + [row_spec(CONV_WIDTH)]
        out_shape = [jax.ShapeDtypeStruct((n_seq, rows, ATT_WIDTH), F32)] * 3 + [
            jax.ShapeDtypeStruct((n_seq, rows, CONV_WIDTH), F32)]
    return pl.pallas_call(
        functools.partial(_proj_kernel, prompt=prompt, tm=tm),
        grid=grid,
        in_specs=[row_spec(D_MODEL),
                  pl.BlockSpec((1, D_MODEL), lambda s, i: (0, 0)),
                  pl.BlockSpec((D_MODEL, n_out), lambda s, i: (0, 0))],
        out_specs=out_specs,
        out_shape=out_shape,
        compiler_params=pltpu.CompilerParams(dimension_semantics=("arbitrary", "arbitrary"),
                                             vmem_limit_bytes=VMEM_LIMIT),
        name="proj_prompt" if prompt else "proj_sample",
    )(x, g, w_bf)


def _select_topk(gate, idx, axis, n_valid):
    big = jnp.int32(2 ** 30)
    g = jnp.where(idx < n_valid, gate, -jnp.inf)
    sel = jnp.zeros(gate.shape, jnp.bool_)
    for _ in range(MOBA_TOPK):
        mx = jnp.max(g, axis=axis, keepdims=True)
        first = jnp.min(jnp.where(g == mx, idx, big), axis=axis, keepdims=True)
        hit = idx == first
        sel = sel | (hit & (mx > -jnp.inf))
        g = jnp.where(hit, -jnp.inf, g)
    return sel


def _attn_stream(i, nblk, s, qt_ref, qtn_ref, k_ref, vt_ref, km_ref, near_ref, o_ref, qaug_ref, mnext_ref,
                 sa_ref, sb_ref, pa_ref, pb_ref, acc_ref):
    blk = MOBA_BLOCK

    def block_diag(qt):
        z = jnp.zeros((HEAD_DIM, blk), F32)
        return jnp.concatenate([jnp.concatenate([qt[:HEAD_DIM], z], axis=1),
                                jnp.concatenate([z, qt[HEAD_DIM:]], axis=1)], axis=0)

    def mask_rows(q2t, tile):
        gate = jnp.dot(km_ref[0], q2t, precision=lax.Precision.HIGHEST, preferred_element_type=F32)
        n_idx = lax.broadcasted_iota(jnp.int32, (nblk, 2 * blk), 0)
        sel = _select_topk(gate, n_idx, 0, tile)
        return jnp.concatenate([jnp.where(sel | (n_idx == tile), 0.0, NEG),
                                jnp.full((LANES - nblk, 2 * blk), NEG, F32)], axis=0).astype(BF16)

    def scores_into(dst_ref, j0, j1, bias):
        mx = None
        for half, j in enumerate((j0, j1)):
            kb = k_ref[0, pl.ds(pl.multiple_of(j * blk, blk), blk), :]
            s = jnp.dot(kb, qaug_ref[...], preferred_element_type=F32)
            if bias is not None:
                s = s + bias[half]
            dst_ref[half] = s
            smax = jnp.max(s, axis=0, keepdims=True)
            mx = smax if mx is None else jnp.maximum(mx, smax)
        return mx

    def probs(src_ref, dst_ref, m, mx):
        m_new = jnp.maximum(m, mx)
        dst_ref[0] = jnp.exp2(src_ref[0] - m_new).astype(BF16)
        dst_ref[1] = jnp.exp2(src_ref[1] - m_new).astype(BF16)
        return m_new, jnp.exp2(m - m_new)

    def accumulate(p_ref, j0, j1, alpha):
        p0, p1 = p_ref[0], p_ref[1]
        v0, v1 = vt_ref[0, j0], vt_ref[0, j1]
        d_a = (jnp.dot(v0[:V_ROWS], p0[:, :blk], preferred_element_type=F32)
               + jnp.dot(v1[:V_ROWS], p1[:, :blk], preferred_element_type=F32))
        d_b = (jnp.dot(v0[V_ROWS:], p0[:, blk:], preferred_element_type=F32)
               + jnp.dot(v1[V_ROWS:], p1[:, blk:], preferred_element_type=F32))
        acc_ref[0] = acc_ref[0] * alpha[:, :blk] + d_a
        acc_ref[1] = acc_ref[1] * alpha[:, blk:] + d_b

    none = nblk - 1
    prev = jnp.where(i > 0, i - 1, none)
    n_far = jnp.maximum(i - 1, 0)

    def prologue():
        qaug_ref[0:LANES, :] = (block_diag(qt_ref[0]) * (SCALE * LOG2E)).astype(BF16)
        row = lax.broadcasted_iota(jnp.int32, (LANES, 2 * blk), 0)
        qaug_ref[LANES:, :] = jnp.where((i == 0) & (row == 0), 0.0,
                                        jnp.where(i == 0, NEG, mnext_ref[...].astype(F32))).astype(BF16)

        mx = scores_into(sa_ref, i, prev, (near_ref[0, blk:, :], near_ref[0, :blk, :]))
        mnext_ref[...] = mask_rows(block_diag(qtn_ref[0]), i + 1)
        pb_ref[...] = jnp.zeros(pb_ref.shape, BF16)
        acc_ref[...] = jnp.zeros(acc_ref.shape, F32)
        return jnp.full((1, 2 * blk), NEG, F32), mx, jnp.ones((1, 2 * blk), F32), i, prev, i, i

    def stage(even, t, state):
        s_src, s_dst, p_dst, p_src = (sa_ref, sb_ref, pa_ref, pb_ref) if even else (sb_ref, sa_ref, pb_ref, pa_ref)
        m, mx, alpha_p, js0, js1, jp0, jp1 = state
        j0 = 2 * t
        j1 = jnp.where(j0 + 1 < n_far, j0 + 1, none)
        mx_next = scores_into(s_dst, j0, j1, None)
        m_new, alpha = probs(s_src, p_dst, m, mx)
        accumulate(p_src, jp0, jp1, alpha_p)
        return m_new, mx_next, alpha, j0, j1, js0, js1

    def finish(even, state):
        s_src, p_dst, p_src = (sa_ref, pa_ref, pb_ref) if even else (sb_ref, pb_ref, pa_ref)
        m, mx, alpha_p, js0, js1, jp0, jp1 = state
        accumulate(p_src, jp0, jp1, alpha_p)
        _, alpha = probs(s_src, p_dst, m, mx)
        accumulate(p_dst, js0, js1, alpha)
        a_a, a_b = acc_ref[0], acc_ref[1]
        out_t = jnp.concatenate([a_a[:HEAD_DIM] / a_a[HEAD_DIM:HEAD_DIM + 1],
                                 a_b[:HEAD_DIM] / a_b[HEAD_DIM:HEAD_DIM + 1]], axis=0)
        o_ref[0, :, s * LANES:(s + 1) * LANES] = out_t.T

    return prologue, stage, finish


ATTN_STREAMS = 4
_STREAM_INPUTS = 6
_STREAM_SCRATCH = 7
_MNEXT_SLOT = 1


def _attn_prompt_kernel(*refs, nblk):
    i = pl.program_id(2)
    n_in = ATTN_STREAMS * _STREAM_INPUTS
    ins, o_ref, scr = refs[:n_in], refs[n_in], refs[n_in + 1:]
    streams = [_attn_stream(i, nblk, s, *ins[s * _STREAM_INPUTS:(s + 1) * _STREAM_INPUTS], o_ref,
                            *scr[s * _STREAM_SCRATCH:(s + 1) * _STREAM_SCRATCH]) for s in range(ATTN_STREAMS)]
    n_pairs = jnp.maximum(i, 1) // 2

    @pl.when((pl.program_id(0) == 0) & (pl.program_id(1) == 0) & (i == 0))
    def _():
        for s in range(ATTN_STREAMS):
            mnext = scr[s * _STREAM_SCRATCH + _MNEXT_SLOT]
            mnext[...] = jnp.zeros(mnext.shape, BF16)

    states = tuple(prologue() for prologue, _, _ in streams)

    def step(t, states):
        def run(even):
            return lambda sts: tuple(stage(even, t, st) for (_, stage, _), st in zip(streams, sts))
        return lax.cond(t % 2 == 0, run(True), run(False), states)

    states = lax.fori_loop(0, n_pairs, step, states)
    for even in (True, False):
        @pl.when(n_pairs % 2 == (0 if even else 1))
        def _():
            for (_, _, finish), st in zip(streams, states):
                finish(even, st)


def _attn_prompt(qt, kaug, vtb, kmean, near_t):
    n_seq, _, t = qt.shape
    nblk = t // MOBA_BLOCK
    assert nblk % 2 == 0 and nblk < LANES and N_PAIRS % ATTN_STREAMS == 0
    ns = ATTN_STREAMS
    once = dict(pipeline_mode=pl.Buffered(1))

    def stream_specs(s):
        pair = lambda g: ns * g + s
        return [pl.BlockSpec((1, LANES, MOBA_BLOCK), lambda b, g, i: (b, pair(g), i)),
                pl.BlockSpec((1, LANES, MOBA_BLOCK), lambda b, g, i: (b, pair(g), jnp.minimum(i + 1, nblk - 1))),
                pl.BlockSpec((1, t, 2 * LANES), lambda b, g, i: (b, 0, pair(g)), **once),
                pl.BlockSpec((1, nblk, 2 * V_ROWS, MOBA_BLOCK), lambda b, g, i: (b, 0, pair(g), 0), **once),
                pl.BlockSpec((1, nblk, LANES), lambda b, g, i: (b, 0, pair(g))),
                pl.BlockSpec((1, 2 * MOBA_BLOCK, 2 * MOBA_BLOCK), lambda b, g, i: (pair(g), 0, 0), **once)]

    stream_scratch = [pltpu.VMEM((2 * LANES, 2 * MOBA_BLOCK), BF16),
                      pltpu.VMEM((LANES, 2 * MOBA_BLOCK), BF16),
                      pltpu.VMEM((2, MOBA_BLOCK, 2 * MOBA_BLOCK), F32),
                      pltpu.VMEM((2, MOBA_BLOCK, 2 * MOBA_BLOCK), F32),
                      pltpu.VMEM((2, MOBA_BLOCK, 2 * MOBA_BLOCK), BF16),
                      pltpu.VMEM((2, MOBA_BLOCK, 2 * MOBA_BLOCK), BF16),
                      pltpu.VMEM((2, V_ROWS, MOBA_BLOCK), F32)]
    assert len(stream_specs(0)) == _STREAM_INPUTS and len(stream_scratch) == _STREAM_SCRATCH
    return pl.pallas_call(
        functools.partial(_attn_prompt_kernel, nblk=nblk),
        grid=(n_seq, N_PAIRS // ns, nblk),
        in_specs=[spec for s in range(ns) for spec in stream_specs(s)],
        out_specs=pl.BlockSpec((1, MOBA_BLOCK, ns * LANES), lambda b, g, i: (b, i, g)),
        out_shape=jax.ShapeDtypeStruct((n_seq, t, ATT_WIDTH), F32),
        scratch_shapes=stream_scratch * ns,
        compiler_params=pltpu.CompilerParams(dimension_semantics=("arbitrary",) * 3,
                                             vmem_limit_bytes=VMEM_LIMIT),
        name="attn_prompt",
    )(*([qt, qt, kaug, vtb, kmean, near_t] * ns))


def _attn_sample_kernel(pt_ref, tbl_ref, q_ref, kn_ref, vn_ref, near_ref, ck_hbm, cv_hbm, o_ref,
                        s_all, km_t, kbuf, vbuf, ksem, vsem, *, n_pages, nbuf, step, n_new):
    b = pl.program_id(0)
    n_seq = pl.num_programs(0)
    n_rows = N_HEADS * n_new
    page = LANES
    nblk = n_pages * page // MOBA_BLOCK

    def k_copy(seq, p, slot):
        return pltpu.make_async_copy(ck_hbm.at[pt_ref[seq, p]], kbuf.at[slot], ksem.at[slot])

    def v_copy(p, slot):
        return pltpu.make_async_copy(cv_hbm.at[pt_ref[b, p]], vbuf.at[slot], vsem.at[slot])

    @pl.when(b == 0)
    def _():
        for p in range(nbuf):
            k_copy(0, p, p).start()

    row_head = lax.broadcasted_iota(jnp.int32, (n_rows, ATT_WIDTH), 0) // n_new
    col_head = lax.broadcasted_iota(jnp.int32, (n_rows, ATT_WIDTH), 1) // HEAD_DIM
    diag = row_head == col_head
    q = q_ref[0]
    qbd_f = jnp.where(diag, jnp.concatenate([q] * N_HEADS, axis=0), 0.0)
    qbd_b = (qbd_f * SCALE).astype(BF16)
    km_t[...] = jnp.zeros(km_t.shape, F32)
    lane_k = lax.broadcasted_iota(jnp.int32, (ATT_WIDTH, LANES), 1)

    def k_step(it, _):
        p0 = it * step
        slot0 = p0 % nbuf
        for r in range(step):
            k_copy(b, p0 + r, slot0 + r).wait()
        km = km_t[...]
        for r in range(0, step, 2):
            kt0 = kbuf[slot0 + r]
            kt1 = kbuf[slot0 + r + 1]
            s_all[p0 + r] = jnp.dot(qbd_b, kt0.astype(BF16), preferred_element_type=F32)
            s_all[p0 + r + 1] = jnp.dot(qbd_b, kt1.astype(BF16), preferred_element_type=F32)
            mean = jnp.sum(kt0 + kt1, axis=1, keepdims=True) * (1.0 / MOBA_BLOCK)
            km = jnp.where(lane_k == (p0 + r) // 2, mean, km)
        km_t[...] = km

        @pl.when(p0 + nbuf < n_pages)
        def _():
            for r in range(step):
                k_copy(b, p0 + nbuf + r, slot0 + r).start()

        return 0

    lax.fori_loop(0, n_pages // step, k_step, 0)

    @pl.when(b + 1 < n_seq)
    def _():
        for p in range(nbuf):
            k_copy(b + 1, p, p).start()

    for p in range(nbuf):
        v_copy(p, p).start()

    gate = jnp.dot(qbd_f, km_t[...], precision=lax.Precision.HIGHEST,
                   preferred_element_type=F32)
    lane = lax.broadcasted_iota(jnp.int32, (n_rows, LANES), 1)
    sel = _select_topk(gate, lane, 1, nblk)
    rh = lax.broadcasted_iota(jnp.int32, (n_rows, 1), 0) // n_new
    far = jnp.zeros((n_rows, 1), F32)
    for h in range(N_HEADS):
        far = jnp.where(rh == h, tbl_ref[N_BUCKETS - 1, h], far)
    mask = jnp.where(sel, far, NEG)
    near = near_ref[...]
    for n in range(nblk):
        if n == nblk - 1:
            seln = jnp.broadcast_to(mask[:, n:n + 1], (n_rows, page)) > 0.5 * NEG
            s_all[2 * n] = s_all[2 * n] + jnp.where(seln, near[:, 0:page], NEG)
            s_all[2 * n + 1] = s_all[2 * n + 1] + jnp.where(seln, near[:, page:2 * page], NEG)
        else:
            add = jnp.broadcast_to(mask[:, n:n + 1], (n_rows, page))
            s_all[2 * n] = s_all[2 * n] + add
            s_all[2 * n + 1] = s_all[2 * n + 1] + add
    kn = jnp.concatenate([kn_ref[0], jnp.zeros((page - n_new, ATT_WIDTH), F32)], axis=0).astype(BF16)
    s_own = lax.dot_general(qbd_b, kn, (((1,), (1,)), ((), ())),
                            preferred_element_type=F32) + near[:, 2 * page:3 * page]
    mrun = lax.fori_loop(0, n_pages, lambda j, mm: jnp.maximum(mm, s_all[j]), s_own)
    m = jnp.max(mrun, axis=1, keepdims=True)
    p_own = jnp.exp(s_own - m)

    def exp_step(j, lsum):
        pj = jnp.exp(s_all[j] - m)
        s_all[j] = pj
        return lsum + pj

    lsum = lax.fori_loop(0, n_pages, exp_step, p_own)
    l = jnp.sum(lsum, axis=1, keepdims=True)
    vn = jnp.concatenate([vn_ref[0], jnp.zeros((page - n_new, ATT_WIDTH), F32)], axis=0).astype(BF16)
    acc = jnp.dot(p_own.astype(BF16), vn, preferred_element_type=F32)

    def v_step(it, acc):
        p0 = it * step
        slot0 = p0 % nbuf
        for r in range(step):
            v_copy(p0 + r, slot0 + r).wait()
        parts = []
        for r in range(step):
            pb = s_all[p0 + r].astype(BF16)
            vt = vbuf[slot0 + r].astype(BF16)
            parts.append(lax.dot_general(pb, vt, (((1,), (1,)), ((), ())), preferred_element_type=F32))
        while len(parts) > 1:
            parts = [x + y for x, y in zip(parts[0::2], parts[1::2])]

        @pl.when(p0 + nbuf < n_pages)
        def _():
            for r in range(step):
                v_copy(p0 + nbuf + r, slot0 + r).start()

        return acc + parts[0]

    acc = lax.fori_loop(0, n_pages // step, v_step, acc)
    o = jnp.where(diag, acc / l, 0.0)
    o_ref[0] = jnp.sum(o.reshape(N_HEADS, n_new, ATT_WIDTH), axis=0)


def _attn_sample(page_table, rel_bias, q, k_new, v_new, near_s, cache_kt, cache_vt, *, nbuf):
    n_seq, n_new, _ = q.shape
    n_pages = page_table.shape[1]
    n_rows = N_HEADS * n_new
    page = cache_kt.shape[2]
    step = min(8, nbuf)
    assert step & (step - 1) == 0 and step >= 2 and nbuf % step == 0 and n_pages % nbuf == 0
    new_spec = pl.BlockSpec((1, n_new, ATT_WIDTH), lambda b, pt: (b, 0, 0))
    in_specs = [pl.BlockSpec(memory_space=pltpu.SMEM), new_spec, new_spec, new_spec,
                pl.BlockSpec((n_rows, 2 * MOBA_BLOCK), lambda b, pt: (0, 0)),
                pl.BlockSpec(memory_space=pl.ANY), pl.BlockSpec(memory_space=pl.ANY)]
    return pl.pallas_call(
        functools.partial(_attn_sample_kernel, n_pages=n_pages, nbuf=nbuf, step=step, n_new=n_new),
        grid_spec=pltpu.PrefetchScalarGridSpec(
            num_scalar_prefetch=1,
            grid=(n_seq,),
            in_specs=in_specs,
            out_specs=new_spec,
            scratch_shapes=[pltpu.VMEM((n_pages, n_rows, page), F32),
                            pltpu.VMEM((ATT_WIDTH, LANES), F32),
                            pltpu.VMEM((nbuf, ATT_WIDTH, page), F32),
                            pltpu.VMEM((nbuf, ATT_WIDTH, page), F32),
                            pltpu.SemaphoreType.DMA((nbuf,)),
                            pltpu.SemaphoreType.DMA((nbuf,))]),
        out_shape=jax.ShapeDtypeStruct((n_seq, n_new, ATT_WIDTH), F32),
        compiler_params=pltpu.CompilerParams(dimension_semantics=("arbitrary",),
                                             vmem_limit_bytes=VMEM_LIMIT),
        name="attn_sample",
    )(page_table, rel_bias, q, k_new, v_new, near_s, cache_kt, cache_vt)


def _conv_offsets(halo, stride):
    off0 = halo - (CONV_KERNEL - 1) * stride
    return [off0 + j * stride for j in range(CONV_KERNEL)]


def _shift_classes(offs):
    return sorted({o % SUBLANES for o in offs} - {0})


def _mix_kernel(att_ref, glu_ref, x_ref, st_ref, cw_ref, cb_ref, lg_ref, lb_ref, wo_ref, gpm_ref, gpf_ref,
                x1_ref, hn_ref, gbuf, cvbuf, *shifted, tm, stride, halo, chunk):
    i = pl.program_id(1)
    group = min(tm, 128)

    @pl.when(i == 0)
    def _():
        gbuf[0:halo, :] = st_ref[0]

    gbuf[halo:halo + tm, :] = glu_ref[0]
    offs = _conv_offsets(halo, stride)
    for r, ref in zip(_shift_classes(offs), shifted):
        ref[...] = gbuf[r:r + ref.shape[0], :]
    src = dict(zip(_shift_classes(offs), shifted))
    src[0] = gbuf
    for c in range(tm // chunk):
        r0 = c * chunk
        acc = jnp.broadcast_to(cb_ref[...], (chunk, CONV_WIDTH))
        for j, o in enumerate(offs):
            a = o - o % SUBLANES + r0
            acc = acc + cw_ref[j:j + 1, :] * src[o % SUBLANES][a:a + chunk, :]
        mu = jnp.mean(acc, axis=-1, keepdims=True)
        d = acc - mu
        var = jnp.mean(d * d, axis=-1, keepdims=True)
        y = d * lax.rsqrt(var + EPS) * lg_ref[...] + lb_ref[...]
        cvbuf[r0:r0 + chunk, :] = (y * jax.nn.sigmoid(y)).astype(BF16)

        g1 = r0 + chunk
        if g1 % group == 0 or g1 == tm:
            g0 = (g1 - 1) // group * group
            mix = (jnp.dot(att_ref[0, g0:g1, :].astype(BF16), wo_ref[0:ATT_WIDTH, :], preferred_element_type=F32)
                   + jnp.dot(cvbuf[g0:g1, :], wo_ref[ATT_WIDTH:, :], preferred_element_type=F32))
            x1 = x_ref[0, g0:g1, :] + _rms(mix, gpm_ref[...])
            x1_ref[0, g0:g1, :] = x1
            hn_ref[0, g0:g1, :] = _rms(x1, gpf_ref[...]).astype(BF16)

    if tm >= halo:
        gbuf[0:halo, :] = gbuf[tm:tm + halo, :]


def _mix(att, glu, x, state, cw, cb, lg, lb, wo_bf, gpm, gpf, *, tm, stride):
    n_seq, rows, _ = x.shape
    halo = state.shape[1]
    assert rows == tm or tm >= halo
    offs = _conv_offsets(halo, stride)
    row_spec = lambda width: pl.BlockSpec((1, tm, width), lambda s, i: (s, i, 0))
    const = lambda shape: pl.BlockSpec(shape, lambda s, i: (0,) * len(shape))
    return pl.pallas_call(
        functools.partial(_mix_kernel, tm=tm, stride=stride, halo=halo, chunk=min(tm, 32)),
        grid=(n_seq, rows // tm),
        in_specs=[row_spec(ATT_WIDTH), row_spec(CONV_WIDTH), row_spec(D_MODEL),
                  pl.BlockSpec((1, halo, CONV_WIDTH), lambda s, i: (s, 0, 0)),
                  const((CONV_KERNEL, CONV_WIDTH)), const((1, CONV_WIDTH)), const((1, CONV_WIDTH)),
                  const((1, CONV_WIDTH)), const((D_MODEL, D_MODEL)), const((1, D_MODEL)), const((1, D_MODEL))],
        out_specs=[row_spec(D_MODEL), row_spec(D_MODEL)],
        out_shape=[jax.ShapeDtypeStruct((n_seq, rows, D_MODEL), F32),
                   jax.ShapeDtypeStruct((n_seq, rows, D_MODEL), BF16)],
        scratch_shapes=[pltpu.VMEM((halo + tm, CONV_WIDTH), F32), pltpu.VMEM((tm, CONV_WIDTH), BF16)] + [
            pltpu.VMEM((max(o - r for o in offs if o % SUBLANES == r) + tm, CONV_WIDTH), F32)
            for r in _shift_classes(offs)],
        compiler_params=pltpu.CompilerParams(dimension_semantics=("arbitrary", "arbitrary"),
                                             vmem_limit_bytes=VMEM_LIMIT),
        name="mix_s%d" % stride,
    )(att, glu, x, state, cw, cb, lg, lb, wo_bf, gpm, gpf)


def _ffn_kernel(hn_ref, x1_ref, wg_ref, wv_ref, wd_ref, dwg_ref, dwv_ref, dbg_ref, dbv_ref, stg_ref, stv_ref,
                gpo_ref, y_ref, tg_ref, tv_ref, ubuf_g, ubuf_v, car_g, car_v, acc_ref, *, tm, stride, halo, nc):
    i = pl.program_id(1)
    c = pl.program_id(2)
    hn = hn_ref[0]

    @pl.when(i == 0)
    def _():
        ubuf_g[0:halo, :] = stg_ref[0]
        ubuf_v[0:halo, :] = stv_ref[0]

    @pl.when(i > 0)
    def _():
        ubuf_g[0:halo, :] = car_g[c]
        ubuf_v[0:halo, :] = car_v[c]

    @pl.when(c == 0)
    def _():
        acc_ref[...] = jnp.zeros(acc_ref.shape, F32)

    def up(c0, c1):
        ubuf_g[halo:halo + tm, c0:c1] = jnp.dot(hn, wg_ref[:, c0:c1], preferred_element_type=F32)
        ubuf_v[halo:halo + tm, c0:c1] = jnp.dot(hn, wv_ref[:, c0:c1], preferred_element_type=F32)

    def conv(c0, c1, dw_ref, db_ref, ubuf):
        return (dw_ref[0:1, c0:c1] * ubuf[halo - 2 * stride:halo - 2 * stride + tm, c0:c1]
                + dw_ref[1:2, c0:c1] * ubuf[halo - stride:halo - stride + tm, c0:c1]
                + dw_ref[2:3, c0:c1] * ubuf[halo:halo + tm, c0:c1] + db_ref[:, c0:c1])

    ck = wg_ref.shape[1]
    bounds = [(c0, min(c0 + MXU_COLS, ck)) for c0 in range(0, ck, MXU_COLS)]
    def down(c0, c1, a):
        acc_ref[...] += jnp.dot(a, wd_ref[c0:c1, :], preferred_element_type=F32)

    up(*bounds[0])
    pending = None
    for n, (c0, c1) in enumerate(bounds):
        if n + 1 < len(bounds):
            up(*bounds[n + 1])
        if pending is not None:
            down(*pending)
        a = jax.nn.gelu(conv(c0, c1, dwg_ref, dbg_ref, ubuf_g), approximate=True) * conv(c0, c1, dwv_ref, dbv_ref,
                                                                                        ubuf_v)
        pending = (c0, c1, a.astype(BF16))
    down(*pending)

    for ubuf, car, tail_ref in ((ubuf_g, car_g, tg_ref), (ubuf_v, car_v, tv_ref)):
        last = ubuf[tm:tm + halo, :]
        car[c] = last
        tail_ref[0, 0] = last

    @pl.when(c == nc - 1)
    def _():
        y_ref[0] = x1_ref[0] + _rms(acc_ref[...], gpo_ref[...])


def _ffn(hn, x1, wup_bf, wdn_bf, dw, db, state, gpo, *, tm, stride, nc):
    n_seq, rows, _ = x1.shape
    halo = state.shape[1]
    ck = D_FF // nc
    row_spec = lambda width: pl.BlockSpec((1, tm, width), lambda s, i, c: (s, i, 0))
    mode = dict(pipeline_mode=pl.Buffered(1)) if nc == 1 else {}
    gate_cols = lambda shape, **kw: pl.BlockSpec(shape, lambda s, i, c: (0, c), **kw)
    val_cols = lambda shape, **kw: pl.BlockSpec(shape, lambda s, i, c: (0, c + nc), **kw)
    return pl.pallas_call(
        functools.partial(_ffn_kernel, tm=tm, stride=stride, halo=halo, nc=nc),
        grid=(n_seq, rows // tm, nc),
        in_specs=[row_spec(D_MODEL), row_spec(D_MODEL),
                  gate_cols((D_MODEL, ck), **mode), val_cols((D_MODEL, ck), **mode),
                  pl.BlockSpec((ck, D_MODEL), lambda s, i, c: (c, 0), **mode),
                  gate_cols((FFN_KERNEL, ck)), val_cols((FFN_KERNEL, ck)),
                  gate_cols((1, ck)), val_cols((1, ck)),
                  pl.BlockSpec((1, halo, ck), lambda s, i, c: (s, 0, c)),
                  pl.BlockSpec((1, halo, ck), lambda s, i, c: (s, 0, c + nc)),
                  pl.BlockSpec((1, D_MODEL), lambda s, i, c: (0, 0))],
        out_specs=[row_spec(D_MODEL),
                   pl.BlockSpec((1, 1, halo, ck), lambda s, i, c: (s, i, 0, c)),
                   pl.BlockSpec((1, 1, halo, ck), lambda s, i, c: (s, i, 0, c))],
        out_shape=[jax.ShapeDtypeStruct((n_seq, rows, D_MODEL), F32),
                   jax.ShapeDtypeStruct((n_seq, rows // tm, halo, D_FF), F32),
                   jax.ShapeDtypeStruct((n_seq, rows // tm, halo, D_FF), F32)],
        scratch_shapes=[pltpu.VMEM((halo + tm, ck), F32), pltpu.VMEM((halo + tm, ck), F32),
                        pltpu.VMEM((nc, halo, ck), F32), pltpu.VMEM((nc, halo, ck), F32),
                        pltpu.VMEM((tm, D_MODEL), F32)],
        compiler_params=pltpu.CompilerParams(dimension_semantics=("arbitrary",) * 3,
                                             vmem_limit_bytes=VMEM_LIMIT),
        name="ffn_s%d" % stride,
    )(hn, x1, wup_bf, wup_bf, wdn_bf, dw, dw, db, db, state, state, gpo)


def _tile(rows, want):
    return want if rows % want == 0 else rows


def kernel(x_prompt, x_sample, cache_k, cache_v, state_conv, state_ffn, page_table, rel_bias, g_pre_mix, w_in,
           conv_dw_w, conv_dw_b, conv_ln_g, conv_ln_b, w_out, g_post_mix, g_pre_ffn, w_ffn_up, ffn_dw_w,
           ffn_dw_b, w_ffn_down, g_post_ffn):
    depth = w_in.shape[0]
    assert depth == 1, "single-layer trunk"
    bp, seq, _ = x_prompt.shape
    bs, n_new, _ = x_sample.shape
    n_pool, page = cache_k.shape[1], cache_k.shape[2]
    n_pages = page_table.shape[1]
    assert seq % MOBA_BLOCK == 0 and page == LANES and (n_pages * page) % MOBA_BLOCK == 0
    assert n_new == SUBLANES and n_pages * page // MOBA_BLOCK <= LANES

    w_in_bf = w_in[0].astype(BF16)
    w_out_bf = w_out[0].astype(BF16)
    w_up_bf = w_ffn_up[0].astype(BF16)
    w_dn_bf = w_ffn_down[0].astype(BF16)
    cw, cb = conv_dw_w[0], conv_dw_b
    near_t, near_s = _bias_tiles(rel_bias)
    conv_halo_p = 32
    ffn_halo_p = SUBLANES

    tm_p = _tile(seq, 512)
    qt, kt, vt, kbf, vtb, kmean, glu_p = _proj(x_prompt, g_pre_mix, w_in_bf, prompt=True, tm=tm_p)
    att_p = _attn_prompt(qt, kbf, vtb, kmean.reshape(bp, seq // MOBA_BLOCK, ATT_WIDTH), near_t)
    x1_p, hn_p = _mix(att_p, glu_p, x_prompt, jnp.zeros((bp, conv_halo_p, CONV_WIDTH), F32), cw, cb,
                      conv_ln_g, conv_ln_b, w_out_bf, g_post_mix, g_pre_ffn, tm=tm_p, stride=1)
    y_prompt, tail_g, tail_v = _ffn(hn_p, x1_p, w_up_bf, w_dn_bf, ffn_dw_w[0], ffn_dw_b,
                                    jnp.zeros((bp, ffn_halo_p, 2 * D_FF), F32), g_post_ffn,
                                    tm=tm_p, stride=1, nc=1)
    k_prompt = kt.reshape(1, bp, N_HEADS, HEAD_DIM, seq).transpose(0, 1, 4, 2, 3)
    v_prompt = vt.reshape(1, bp, N_HEADS, HEAD_DIM, seq).transpose(0, 1, 4, 2, 3)
    conv_prompt = glu_p[:, seq - (CONV_KERNEL - 1):, :][None]
    ffn_prompt = jnp.concatenate([tail_g[:, -1], tail_v[:, -1]],
                                 axis=-1)[:, ffn_halo_p - (FFN_KERNEL - 1):, :][None]

    rows_s = n_new * bs
    to_tb = lambda a: a.transpose(1, 0, 2).reshape(1, rows_s, a.shape[-1])
    to_bt = lambda a: a.reshape(n_new, bs, a.shape[-1]).transpose(1, 0, 2)
    xs = to_tb(x_sample)
    q_s, k_s, v_s, glu_s = _proj(xs, g_pre_mix, w_in_bf, prompt=False, tm=rows_s)
    q_b, k_b, v_b = to_bt(q_s), to_bt(k_s), to_bt(v_s)
    cache_kt = cache_k[0].transpose(0, 2, 3, 1).reshape(n_pool, ATT_WIDTH, page)
    cache_vt = cache_v[0].transpose(0, 2, 3, 1).reshape(n_pool, ATT_WIDTH, page)
    att_b = _attn_sample(page_table, rel_bias, q_b, k_b, v_b, near_s, cache_kt, cache_vt,
                         nbuf=min(32, n_pages))
    conv_state = state_conv[0].transpose(1, 0, 2).reshape(1, (CONV_KERNEL - 1) * bs, CONV_WIDTH)
    x1_s, hn_s = _mix(to_tb(att_b), glu_s, xs, conv_state, cw, cb, conv_ln_g, conv_ln_b, w_out_bf,
                      g_post_mix, g_pre_ffn, tm=rows_s, stride=bs)
    ffn_state = state_ffn[0].transpose(1, 0, 2).reshape(1, (FFN_KERNEL - 1) * bs, 2 * D_FF)
    y_s, tail_gs, tail_vs = _ffn(hn_s, x1_s, w_up_bf, w_dn_bf, ffn_dw_w[0], ffn_dw_b, ffn_state, g_post_ffn,
                                 tm=rows_s, stride=bs, nc=2)
    y_sample = to_bt(y_s)
    k_sample = k_b.reshape(1, bs, n_new, N_HEADS, HEAD_DIM)
    v_sample = v_b.reshape(1, bs, n_new, N_HEADS, HEAD_DIM)
    conv_all = jnp.concatenate([conv_state[0], glu_s[0]], axis=0)[n_new * bs:]
    conv_sample = conv_all.reshape(CONV_KERNEL - 1, bs, CONV_WIDTH).transpose(1, 0, 2)[None]
    ffn_sample = jnp.concatenate([tail_gs[:, -1], tail_vs[:, -1]], axis=-1).reshape(
        FFN_KERNEL - 1, bs, 2 * D_FF).transpose(1, 0, 2)[None]

    return (y_prompt, y_sample, k_prompt, v_prompt, conv_prompt, ffn_prompt,
            k_sample, v_sample, conv_sample, ffn_sample)
```

```python
import functools
import math

import numpy as np
import jax
import jax.numpy as jnp
from jax import lax
from jax.experimental import pallas as pl
from jax.experimental.pallas import tpu as pltpu

F32 = jnp.float32
BF16 = jnp.bfloat16

D_MODEL = 1024
HEAD_DIM = 64
ATT_WIDTH = 512
N_HEADS = 8
N_PAIRS = N_HEADS // 2
CONV_WIDTH = 512
CONV_KERNEL = 31
MOBA_BLOCK = 256
MOBA_TOPK = 3
N_BUCKETS = 32
MAX_DISTANCE = 128
D_FF = 2816
FFN_KERNEL = 3
EPS = 1e-6
SCALE = HEAD_DIM ** -0.5
LOG2E = math.log2(math.e)
NEG = -1e30
LANES = 128
SUBLANES = 8
VMEM_LIMIT = 56 * 1024 * 1024
DOWN_GROUP = 6
MXU_COLS = 256
ONES_ROWS = 16
V_ROWS = HEAD_DIM + ONES_ROWS


def _bucket_thresholds():
    n = np.arange(0, 4 * MAX_DISTANCE)
    max_exact = N_BUCKETS // 2
    nf = np.maximum(n, 1).astype(np.float64)
    large = max_exact + (np.log(nf / max_exact) / math.log(MAX_DISTANCE / max_exact)
                         * (N_BUCKETS - max_exact)).astype(np.int64)
    bucket = np.where(n < max_exact, n, np.minimum(large, N_BUCKETS - 1))
    return tuple(int(np.argmax(bucket >= k)) for k in range(1, N_BUCKETS))


_BUCKET_THR = _bucket_thresholds()


def _rms(x, g):
    return x * lax.rsqrt(jnp.mean(x * x, axis=-1, keepdims=True) + EPS) * g


def _bias_kernel(tbl_ref, near_t_ref, near_s_ref):
    p = pl.program_id(0)

    def bias_of(dist, h):
        b = jnp.full(dist.shape, tbl_ref[0, h], F32)
        for k in range(1, N_BUCKETS):
            b = jnp.where(dist >= _BUCKET_THR[k - 1], tbl_ref[k, h], b)
        return jnp.where(dist >= 0, b, NEG)

    jj = lax.broadcasted_iota(jnp.int32, (2 * MOBA_BLOCK, MOBA_BLOCK), 0)
    ii = lax.broadcasted_iota(jnp.int32, (2 * MOBA_BLOCK, MOBA_BLOCK), 1)
    dist = MOBA_BLOCK + ii - jj
    for half in range(2):
        h = 2 * p + half
        near_t_ref[0, :, half * MOBA_BLOCK:(half + 1) * MOBA_BLOCK] = (
            bias_of(dist, h) - tbl_ref[N_BUCKETS - 1, h]) * LOG2E

    @pl.when(p == 0)
    def _():
        tt = lax.broadcasted_iota(jnp.int32, (SUBLANES, 2 * MOBA_BLOCK), 0)
        j2 = lax.broadcasted_iota(jnp.int32, (SUBLANES, 2 * MOBA_BLOCK), 1)
        d2 = MOBA_BLOCK + tt - j2
        for h in range(N_HEADS):
            near_s_ref[h * SUBLANES:(h + 1) * SUBLANES, :] = bias_of(d2, h)


def _bias_tiles(rel_bias):
    return pl.pallas_call(
        _bias_kernel,
        grid=(N_PAIRS,),
        in_specs=[pl.BlockSpec(memory_space=pltpu.SMEM)],
        out_specs=[pl.BlockSpec((1, 2 * MOBA_BLOCK, 2 * MOBA_BLOCK), lambda p: (p, 0, 0)),
                   pl.BlockSpec((N_HEADS * SUBLANES, 2 * MOBA_BLOCK), lambda p: (0, 0))],
        out_shape=[jax.ShapeDtypeStruct((N_PAIRS, 2 * MOBA_BLOCK, 2 * MOBA_BLOCK), F32),
                   jax.ShapeDtypeStruct((N_HEADS * SUBLANES, 2 * MOBA_BLOCK), F32)],
        compiler_params=pltpu.CompilerParams(dimension_semantics=("arbitrary",)),
        name="bias_tiles",
    )(rel_bias)


def _proj_kernel(x_ref, g_ref, w_ref, *out_refs, prompt, tm):
    h = _rms(x_ref[0], g_ref[...])
    p = jnp.dot(h.astype(BF16), w_ref[...], preferred_element_type=F32)
    q = p[:, 0:ATT_WIDTH]
    k = p[:, ATT_WIDTH:2 * ATT_WIDTH]
    v = p[:, 2 * ATT_WIDTH:3 * ATT_WIDTH]
    ga = p[:, 3 * ATT_WIDTH:3 * ATT_WIDTH + CONV_WIDTH]
    gb = p[:, 3 * ATT_WIDTH + CONV_WIDTH:]
    glu = ga * jax.nn.sigmoid(gb)
    if prompt:
        qt_ref, kt_ref, vt_ref, kbf_ref, vtb_ref, km_ref, glu_ref = out_refs
        qt_ref[0] = q.T
        kt_ref[0] = k.T
        vt = v.T
        vt_ref[0] = vt
        row_blk = (pl.program_id(1) * (tm // MOBA_BLOCK)
                   + lax.broadcasted_iota(jnp.int32, (tm, LANES), 0) // MOBA_BLOCK)
        onehot = jnp.where(lax.broadcasted_iota(jnp.int32, (tm, LANES), 1) == row_blk, 1.0, 0.0).astype(BF16)
        kb = k.astype(BF16)
        for pr in range(N_PAIRS):
            kbf_ref[0, :, 2 * pr * LANES:(2 * pr + 1) * LANES] = kb[:, pr * LANES:(pr + 1) * LANES]
            kbf_ref[0, :, (2 * pr + 1) * LANES:(2 * pr + 2) * LANES] = onehot
        ones = jnp.ones((ONES_ROWS, MOBA_BLOCK), F32)
        for c in range(tm // MOBA_BLOCK):
            vc = vt[:, c * MOBA_BLOCK:(c + 1) * MOBA_BLOCK]
            parts = []
            for hd in range(N_HEADS):
                parts += [vc[hd * HEAD_DIM:(hd + 1) * HEAD_DIM], ones]
            vtb_ref[0, c] = jnp.concatenate(parts, axis=0).astype(BF16)
        km_ref[0] = jnp.mean(k.reshape(tm // MOBA_BLOCK, MOBA_BLOCK, ATT_WIDTH), axis=1)[:, None, :]
        glu_ref[0] = glu
    else:
        q_ref, k_ref, v_ref, glu_ref = out_refs
        q_ref[0] = q
        k_ref[0] = k
        v_ref[0] = v
        glu_ref[0] = glu


def _proj(x, g, w_bf, *, prompt, tm):
    n_seq, rows, _ = x.shape
    n_out = w_bf.shape[1]
    grid = (n_seq, rows // tm)
    row_spec = lambda width: pl.BlockSpec((1, tm, width), lambda s, i: (s, i, 0))
    col_spec = pl.BlockSpec((1, ATT_WIDTH, tm), lambda s, i: (s, 0, i))
    if prompt:
        nb = tm // MOBA_BLOCK
        out_specs = [col_spec, col_spec, col_spec, row_spec(2 * ATT_WIDTH),
                     pl.BlockSpec((1, nb, N_HEADS * V_ROWS, MOBA_BLOCK), lambda s, i: (s, i, 0, 0)),
                     pl.BlockSpec((1, nb, 1, ATT_WIDTH), lambda s, i: (s, i, 0, 0)),
                     row_spec(CONV_WIDTH)]
        t_shape = jax.ShapeDtypeStruct((n_seq, ATT_WIDTH, rows), F32)
        out_shape = [t_shape, t_shape, t_shape,
                     jax.ShapeDtypeStruct((n_seq, rows, 2 * ATT_WIDTH), BF16),
                     jax.ShapeDtypeStruct((n_seq, rows // MOBA_BLOCK, N_HEADS * V_ROWS, MOBA_BLOCK), BF16),
                     jax.ShapeDtypeStruct((n_seq, rows // MOBA_BLOCK, 1, ATT_WIDTH), F32),
                     jax.ShapeDtypeStruct((n_seq, rows, CONV_WIDTH), F32)]
    else:
        out_specs = [row_spec(ATT_WIDTH)] * 3 + [row_spec(CONV_WIDTH)]
        out_shape = [jax.ShapeDtypeStruct((n_seq, rows, ATT_WIDTH), F32)] * 3 + [
            jax.ShapeDtypeStruct((n_seq, rows, CONV_WIDTH), F32)]
    return pl.pallas_call(
        functools.partial(_proj_kernel, prompt=prompt, tm=tm),
        grid=grid,
        in_specs=[row_spec(D_MODEL),
                  pl.BlockSpec((1, D_MODEL), lambda s, i: (0, 0)),
                  pl.BlockSpec((D_MODEL, n_out), lambda s, i: (0, 0))],
        out_specs=out_specs,
        out_shape=out_shape,
        compiler_params=pltpu.CompilerParams(dimension_semantics=("arbitrary", "arbitrary"),
                                             vmem_limit_bytes=VMEM_LIMIT),
        name="proj_prompt" if prompt else "proj_sample",
    )(x, g, w_bf)


def _select_topk(gate, idx, axis, n_valid):
    big = jnp.int32(2 ** 30)
    g = jnp.where(idx < n_valid, gate, -jnp.inf)
    sel = jnp.zeros(gate.shape, jnp.bool_)
    for _ in range(MOBA_TOPK):
        mx = jnp.max(g, axis=axis, keepdims=True)
        first = jnp.min(jnp.where(g == mx, idx, big), axis=axis, keepdims=True)
        hit = idx == first
        sel = sel | (hit & (mx > -jnp.inf))
        g = jnp.where(hit, -jnp.inf, g)
    return sel


def _attn_stream(i, nblk, s, qt_ref, qtn_ref, k_ref, vt_ref, km_ref, near_ref, o_ref, qaug_ref, mnext_ref,
                 sa_ref, sb_ref, pa_ref, pb_ref, acc_ref):
    blk = MOBA_BLOCK

    def block_diag(qt):
        z = jnp.zeros((HEAD_DIM, blk), F32)
        return jnp.concatenate([jnp.concatenate([qt[:HEAD_DIM], z], axis=1),
                                jnp.concatenate([z, qt[HEAD_DIM:]], axis=1)], axis=0)

    def mask_rows(q2t, tile):
        gate = jnp.dot(km_ref[0], q2t, precision=lax.Precision.HIGHEST, preferred_element_type=F32)
        n_idx = lax.broadcasted_iota(jnp.int32, (nblk, 2 * blk), 0)
        sel = _select_topk(gate, n_idx, 0, tile)
        return jnp.concatenate([jnp.where(sel | (n_idx == tile), 0.0, NEG),
                                jnp.full((LANES - nblk, 2 * blk), NEG, F32)], axis=0).astype(BF16)

    def scores_into(dst_ref, j0, j1, bias):
        mx = None
        for half, j in enumerate((j0, j1)):
            kb = k_ref[0, pl.ds(pl.multiple_of(j * blk, blk), blk), :]
            s = jnp.dot(kb, qaug_ref[...], preferred_element_type=F32)
            if bias is not None:
                s = s + bias[half]
            dst_ref[half] = s
            smax = jnp.max(s, axis=0, keepdims=True)
            mx = smax if mx is None else jnp.maximum(mx, smax)
        return mx

    def probs(src_ref, dst_ref, m, mx):
        m_new = jnp.maximum(m, mx)
        dst_ref[0] = jnp.exp2(src_ref[0] - m_new).astype(BF16)
        dst_ref[1] = jnp.exp2(src_ref[1] - m_new).astype(BF16)
        return m_new, jnp.exp2(m - m_new)

    def accumulate(p_ref, j0, j1, alpha):
        p0, p1 = p_ref[0], p_ref[1]
        v0, v1 = vt_ref[0, j0], vt_ref[0, j1]
        d_a = (jnp.dot(v0[:V_ROWS], p0[:, :blk], preferred_element_type=F32)
               + jnp.dot(v1[:V_ROWS], p1[:, :blk], preferred_element_type=F32))
        d_b = (jnp.dot(v0[V_ROWS:], p0[:, blk:], preferred_element_type=F32)
               + jnp.dot(v1[V_ROWS:], p1[:, blk:], preferred_element_type=F32))
        acc_ref[0] = acc_ref[0] * alpha[:, :blk] + d_a
        acc_ref[1] = acc_ref[1] * alpha[:, blk:] + d_b

    none = nblk - 1
    prev = jnp.where(i > 0, i - 1, none)
    n_far = jnp.maximum(i - 1, 0)

    def prologue():
        qaug_ref[0:LANES, :] = (block_diag(qt_ref[0]) * (SCALE * LOG2E)).astype(BF16)
        row = lax.broadcasted_iota(jnp.int32, (LANES, 2 * blk), 0)
        qaug_ref[LANES:, :] = jnp.where((i == 0) & (row == 0), 0.0,
                                        jnp.where(i == 0, NEG, mnext_ref[...].astype(F32))).astype(BF16)

        mx = scores_into(sa_ref, i, prev, (near_ref[0, blk:, :], near_ref[0, :blk, :]))
        mnext_ref[...] = mask_rows(block_diag(qtn_ref[0]), i + 1)
        pb_ref[...] = jnp.zeros(pb_ref.shape, BF16)
        acc_ref[...] = jnp.zeros(acc_ref.shape, F32)
        return jnp.full((1, 2 * blk), NEG, F32), mx, jnp.ones((1, 2 * blk), F32), i, prev, i, i

    def stage(even, t, state):
        s_src, s_dst, p_dst, p_src = (sa_ref, sb_ref, pa_ref, pb_ref) if even else (sb_ref, sa_ref, pb_ref, pa_ref)
        m, mx, alpha_p, js0, js1, jp0, jp1 = state
        j0 = 2 * t
        j1 = jnp.where(j0 + 1 < n_far, j0 + 1, none)
        mx_next = scores_into(s_dst, j0, j1, None)
        m_new, alpha = probs(s_src, p_dst, m, mx)
        accumulate(p_src, jp0, jp1, alpha_p)
        return m_new, mx_next, alpha, j0, j1, js0, js1

    def finish(even, state):
        s_src, p_dst, p_src = (sa_ref, pa_ref, pb_ref) if even else (sb_ref, pb_ref, pa_ref)
        m, mx, alpha_p, js0, js1, jp0, jp1 = state
        accumulate(p_src, jp0, jp1, alpha_p)
        _, alpha = probs(s_src, p_dst, m, mx)
        accumulate(p_dst, js0, js1, alpha)
        a_a, a_b = acc_ref[0], acc_ref[1]
        out_t = jnp.concatenate([a_a[:HEAD_DIM] / a_a[HEAD_DIM:HEAD_DIM + 1],
                                 a_b[:HEAD_DIM] / a_b[HEAD_DIM:HEAD_DIM + 1]], axis=0)
        o_ref[0, :, s * LANES:(s + 1) * LANES] = out_t.T

    return prologue, stage, finish


ATTN_STREAMS = 4
_STREAM_INPUTS = 6
_STREAM_SCRATCH = 7
_MNEXT_SLOT = 1


def _attn_prompt_kernel(*refs, nblk):
    i = pl.program_id(2)
    n_in = ATTN_STREAMS * _STREAM_INPUTS
    ins, o_ref, scr = refs[:n_in], refs[n_in], refs[n_in + 1:]
    streams = [_attn_stream(i, nblk, s, *ins[s * _STREAM_INPUTS:(s + 1) * _STREAM_INPUTS], o_ref,
                            *scr[s * _STREAM_SCRATCH:(s + 1) * _STREAM_SCRATCH]) for s in range(ATTN_STREAMS)]
    n_pairs = jnp.maximum(i, 1) // 2

    @pl.when((pl.program_id(0) == 0) & (pl.program_id(1) == 0) & (i == 0))
    def _():
        for s in range(ATTN_STREAMS):
            mnext = scr[s * _STREAM_SCRATCH + _MNEXT_SLOT]
            mnext[...] = jnp.zeros(mnext.shape, BF16)

    states = tuple(prologue() for prologue, _, _ in streams)

    def step(t, states):
        def run(even):
            return lambda sts: tuple(stage(even, t, st) for (_, stage, _), st in zip(streams, sts))
        return lax.cond(t % 2 == 0, run(True), run(False), states)

    states = lax.fori_loop(0, n_pairs, step, states)
    for even in (True, False):
        @pl.when(n_pairs % 2 == (0 if even else 1))
        def _():
            for (_, _, finish), st in zip(streams, states):
                finish(even, st)


def _attn_prompt(qt, kaug, vtb, kmean, near_t):
    n_seq, _, t = qt.shape
    nblk = t // MOBA_BLOCK
    assert nblk % 2 == 0 and nblk < LANES and N_PAIRS % ATTN_STREAMS == 0
    ns = ATTN_STREAMS
    once = dict(pipeline_mode=pl.Buffered(1))

    def stream_specs(s):
        pair = lambda g: ns * g + s
        return [pl.BlockSpec((1, LANES, MOBA_BLOCK), lambda b, g, i: (b, pair(g), i)),
                pl.BlockSpec((1, LANES, MOBA_BLOCK), lambda b, g, i: (b, pair(g), jnp.minimum(i + 1, nblk - 1))),
                pl.BlockSpec((1, t, 2 * LANES), lambda b, g, i: (b, 0, pair(g)), **once),
                pl.BlockSpec((1, nblk, 2 * V_ROWS, MOBA_BLOCK), lambda b, g, i: (b, 0, pair(g), 0), **once),
                pl.BlockSpec((1, nblk, LANES), lambda b, g, i: (b, 0, pair(g))),
                pl.BlockSpec((1, 2 * MOBA_BLOCK, 2 * MOBA_BLOCK), lambda b, g, i: (pair(g), 0, 0), **once)]

    stream_scratch = [pltpu.VMEM((2 * LANES, 2 * MOBA_BLOCK), BF16),
                      pltpu.VMEM((LANES, 2 * MOBA_BLOCK), BF16),
                      pltpu.VMEM((2, MOBA_BLOCK, 2 * MOBA_BLOCK), F32),
                      pltpu.VMEM((2, MOBA_BLOCK, 2 * MOBA_BLOCK), F32),
                      pltpu.VMEM((2, MOBA_BLOCK, 2 * MOBA_BLOCK), BF16),
                      pltpu.VMEM((2, MOBA_BLOCK, 2 * MOBA_BLOCK), BF16),
                      pltpu.VMEM((2, V_ROWS, MOBA_BLOCK), F32)]
    assert len(stream_specs(0)) == _STREAM_INPUTS and len(stream_scratch) == _STREAM_SCRATCH
    return pl.pallas_call(
        functools.partial(_attn_prompt_kernel, nblk=nblk),
        grid=(n_seq, N_PAIRS // ns, nblk),
        in_specs=[spec for s in range(ns) for spec in stream_specs(s)],
        out_specs=pl.BlockSpec((1, MOBA_BLOCK, ns * LANES), lambda b, g, i: (b, i, g)),
        out_shape=jax.ShapeDtypeStruct((n_seq, t, ATT_WIDTH), F32),
        scratch_shapes=stream_scratch * ns,
        compiler_params=pltpu.CompilerParams(dimension_semantics=("arbitrary",) * 3,
                                             vmem_limit_bytes=VMEM_LIMIT),
        name="attn_prompt",
    )(*([qt, qt, kaug, vtb, kmean, near_t] * ns))


def _attn_sample_kernel(pt_ref, tbl_ref, q_ref, kn_ref, vn_ref, near_ref, ck_hbm, cv_hbm, o_ref,
                        s_all, km_t, kbuf, vbuf, ksem, vsem, *, n_pages, nbuf, step, n_new):
    b = pl.program_id(0)
    n_seq = pl.num_programs(0)
    n_rows = N_HEADS * n_new
    page = LANES
    nblk = n_pages * page // MOBA_BLOCK

    def k_copy(seq, p, slot):
        return pltpu.make_async_copy(ck_hbm.at[pt_ref[seq, p]], kbuf.at[slot], ksem.at[slot])

    def v_copy(p, slot):
        return pltpu.make_async_copy(cv_hbm.at[pt_ref[b, p]], vbuf.at[slot], vsem.at[slot])

    @pl.when(b == 0)
    def _():
        for p in range(nbuf):
            k_copy(0, p, p).start()

    row_head = lax.broadcasted_iota(jnp.int32, (n_rows, ATT_WIDTH), 0) // n_new
    col_head = lax.broadcasted_iota(jnp.int32, (n_rows, ATT_WIDTH), 1) // HEAD_DIM
    diag = row_head == col_head
    q = q_ref[0]
    qbd_f = jnp.where(diag, jnp.concatenate([q] * N_HEADS, axis=0), 0.0)
    qbd_b = (qbd_f * SCALE).astype(BF16)
    km_t[...] = jnp.zeros(km_t.shape, F32)
    lane_k = lax.broadcasted_iota(jnp.int32, (ATT_WIDTH, LANES), 1)

    def k_step(it, _):
        p0 = it * step
        slot0 = p0 % nbuf
        for r in range(step):
            k_copy(b, p0 + r, slot0 + r).wait()
        km = km_t[...]
        for r in range(0, step, 2):
            kt0 = kbuf[slot0 + r]
            kt1 = kbuf[slot0 + r + 1]
            s_all[p0 + r] = jnp.dot(qbd_b, kt0.astype(BF16), preferred_element_type=F32)
            s_all[p0 + r + 1] = jnp.dot(qbd_b, kt1.astype(BF16), preferred_element_type=F32)
            mean = jnp.sum(kt0 + kt1, axis=1, keepdims=True) * (1.0 / MOBA_BLOCK)
            km = jnp.where(lane_k == (p0 + r) // 2, mean, km)
        km_t[...] = km

        @pl.when(p0 + nbuf < n_pages)
        def _():
            for r in range(step):
                k_copy(b, p0 + nbuf + r, slot0 + r).start()

        return 0

    lax.fori_loop(0, n_pages // step, k_step, 0)

    @pl.when(b + 1 < n_seq)
    def _():
        for p in range(nbuf):
            k_copy(b + 1, p, p).start()

    for p in range(nbuf):
        v_copy(p, p).start()

    gate = jnp.dot(qbd_f, km_t[...], precision=lax.Precision.HIGHEST,
                   preferred_element_type=F32)
    lane = lax.broadcasted_iota(jnp.int32, (n_rows, LANES), 1)
    sel = _select_topk(gate, lane, 1, nblk)
    rh = lax.broadcasted_iota(jnp.int32, (n_rows, 1), 0) // n_new
    far = jnp.zeros((n_rows, 1), F32)
    for h in range(N_HEADS):
        far = jnp.where(rh == h, tbl_ref[N_BUCKETS - 1, h], far)
    mask = jnp.where(sel, far, NEG)
    near = near_ref[...]
    for n in range(nblk):
        if n == nblk - 1:
            seln = jnp.broadcast_to(mask[:, n:n + 1], (n_rows, page)) > 0.5 * NEG
            s_all[2 * n] = s_all[2 * n] + jnp.where(seln, near[:, 0:page], NEG)
            s_all[2 * n + 1] = s_all[2 * n + 1] + jnp.where(seln, near[:, page:2 * page], NEG)
        else:
            add = jnp.broadcast_to(mask[:, n:n + 1], (n_rows, page))
            s_all[2 * n] = s_all[2 * n] + add
            s_all[2 * n + 1] = s_all[2 * n + 1] + add
    kn = jnp.concatenate([kn_ref[0], jnp.zeros((page - n_new, ATT_WIDTH), F32)], axis=0).astype(BF16)
    s_own = lax.dot_general(qbd_b, kn, (((1,), (1,)), ((), ())),
                            preferred_element_type=F32) + near[:, 2 * page:3 * page]
    mrun = lax.fori_loop(0, n_pages, lambda j, mm: jnp.maximum(mm, s_all[j]), s_own)
    m = jnp.max(mrun, axis=1, keepdims=True)
    p_own = jnp.exp(s_own - m)

    def exp_step(j, lsum):
        pj = jnp.exp(s_all[j] - m)
        s_all[j] = pj
        return lsum + pj

    lsum = lax.fori_loop(0, n_pages, exp_step, p_own)
    l = jnp.sum(lsum, axis=1, keepdims=True)
    vn = jnp.concatenate([vn_ref[0], jnp.zeros((page - n_new, ATT_WIDTH), F32)], axis=0).astype(BF16)
    acc = jnp.dot(p_own.astype(BF16), vn, preferred_element_type=F32)

    def v_step(it, acc):
        p0 = it * step
        slot0 = p0 % nbuf
        for r in range(step):
            v_copy(p0 + r, slot0 + r).wait()
        parts = []
        for r in range(step):
            pb = s_all[p0 + r].astype(BF16)
            vt = vbuf[slot0 + r].astype(BF16)
            parts.append(lax.dot_general(pb, vt, (((1,), (1,)), ((), ())), preferred_element_type=F32))
        while len(parts) > 1:
            parts = [x + y for x, y in zip(parts[0::2], parts[1::2])]

        @pl.when(p0 + nbuf < n_pages)
        def _():
            for r in range(step):
                v_copy(p0 + nbuf + r, slot0 + r).start()

        return acc + parts[0]

    acc = lax.fori_loop(0, n_pages // step, v_step, acc)
    o = jnp.where(diag, acc / l, 0.0)
    o_ref[0] = jnp.sum(o.reshape(N_HEADS, n_new, ATT_WIDTH), axis=0)


def _attn_sample(page_table, rel_bias, q, k_new, v_new, near_s, cache_kt, cache_vt, *, nbuf):
    n_seq, n_new, _ = q.shape
    n_pages = page_table.shape[1]
    n_rows = N_HEADS * n_new
    page = cache_kt.shape[2]
    step = min(8, nbuf)
    assert step & (step - 1) == 0 and step >= 2 and nbuf % step == 0 and n_pages % nbuf == 0
    new_spec = pl.BlockSpec((1, n_new, ATT_WIDTH), lambda b, pt: (b, 0, 0))
    in_specs = [pl.BlockSpec(memory_space=pltpu.SMEM), new_spec, new_spec, new_spec,
                pl.BlockSpec((n_rows, 2 * MOBA_BLOCK), lambda b, pt: (0, 0)),
                pl.BlockSpec(memory_space=pl.ANY), pl.BlockSpec(memory_space=pl.ANY)]
    return pl.pallas_call(
        functools.partial(_attn_sample_kernel, n_pages=n_pages, nbuf=nbuf, step=step, n_new=n_new),
        grid_spec=pltpu.PrefetchScalarGridSpec(
            num_scalar_prefetch=1,
            grid=(n_seq,),
            in_specs=in_specs,
            out_specs=new_spec,
            scratch_shapes=[pltpu.VMEM((n_pages, n_rows, page), F32),
                            pltpu.VMEM((ATT_WIDTH, LANES), F32),
                            pltpu.VMEM((nbuf, ATT_WIDTH, page), F32),
                            pltpu.VMEM((nbuf, ATT_WIDTH, page), F32),
                            pltpu.SemaphoreType.DMA((nbuf,)),
                            pltpu.SemaphoreType.DMA((nbuf,))]),
        out_shape=jax.ShapeDtypeStruct((n_seq, n_new, ATT_WIDTH), F32),
        compiler_params=pltpu.CompilerParams(dimension_semantics=("arbitrary",),
                                             vmem_limit_bytes=VMEM_LIMIT),
        name="attn_sample",
    )(page_table, rel_bias, q, k_new, v_new, near_s, cache_kt, cache_vt)


def _conv_offsets(halo, stride):
    off0 = halo - (CONV_KERNEL - 1) * stride
    return [off0 + j * stride for j in range(CONV_KERNEL)]


def _shift_classes(offs):
    return sorted({o % SUBLANES for o in offs} - {0})


def _mix_kernel(att_ref, glu_ref, x_ref, st_ref, cw_ref, cb_ref, lg_ref, lb_ref, wo_ref, gpm_ref, gpf_ref,
                x1_ref, hn_ref, gbuf, cvbuf, wb, *shifted, tm, stride, halo, chunk):
    i = pl.program_id(1)
    group = min(tm, 128)

    @pl.when(i == 0)
    def _():
        gbuf[0:halo, :] = st_ref[0]

    gbuf[halo:halo + tm, :] = glu_ref[0]
    offs = _conv_offsets(halo, stride)
    for r, ref in zip(_shift_classes(offs), shifted):
        ref[...] = gbuf[r:r + ref.shape[0], :]
    src = dict(zip(_shift_classes(offs), shifted))
    src[0] = gbuf
    for j in range(CONV_KERNEL):
        wb[j] = jnp.broadcast_to(cw_ref[j:j + 1, :], (SUBLANES, CONV_WIDTH))
    wb[CONV_KERNEL] = jnp.broadcast_to(cb_ref[...], (SUBLANES, CONV_WIDTH))
    for c in range(tm // chunk):
        r0 = c * chunk
        acc = jnp.broadcast_to(wb[CONV_KERNEL][None], (chunk // SUBLANES, SUBLANES, CONV_WIDTH))
        for j, o in enumerate(offs):
            a = o - o % SUBLANES + r0
            rows = src[o % SUBLANES][a:a + chunk, :].reshape(chunk // SUBLANES, SUBLANES, CONV_WIDTH)
            acc = acc + wb[j][None] * rows
        acc = acc.reshape(chunk, CONV_WIDTH)
        mu = jnp.mean(acc, axis=-1, keepdims=True)
        d = acc - mu
        var = jnp.mean(d * d, axis=-1, keepdims=True)
        y = d * lax.rsqrt(var + EPS) * lg_ref[...] + lb_ref[...]
        cvbuf[r0:r0 + chunk, :] = (y * jax.nn.sigmoid(y)).astype(BF16)

        g1 = r0 + chunk
        if g1 % group == 0 or g1 == tm:
            g0 = (g1 - 1) // group * group
            mix = (jnp.dot(att_ref[0, g0:g1, :].astype(BF16), wo_ref[0:ATT_WIDTH, :], preferred_element_type=F32)
                   + jnp.dot(cvbuf[g0:g1, :], wo_ref[ATT_WIDTH:, :], preferred_element_type=F32))
            x1 = x_ref[0, g0:g1, :] + _rms(mix, gpm_ref[...])
            x1_ref[0, g0:g1, :] = x1
            hn_ref[0, g0:g1, :] = _rms(x1, gpf_ref[...]).astype(BF16)

    if tm >= halo:
        gbuf[0:halo, :] = gbuf[tm:tm + halo, :]


def _mix(att, glu, x, state, cw, cb, lg, lb, wo_bf, gpm, gpf, *, tm, stride):
    n_seq, rows, _ = x.shape
    halo = state.shape[1]
    assert rows == tm or tm >= halo
    offs = _conv_offsets(halo, stride)
    row_spec = lambda width: pl.BlockSpec((1, tm, width), lambda s, i: (s, i, 0))
    const = lambda shape: pl.BlockSpec(shape, lambda s, i: (0,) * len(shape))
    return pl.pallas_call(
        functools.partial(_mix_kernel, tm=tm, stride=stride, halo=halo, chunk=min(tm, 32)),
        grid=(n_seq, rows // tm),
        in_specs=[row_spec(ATT_WIDTH), row_spec(CONV_WIDTH), row_spec(D_MODEL),
                  pl.BlockSpec((1, halo, CONV_WIDTH), lambda s, i: (s, 0, 0)),
                  const((CONV_KERNEL, CONV_WIDTH)), const((1, CONV_WIDTH)), const((1, CONV_WIDTH)),
                  const((1, CONV_WIDTH)), const((D_MODEL, D_MODEL)), const((1, D_MODEL)), const((1, D_MODEL))],
        out_specs=[row_spec(D_MODEL), row_spec(D_MODEL)],
        out_shape=[jax.ShapeDtypeStruct((n_seq, rows, D_MODEL), F32),
                   jax.ShapeDtypeStruct((n_seq, rows, D_MODEL), BF16)],
        scratch_shapes=[pltpu.VMEM((halo + tm, CONV_WIDTH), F32), pltpu.VMEM((tm, CONV_WIDTH), BF16),
                        pltpu.VMEM((CONV_KERNEL + 1, SUBLANES, CONV_WIDTH), F32)] + [
            pltpu.VMEM((max(o - r for o in offs if o % SUBLANES == r) + tm, CONV_WIDTH), F32)
            for r in _shift_classes(offs)],
        compiler_params=pltpu.CompilerParams(dimension_semantics=("arbitrary", "arbitrary"),
                                             vmem_limit_bytes=VMEM_LIMIT),
        name="mix_s%d" % stride,
    )(att, glu, x, state, cw, cb, lg, lb, wo_bf, gpm, gpf)


def _ffn_kernel(hn_ref, x1_ref, wg_ref, wv_ref, wd_ref, dwg_ref, dwv_ref, dbg_ref, dbv_ref, stg_ref, stv_ref,
                gpo_ref, y_ref, tg_ref, tv_ref, ubuf_g, ubuf_v, car_g, car_v, acc_ref, abuf, *, tm, stride, halo, nc):
    i = pl.program_id(1)
    c = pl.program_id(2)
    hn = hn_ref[0]

    @pl.when(i == 0)
    def _():
        ubuf_g[0:halo, :] = stg_ref[0]
        ubuf_v[0:halo, :] = stv_ref[0]

    @pl.when(i > 0)
    def _():
        ubuf_g[0:halo, :] = car_g[c]
        ubuf_v[0:halo, :] = car_v[c]

    @pl.when(c == 0)
    def _():
        acc_ref[...] = jnp.zeros(acc_ref.shape, F32)

    def up(c0, c1):
        ubuf_g[halo:halo + tm, c0:c1] = jnp.dot(hn, wg_ref[:, c0:c1], preferred_element_type=F32)
        ubuf_v[halo:halo + tm, c0:c1] = jnp.dot(hn, wv_ref[:, c0:c1], preferred_element_type=F32)

    def conv(c0, c1, dw_ref, db_ref, ubuf):
        return (dw_ref[0:1, c0:c1] * ubuf[halo - 2 * stride:halo - 2 * stride + tm, c0:c1]
                + dw_ref[1:2, c0:c1] * ubuf[halo - stride:halo - stride + tm, c0:c1]
                + dw_ref[2:3, c0:c1] * ubuf[halo:halo + tm, c0:c1] + db_ref[:, c0:c1])

    ck = wg_ref.shape[1]
    bounds = [(c0, min(c0 + MXU_COLS, ck)) for c0 in range(0, ck, MXU_COLS)]
    def down(c0, c1):
        acc_ref[...] += jnp.dot(abuf[:, c0:c1], wd_ref[c0:c1, :], preferred_element_type=F32)

    up(*bounds[0])
    pending = None
    for n, (c0, c1) in enumerate(bounds):
        if n + 1 < len(bounds):
            up(*bounds[n + 1])
        if pending is not None:
            down(*pending)
            pending = None
        a = jax.nn.gelu(conv(c0, c1, dwg_ref, dbg_ref, ubuf_g), approximate=True) * conv(c0, c1, dwv_ref, dbv_ref,
                                                                                        ubuf_v)
        abuf[:, c0:c1] = a.astype(BF16)
        if (n + 1) % DOWN_GROUP == 0 or n + 1 == len(bounds):
            pending = (bounds[n // DOWN_GROUP * DOWN_GROUP][0], c1)
    down(*pending)

    for ubuf, car, tail_ref in ((ubuf_g, car_g, tg_ref), (ubuf_v, car_v, tv_ref)):
        last = ubuf[tm:tm + halo, :]
        car[c] = last
        tail_ref[0, 0] = last

    @pl.when(c == nc - 1)
    def _():
        y_ref[0] = x1_ref[0] + _rms(acc_ref[...], gpo_ref[...])


def _ffn(hn, x1, wup_bf, wdn_bf, dw, db, state, gpo, *, tm, stride, nc):
    n_seq, rows, _ = x1.shape
    halo = state.shape[1]
    ck = D_FF // nc
    row_spec = lambda width: pl.BlockSpec((1, tm, width), lambda s, i, c: (s, i, 0))
    mode = dict(pipeline_mode=pl.Buffered(1)) if nc == 1 else {}
    gate_cols = lambda shape, **kw: pl.BlockSpec(shape, lambda s, i, c: (0, c), **kw)
    val_cols = lambda shape, **kw: pl.BlockSpec(shape, lambda s, i, c: (0, c + nc), **kw)
    return pl.pallas_call(
        functools.partial(_ffn_kernel, tm=tm, stride=stride, halo=halo, nc=nc),
        grid=(n_seq, rows // tm, nc),
        in_specs=[row_spec(D_MODEL), row_spec(D_MODEL),
                  gate_cols((D_MODEL, ck), **mode), val_cols((D_MODEL, ck), **mode),
                  pl.BlockSpec((ck, D_MODEL), lambda s, i, c: (c, 0), **mode),
                  gate_cols((FFN_KERNEL, ck)), val_cols((FFN_KERNEL, ck)),
                  gate_cols((1, ck)), val_cols((1, ck)),
                  pl.BlockSpec((1, halo, ck), lambda s, i, c: (s, 0, c)),
                  pl.BlockSpec((1, halo, ck), lambda s, i, c: (s, 0, c + nc)),
                  pl.BlockSpec((1, D_MODEL), lambda s, i, c: (0, 0))],
        out_specs=[row_spec(D_MODEL),
                   pl.BlockSpec((1, 1, halo, ck), lambda s, i, c: (s, i, 0, c)),
                   pl.BlockSpec((1, 1, halo, ck), lambda s, i, c: (s, i, 0, c))],
        out_shape=[jax.ShapeDtypeStruct((n_seq, rows, D_MODEL), F32),
                   jax.ShapeDtypeStruct((n_seq, rows // tm, halo, D_FF), F32),
                   jax.ShapeDtypeStruct((n_seq, rows // tm, halo, D_FF), F32)],
        scratch_shapes=[pltpu.VMEM((halo + tm, ck), F32), pltpu.VMEM((halo + tm, ck), F32),
                        pltpu.VMEM((nc, halo, ck), F32), pltpu.VMEM((nc, halo, ck), F32),
                        pltpu.VMEM((tm, D_MODEL), F32), pltpu.VMEM((tm, ck), BF16)],
        compiler_params=pltpu.CompilerParams(dimension_semantics=("arbitrary",) * 3,
                                             vmem_limit_bytes=VMEM_LIMIT),
        name="ffn_s%d" % stride,
    )(hn, x1, wup_bf, wup_bf, wdn_bf, dw, dw, db, db, state, state, gpo)


def _tile(rows, want):
    return want if rows % want == 0 else rows


def kernel(x_prompt, x_sample, cache_k, cache_v, state_conv, state_ffn, page_table, rel_bias, g_pre_mix, w_in,
           conv_dw_w, conv_dw_b, conv_ln_g, conv_ln_b, w_out, g_post_mix, g_pre_ffn, w_ffn_up, ffn_dw_w,
           ffn_dw_b, w_ffn_down, g_post_ffn):
    depth = w_in.shape[0]
    assert depth == 1, "single-layer trunk"
    bp, seq, _ = x_prompt.shape
    bs, n_new, _ = x_sample.shape
    n_pool, page = cache_k.shape[1], cache_k.shape[2]
    n_pages = page_table.shape[1]
    assert seq % MOBA_BLOCK == 0 and page == LANES and (n_pages * page) % MOBA_BLOCK == 0
    assert n_new == SUBLANES and n_pages * page // MOBA_BLOCK <= LANES

    w_in_bf = w_in[0].astype(BF16)
    w_out_bf = w_out[0].astype(BF16)
    w_up_bf = w_ffn_up[0].astype(BF16)
    w_dn_bf = w_ffn_down[0].astype(BF16)
    cw, cb = conv_dw_w[0], conv_dw_b
    near_t, near_s = _bias_tiles(rel_bias)
    conv_halo_p = 32
    ffn_halo_p = SUBLANES

    tm_p = _tile(seq, 512)
    qt, kt, vt, kbf, vtb, kmean, glu_p = _proj(x_prompt, g_pre_mix, w_in_bf, prompt=True, tm=tm_p)
    att_p = _attn_prompt(qt, kbf, vtb, kmean.reshape(bp, seq // MOBA_BLOCK, ATT_WIDTH), near_t)
    x1_p, hn_p = _mix(att_p, glu_p, x_prompt, jnp.zeros((bp, conv_halo_p, CONV_WIDTH), F32), cw, cb,
                      conv_ln_g, conv_ln_b, w_out_bf, g_post_mix, g_pre_ffn, tm=tm_p, stride=1)
    y_prompt, tail_g, tail_v = _ffn(hn_p, x1_p, w_up_bf, w_dn_bf, ffn_dw_w[0], ffn_dw_b,
                                    jnp.zeros((bp, ffn_halo_p, 2 * D_FF), F32), g_post_ffn,
                                    tm=tm_p, stride=1, nc=1)
    k_prompt = kt.reshape(1, bp, N_HEADS, HEAD_DIM, seq).transpose(0, 1, 4, 2, 3)
    v_prompt = vt.reshape(1, bp, N_HEADS, HEAD_DIM, seq).transpose(0, 1, 4, 2, 3)
    conv_prompt = glu_p[:, seq - (CONV_KERNEL - 1):, :][None]
    ffn_prompt = jnp.concatenate([tail_g[:, -1], tail_v[:, -1]],
                                 axis=-1)[:, ffn_halo_p - (FFN_KERNEL - 1):, :][None]

    rows_s = n_new * bs
    to_tb = lambda a: a.transpose(1, 0, 2).reshape(1, rows_s, a.shape[-1])
    to_bt = lambda a: a.reshape(n_new, bs, a.shape[-1]).transpose(1, 0, 2)
    xs = to_tb(x_sample)
    q_s, k_s, v_s, glu_s = _proj(xs, g_pre_mix, w_in_bf, prompt=False, tm=rows_s)
    q_b, k_b, v_b = to_bt(q_s), to_bt(k_s), to_bt(v_s)
    cache_kt = cache_k[0].transpose(0, 2, 3, 1).reshape(n_pool, ATT_WIDTH, page)
    cache_vt = cache_v[0].transpose(0, 2, 3, 1).reshape(n_pool, ATT_WIDTH, page)
    att_b = _attn_sample(page_table, rel_bias, q_b, k_b, v_b, near_s, cache_kt, cache_vt,
                         nbuf=min(32, n_pages))
    conv_state = state_conv[0].transpose(1, 0, 2).reshape(1, (CONV_KERNEL - 1) * bs, CONV_WIDTH)
    x1_s, hn_s = _mix(to_tb(att_b), glu_s, xs, conv_state, cw, cb, conv_ln_g, conv_ln_b, w_out_bf,
                      g_post_mix, g_pre_ffn, tm=rows_s, stride=bs)
    ffn_state = state_ffn[0].transpose(1, 0, 2).reshape(1, (FFN_KERNEL - 1) * bs, 2 * D_FF)
    y_s, tail_gs, tail_vs = _ffn(hn_s, x1_s, w_up_bf, w_dn_bf, ffn_dw_w[0], ffn_dw_b, ffn_state, g_post_ffn,
                                 tm=rows_s, stride=bs, nc=2)
    y_sample = to_bt(y_s)
    k_sample = k_b.reshape(1, bs, n_new, N_HEADS, HEAD_DIM)
    v_sample = v_b.reshape(1, bs, n_new, N_HEADS, HEAD_DIM)
    conv_all = jnp.concatenate([conv_state[0], glu_s[0]], axis=0)[n_new * bs:]
    conv_sample = conv_all.reshape(CONV_KERNEL - 1, bs, CONV_WIDTH).transpose(1, 0, 2)[None]
    ffn_sample = jnp.concatenate([tail_gs[:, -1], tail_vs[:, -1]], axis=-1).reshape(
        FFN_KERNEL - 1, bs, 2 * D_FF).transpose(1, 0, 2)[None]

    return (y_prompt, y_sample, k_prompt, v_prompt, conv_prompt, ffn_prompt,
            k_sample, v_sample, conv_sample, ffn_sample)
```

```python
import functools
import math

import numpy as np
import jax
import jax.numpy as jnp
from jax import lax
from jax.experimental import pallas as pl
from jax.experimental.pallas import tpu as pltpu

F32 = jnp.float32
BF16 = jnp.bfloat16

D_MODEL = 1024
HEAD_DIM = 64
ATT_WIDTH = 512
N_HEADS = 8
N_PAIRS = N_HEADS // 2
CONV_WIDTH = 512
CONV_KERNEL = 31
MOBA_BLOCK = 256
MOBA_TOPK = 3
N_BUCKETS = 32
MAX_DISTANCE = 128
D_FF = 2816
FFN_KERNEL = 3
EPS = 1e-6
SCALE = HEAD_DIM ** -0.5
LOG2E = math.log2(math.e)
NEG = -1e30
LANES = 128
SUBLANES = 8
VMEM_LIMIT = 56 * 1024 * 1024
DOWN_GROUP = 6
MXU_COLS = 256
ONES_ROWS = 16
V_ROWS = HEAD_DIM + ONES_ROWS


def _bucket_thresholds():
    n = np.arange(0, 4 * MAX_DISTANCE)
    max_exact = N_BUCKETS // 2
    nf = np.maximum(n, 1).astype(np.float64)
    large = max_exact + (np.log(nf / max_exact) / math.log(MAX_DISTANCE / max_exact)
                         * (N_BUCKETS - max_exact)).astype(np.int64)
    bucket = np.where(n < max_exact, n, np.minimum(large, N_BUCKETS - 1))
    return tuple(int(np.argmax(bucket >= k)) for k in range(1, N_BUCKETS))


_BUCKET_THR = _bucket_thresholds()


def _rms(x, g):
    return x * lax.rsqrt(jnp.mean(x * x, axis=-1, keepdims=True) + EPS) * g


def _bias_kernel(tbl_ref, near_t_ref, near_s_ref):
    p = pl.program_id(0)

    def bias_of(dist, h):
        b = jnp.full(dist.shape, tbl_ref[0, h], F32)
        for k in range(1, N_BUCKETS):
            b = jnp.where(dist >= _BUCKET_THR[k - 1], tbl_ref[k, h], b)
        return jnp.where(dist >= 0, b, NEG)

    jj = lax.broadcasted_iota(jnp.int32, (2 * MOBA_BLOCK, MOBA_BLOCK), 0)
    ii = lax.broadcasted_iota(jnp.int32, (2 * MOBA_BLOCK, MOBA_BLOCK), 1)
    dist = MOBA_BLOCK + ii - jj
    for half in range(2):
        h = 2 * p + half
        near_t_ref[0, :, half * MOBA_BLOCK:(half + 1) * MOBA_BLOCK] = (
            bias_of(dist, h) - tbl_ref[N_BUCKETS - 1, h]) * LOG2E

    @pl.when(p == 0)
    def _():
        tt = lax.broadcasted_iota(jnp.int32, (SUBLANES, 2 * MOBA_BLOCK), 0)
        j2 = lax.broadcasted_iota(jnp.int32, (SUBLANES, 2 * MOBA_BLOCK), 1)
        d2 = MOBA_BLOCK + tt - j2
        for h in range(N_HEADS):
            near_s_ref[h * SUBLANES:(h + 1) * SUBLANES, :] = bias_of(d2, h)


def _bias_tiles(rel_bias):
    return pl.pallas_call(
        _bias_kernel,
        grid=(N_PAIRS,),
        in_specs=[pl.BlockSpec(memory_space=pltpu.SMEM)],
        out_specs=[pl.BlockSpec((1, 2 * MOBA_BLOCK, 2 * MOBA_BLOCK), lambda p: (p, 0, 0)),
                   pl.BlockSpec((N_HEADS * SUBLANES, 2 * MOBA_BLOCK), lambda p: (0, 0))],
        out_shape=[jax.ShapeDtypeStruct((N_PAIRS, 2 * MOBA_BLOCK, 2 * MOBA_BLOCK), F32),
                   jax.ShapeDtypeStruct((N_HEADS * SUBLANES, 2 * MOBA_BLOCK), F32)],
        compiler_params=pltpu.CompilerParams(dimension_semantics=("arbitrary",)),
        name="bias_tiles",
    )(rel_bias)


def _proj_kernel(x_ref, g_ref, w_ref, *out_refs, prompt, tm):
    h = _rms(x_ref[0], g_ref[...])
    p = jnp.dot(h.astype(BF16), w_ref[...], preferred_element_type=F32)
    q = p[:, 0:ATT_WIDTH]
    k = p[:, ATT_WIDTH:2 * ATT_WIDTH]
    v = p[:, 2 * ATT_WIDTH:3 * ATT_WIDTH]
    ga = p[:, 3 * ATT_WIDTH:3 * ATT_WIDTH + CONV_WIDTH]
    gb = p[:, 3 * ATT_WIDTH + CONV_WIDTH:]
    glu = ga * jax.nn.sigmoid(gb)
    if prompt:
        qt_ref, kt_ref, vt_ref, kbf_ref, vtb_ref, km_ref, glu_ref = out_refs
        qt_ref[0] = q.T
        kt_ref[0] = k.T
        vt = v.T
        vt_ref[0] = vt
        row_blk = (pl.program_id(1) * (tm // MOBA_BLOCK)
                   + lax.broadcasted_iota(jnp.int32, (tm, LANES), 0) // MOBA_BLOCK)
        onehot = jnp.where(lax.broadcasted_iota(jnp.int32, (tm, LANES), 1) == row_blk, 1.0, 0.0).astype(BF16)
        kb = k.astype(BF16)
        for pr in range(N_PAIRS):
            kbf_ref[0, :, 2 * pr * LANES:(2 * pr + 1) * LANES] = kb[:, pr * LANES:(pr + 1) * LANES]
            kbf_ref[0, :, (2 * pr + 1) * LANES:(2 * pr + 2) * LANES] = onehot
        ones = jnp.ones((ONES_ROWS, MOBA_BLOCK), F32)
        for c in range(tm // MOBA_BLOCK):
            vc = vt[:, c * MOBA_BLOCK:(c + 1) * MOBA_BLOCK]
            parts = []
            for hd in range(N_HEADS):
                parts += [vc[hd * HEAD_DIM:(hd + 1) * HEAD_DIM], ones]
            vtb_ref[0, c] = jnp.concatenate(parts, axis=0).astype(BF16)
        km_ref[0] = jnp.mean(k.reshape(tm // MOBA_BLOCK, MOBA_BLOCK, ATT_WIDTH), axis=1)[:, None, :]
        glu_ref[0] = glu
    else:
        q_ref, k_ref, v_ref, glu_ref = out_refs
        q_ref[0] = q
        k_ref[0] = k
        v_ref[0] = v
        glu_ref[0] = glu


def _proj(x, g, w_bf, *, prompt, tm):
    n_seq, rows, _ = x.shape
    n_out = w_bf.shape[1]
    grid = (n_seq, rows // tm)
    row_spec = lambda width: pl.BlockSpec((1, tm, width), lambda s, i: (s, i, 0))
    col_spec = pl.BlockSpec((1, ATT_WIDTH, tm), lambda s, i: (s, 0, i))
    if prompt:
        nb = tm // MOBA_BLOCK
        out_specs = [col_spec, col_spec, col_spec, row_spec(2 * ATT_WIDTH),
                     pl.BlockSpec((1, nb, N_HEADS * V_ROWS, MOBA_BLOCK), lambda s, i: (s, i, 0, 0)),
                     pl.BlockSpec((1, nb, 1, ATT_WIDTH), lambda s, i: (s, i, 0, 0)),
                     row_spec(CONV_WIDTH)]
        t_shape = jax.ShapeDtypeStruct((n_seq, ATT_WIDTH, rows), F32)
        out_shape = [t_shape, t_shape, t_shape,
                     jax.ShapeDtypeStruct((n_seq, rows, 2 * ATT_WIDTH), BF16),
                     jax.ShapeDtypeStruct((n_seq, rows // MOBA_BLOCK, N_HEADS * V_ROWS, MOBA_BLOCK), BF16),
                     jax.ShapeDtypeStruct((n_seq, rows // MOBA_BLOCK, 1, ATT_WIDTH), F32),
                     jax.ShapeDtypeStruct((n_seq, rows, CONV_WIDTH), F32)]
    else:
        out_specs = [row_spec(ATT_WIDTH)] * 3 + [row_spec(CONV_WIDTH)]
        out_shape = [jax.ShapeDtypeStruct((n_seq, rows, ATT_WIDTH), F32)] * 3 + [
            jax.ShapeDtypeStruct((n_seq, rows, CONV_WIDTH), F32)]
    return pl.pallas_call(
        functools.partial(_proj_kernel, prompt=prompt, tm=tm),
        grid=grid,
        in_specs=[row_spec(D_MODEL),
                  pl.BlockSpec((1, D_MODEL), lambda s, i: (0, 0)),
                  pl.BlockSpec((D_MODEL, n_out), lambda s, i: (0, 0))],
        out_specs=out_specs,
        out_shape=out_shape,
        compiler_params=pltpu.CompilerParams(dimension_semantics=("arbitrary", "arbitrary"),
                                             vmem_limit_bytes=VMEM_LIMIT),
        name="proj_prompt" if prompt else "proj_sample",
    )(x, g, w_bf)


def _select_topk(gate, idx, axis, n_valid):
    big = jnp.int32(2 ** 30)
    g = jnp.where(idx < n_valid, gate, -jnp.inf)
    sel = jnp.zeros(gate.shape, jnp.bool_)
    for _ in range(MOBA_TOPK):
        mx = jnp.max(g, axis=axis, keepdims=True)
        first = jnp.min(jnp.where(g == mx, idx, big), axis=axis, keepdims=True)
        hit = idx == first
        sel = sel | (hit & (mx > -jnp.inf))
        g = jnp.where(hit, -jnp.inf, g)
    return sel


def _attn_stream(i, nblk, s, qt_ref, qtn_ref, k_ref, vt_ref, km_ref, near_ref, o_ref, qaug_ref, mnext_ref,
                 sa_ref, sb_ref, pa_ref, pb_ref, acc_ref):
    blk = MOBA_BLOCK

    def block_diag(qt):
        z = jnp.zeros((HEAD_DIM, blk), F32)
        return jnp.concatenate([jnp.concatenate([qt[:HEAD_DIM], z], axis=1),
                                jnp.concatenate([z, qt[HEAD_DIM:]], axis=1)], axis=0)

    def mask_rows(q2t, tile):
        gate = jnp.dot(km_ref[0], q2t, precision=lax.Precision.HIGHEST, preferred_element_type=F32)
        n_idx = lax.broadcasted_iota(jnp.int32, (nblk, 2 * blk), 0)
        sel = _select_topk(gate, n_idx, 0, tile)
        return jnp.concatenate([jnp.where(sel | (n_idx == tile), 0.0, NEG),
                                jnp.full((LANES - nblk, 2 * blk), NEG, F32)], axis=0).astype(BF16)

    def scores_into(dst_ref, j0, j1, bias):
        mx = None
        for half, j in enumerate((j0, j1)):
            kb = k_ref[0, pl.ds(pl.multiple_of(j * blk, blk), blk), :]
            s = jnp.dot(kb, qaug_ref[...], preferred_element_type=F32)
            if bias is not None:
                s = s + bias[half]
            dst_ref[half] = s
            smax = jnp.max(s, axis=0, keepdims=True)
            mx = smax if mx is None else jnp.maximum(mx, smax)
        return mx

    def probs(src_ref, dst_ref, m, mx):
        m_new = jnp.maximum(m, mx)
        dst_ref[0] = jnp.exp2(src_ref[0] - m_new).astype(BF16)
        dst_ref[1] = jnp.exp2(src_ref[1] - m_new).astype(BF16)
        return m_new, jnp.exp2(m - m_new)

    def accumulate(p_ref, j0, j1, alpha):
        p0, p1 = p_ref[0], p_ref[1]
        v0, v1 = vt_ref[0, j0], vt_ref[0, j1]
        d_a = (jnp.dot(v0[:V_ROWS], p0[:, :blk], preferred_element_type=F32)
               + jnp.dot(v1[:V_ROWS], p1[:, :blk], preferred_element_type=F32))
        d_b = (jnp.dot(v0[V_ROWS:], p0[:, blk:], preferred_element_type=F32)
               + jnp.dot(v1[V_ROWS:], p1[:, blk:], preferred_element_type=F32))
        acc_ref[0] = acc_ref[0] * alpha[:, :blk] + d_a
        acc_ref[1] = acc_ref[1] * alpha[:, blk:] + d_b

    none = nblk - 1
    prev = jnp.where(i > 0, i - 1, none)
    n_far = jnp.maximum(i - 1, 0)

    def prologue():
        qaug_ref[0:LANES, :] = (block_diag(qt_ref[0]) * (SCALE * LOG2E)).astype(BF16)
        row = lax.broadcasted_iota(jnp.int32, (LANES, 2 * blk), 0)
        qaug_ref[LANES:, :] = jnp.where((i == 0) & (row == 0), 0.0,
                                        jnp.where(i == 0, NEG, mnext_ref[...].astype(F32))).astype(BF16)

        mx = scores_into(sa_ref, i, prev, (near_ref[0, blk:, :], near_ref[0, :blk, :]))
        mnext_ref[...] = mask_rows(block_diag(qtn_ref[0]), i + 1)
        pb_ref[...] = jnp.zeros(pb_ref.shape, BF16)
        acc_ref[...] = jnp.zeros(acc_ref.shape, F32)
        return jnp.full((1, 2 * blk), NEG, F32), mx, jnp.ones((1, 2 * blk), F32), i, prev, i, i

    def stage(even, t, state):
        s_src, s_dst, p_dst, p_src = (sa_ref, sb_ref, pa_ref, pb_ref) if even else (sb_ref, sa_ref, pb_ref, pa_ref)
        m, mx, alpha_p, js0, js1, jp0, jp1 = state
        j0 = 2 * t
        j1 = jnp.where(j0 + 1 < n_far, j0 + 1, none)
        mx_next = scores_into(s_dst, j0, j1, None)
        m_new, alpha = probs(s_src, p_dst, m, mx)
        accumulate(p_src, jp0, jp1, alpha_p)
        return m_new, mx_next, alpha, j0, j1, js0, js1

    def finish(even, state):
        s_src, p_dst, p_src = (sa_ref, pa_ref, pb_ref) if even else (sb_ref, pb_ref, pa_ref)
        m, mx, alpha_p, js0, js1, jp0, jp1 = state
        accumulate(p_src, jp0, jp1, alpha_p)
        _, alpha = probs(s_src, p_dst, m, mx)
        accumulate(p_dst, js0, js1, alpha)
        a_a, a_b = acc_ref[0], acc_ref[1]
        out_t = jnp.concatenate([a_a[:HEAD_DIM] / a_a[HEAD_DIM:HEAD_DIM + 1],
                                 a_b[:HEAD_DIM] / a_b[HEAD_DIM:HEAD_DIM + 1]], axis=0)
        o_ref[0, :, s * LANES:(s + 1) * LANES] = out_t.T

    return prologue, stage, finish


ATTN_STREAMS = 4
_STREAM_INPUTS = 6
_STREAM_SCRATCH = 7
_MNEXT_SLOT = 1


def _attn_prompt_kernel(*refs, nblk):
    i = pl.program_id(2)
    n_in = ATTN_STREAMS * _STREAM_INPUTS
    ins, o_ref, scr = refs[:n_in], refs[n_in], refs[n_in + 1:]
    streams = [_attn_stream(i, nblk, s, *ins[s * _STREAM_INPUTS:(s + 1) * _STREAM_INPUTS], o_ref,
                            *scr[s * _STREAM_SCRATCH:(s + 1) * _STREAM_SCRATCH]) for s in range(ATTN_STREAMS)]
    n_pairs = jnp.maximum(i, 1) // 2

    @pl.when((pl.program_id(0) == 0) & (pl.program_id(1) == 0) & (i == 0))
    def _():
        for s in range(ATTN_STREAMS):
            mnext = scr[s * _STREAM_SCRATCH + _MNEXT_SLOT]
            mnext[...] = jnp.zeros(mnext.shape, BF16)

    states = tuple(prologue() for prologue, _, _ in streams)

    def step(t, states):
        def run(even):
            return lambda sts: tuple(stage(even, t, st) for (_, stage, _), st in zip(streams, sts))
        return lax.cond(t % 2 == 0, run(True), run(False), states)

    states = lax.fori_loop(0, n_pairs, step, states)
    for even in (True, False):
        @pl.when(n_pairs % 2 == (0 if even else 1))
        def _():
            for (_, _, finish), st in zip(streams, states):
                finish(even, st)


def _attn_prompt(qt, kaug, vtb, kmean, near_t):
    n_seq, _, t = qt.shape
    nblk = t // MOBA_BLOCK
    assert nblk % 2 == 0 and nblk < LANES and N_PAIRS % ATTN_STREAMS == 0
    ns = ATTN_STREAMS
    once = dict(pipeline_mode=pl.Buffered(1))

    def stream_specs(s):
        pair = lambda g: ns * g + s
        return [pl.BlockSpec((1, LANES, MOBA_BLOCK), lambda b, g, i: (b, pair(g), i)),
                pl.BlockSpec((1, LANES, MOBA_BLOCK), lambda b, g, i: (b, pair(g), jnp.minimum(i + 1, nblk - 1))),
                pl.BlockSpec((1, t, 2 * LANES), lambda b, g, i: (b, 0, pair(g)), **once),
                pl.BlockSpec((1, nblk, 2 * V_ROWS, MOBA_BLOCK), lambda b, g, i: (b, 0, pair(g), 0), **once),
                pl.BlockSpec((1, nblk, LANES), lambda b, g, i: (b, 0, pair(g))),
                pl.BlockSpec((1, 2 * MOBA_BLOCK, 2 * MOBA_BLOCK), lambda b, g, i: (pair(g), 0, 0), **once)]

    stream_scratch = [pltpu.VMEM((2 * LANES, 2 * MOBA_BLOCK), BF16),
                      pltpu.VMEM((LANES, 2 * MOBA_BLOCK), BF16),
                      pltpu.VMEM((2, MOBA_BLOCK, 2 * MOBA_BLOCK), F32),
                      pltpu.VMEM((2, MOBA_BLOCK, 2 * MOBA_BLOCK), F32),
                      pltpu.VMEM((2, MOBA_BLOCK, 2 * MOBA_BLOCK), BF16),
                      pltpu.VMEM((2, MOBA_BLOCK, 2 * MOBA_BLOCK), BF16),
                      pltpu.VMEM((2, V_ROWS, MOBA_BLOCK), F32)]
    assert len(stream_specs(0)) == _STREAM_INPUTS and len(stream_scratch) == _STREAM_SCRATCH
    return pl.pallas_call(
        functools.partial(_attn_prompt_kernel, nblk=nblk),
        grid=(n_seq, N_PAIRS // ns, nblk),
        in_specs=[spec for s in range(ns) for spec in stream_specs(s)],
        out_specs=pl.BlockSpec((1, MOBA_BLOCK, ns * LANES), lambda b, g, i: (b, i, g)),
        out_shape=jax.ShapeDtypeStruct((n_seq, t, ATT_WIDTH), F32),
        scratch_shapes=stream_scratch * ns,
        compiler_params=pltpu.CompilerParams(dimension_semantics=("arbitrary",) * 3,
                                             vmem_limit_bytes=VMEM_LIMIT),
        name="attn_prompt",
    )(*([qt, qt, kaug, vtb, kmean, near_t] * ns))


def _attn_sample_kernel(pt_ref, tbl_ref, q_ref, kn_ref, vn_ref, near_ref, ck_hbm, cv_hbm, o_ref,
                        s_all, km_t, kbuf, vbuf, ksem, vsem, *, n_pages, nbuf, step, n_new):
    b = pl.program_id(0)
    n_seq = pl.num_programs(0)
    n_rows = N_HEADS * n_new
    page = LANES
    nblk = n_pages * page // MOBA_BLOCK

    def k_copy(seq, p, slot):
        return pltpu.make_async_copy(ck_hbm.at[pt_ref[seq, p]], kbuf.at[slot], ksem.at[slot])

    def v_copy(p, slot):
        return pltpu.make_async_copy(cv_hbm.at[pt_ref[b, p]], vbuf.at[slot], vsem.at[slot])

    @pl.when(b == 0)
    def _():
        for p in range(nbuf):
            k_copy(0, p, p).start()

    row_head = lax.broadcasted_iota(jnp.int32, (n_rows, ATT_WIDTH), 0) // n_new
    col_head = lax.broadcasted_iota(jnp.int32, (n_rows, ATT_WIDTH), 1) // HEAD_DIM
    diag = row_head == col_head
    q = q_ref[0]
    qbd_f = jnp.where(diag, jnp.concatenate([q] * N_HEADS, axis=0), 0.0)
    qbd_b = (qbd_f * SCALE).astype(BF16)
    km_t[...] = jnp.zeros(km_t.shape, F32)
    lane_k = lax.broadcasted_iota(jnp.int32, (ATT_WIDTH, LANES), 1)

    def k_step(it, _):
        p0 = it * step
        slot0 = p0 % nbuf
        for r in range(step):
            k_copy(b, p0 + r, slot0 + r).wait()
        km = km_t[...]
        for r in range(0, step, 2):
            kt0 = kbuf[slot0 + r]
            kt1 = kbuf[slot0 + r + 1]
            s_all[p0 + r] = jnp.dot(qbd_b, kt0.astype(BF16), preferred_element_type=F32)
            s_all[p0 + r + 1] = jnp.dot(qbd_b, kt1.astype(BF16), preferred_element_type=F32)
            mean = jnp.sum(kt0 + kt1, axis=1, keepdims=True) * (1.0 / MOBA_BLOCK)
            km = jnp.where(lane_k == (p0 + r) // 2, mean, km)
        km_t[...] = km

        @pl.when(p0 + nbuf < n_pages)
        def _():
            for r in range(step):
                k_copy(b, p0 + nbuf + r, slot0 + r).start()

        return 0

    lax.fori_loop(0, n_pages // step, k_step, 0)

    @pl.when(b + 1 < n_seq)
    def _():
        for p in range(nbuf):
            k_copy(b + 1, p, p).start()

    for p in range(nbuf):
        v_copy(p, p).start()

    gate = jnp.dot(qbd_f, km_t[...], precision=lax.Precision.HIGHEST,
                   preferred_element_type=F32)
    lane = lax.broadcasted_iota(jnp.int32, (n_rows, LANES), 1)
    sel = _select_topk(gate, lane, 1, nblk)
    rh = lax.broadcasted_iota(jnp.int32, (n_rows, 1), 0) // n_new
    far = jnp.zeros((n_rows, 1), F32)
    for h in range(N_HEADS):
        far = jnp.where(rh == h, tbl_ref[N_BUCKETS - 1, h], far)
    mask = jnp.where(sel, far, NEG)
    near = near_ref[...]
    for n in range(nblk):
        if n == nblk - 1:
            seln = jnp.broadcast_to(mask[:, n:n + 1], (n_rows, page)) > 0.5 * NEG
            s_all[2 * n] = s_all[2 * n] + jnp.where(seln, near[:, 0:page], NEG)
            s_all[2 * n + 1] = s_all[2 * n + 1] + jnp.where(seln, near[:, page:2 * page], NEG)
        else:
            add = jnp.broadcast_to(mask[:, n:n + 1], (n_rows, page))
            s_all[2 * n] = s_all[2 * n] + add
            s_all[2 * n + 1] = s_all[2 * n + 1] + add
    kn = jnp.concatenate([kn_ref[0], jnp.zeros((page - n_new, ATT_WIDTH), F32)], axis=0).astype(BF16)
    s_own = lax.dot_general(qbd_b, kn, (((1,), (1,)), ((), ())),
                            preferred_element_type=F32) + near[:, 2 * page:3 * page]
    mrun = lax.fori_loop(0, n_pages, lambda j, mm: jnp.maximum(mm, s_all[j]), s_own)
    m = jnp.max(mrun, axis=1, keepdims=True)
    p_own = jnp.exp(s_own - m)

    def exp_step(j, lsum):
        pj = jnp.exp(s_all[j] - m)
        s_all[j] = pj
        return lsum + pj

    lsum = lax.fori_loop(0, n_pages, exp_step, p_own)
    l = jnp.sum(lsum, axis=1, keepdims=True)
    vn = jnp.concatenate([vn_ref[0], jnp.zeros((page - n_new, ATT_WIDTH), F32)], axis=0).astype(BF16)
    acc = jnp.dot(p_own.astype(BF16), vn, preferred_element_type=F32)

    def v_step(it, acc):
        p0 = it * step
        slot0 = p0 % nbuf
        for r in range(step):
            v_copy(p0 + r, slot0 + r).wait()
        parts = []
        for r in range(step):
            pb = s_all[p0 + r].astype(BF16)
            vt = vbuf[slot0 + r].astype(BF16)
            parts.append(lax.dot_general(pb, vt, (((1,), (1,)), ((), ())), preferred_element_type=F32))
        while len(parts) > 1:
            parts = [x + y for x, y in zip(parts[0::2], parts[1::2])]

        @pl.when(p0 + nbuf < n_pages)
        def _():
            for r in range(step):
                v_copy(p0 + nbuf + r, slot0 + r).start()

        return acc + parts[0]

    acc = lax.fori_loop(0, n_pages // step, v_step, acc)
    o = jnp.where(diag, acc / l, 0.0)
    o_ref[0] = jnp.sum(o.reshape(N_HEADS, n_new, ATT_WIDTH), axis=0)


def _attn_sample(page_table, rel_bias, q, k_new, v_new, near_s, cache_kt, cache_vt, *, nbuf):
    n_seq, n_new, _ = q.shape
    n_pages = page_table.shape[1]
    n_rows = N_HEADS * n_new
    page = cache_kt.shape[2]
    step = min(8, nbuf)
    assert step & (step - 1) == 0 and step >= 2 and nbuf % step == 0 and n_pages % nbuf == 0
    new_spec = pl.BlockSpec((1, n_new, ATT_WIDTH), lambda b, pt: (b, 0, 0))
    in_specs = [pl.BlockSpec(memory_space=pltpu.SMEM), new_spec, new_spec, new_spec,
                pl.BlockSpec((n_rows, 2 * MOBA_BLOCK), lambda b, pt: (0, 0)),
                pl.BlockSpec(memory_space=pl.ANY), pl.BlockSpec(memory_space=pl.ANY)]
    return pl.pallas_call(
        functools.partial(_attn_sample_kernel, n_pages=n_pages, nbuf=nbuf, step=step, n_new=n_new),
        grid_spec=pltpu.PrefetchScalarGridSpec(
            num_scalar_prefetch=1,
            grid=(n_seq,),
            in_specs=in_specs,
            out_specs=new_spec,
            scratch_shapes=[pltpu.VMEM((n_pages, n_rows, page), F32),
                            pltpu.VMEM((ATT_WIDTH, LANES), F32),
                            pltpu.VMEM((nbuf, ATT_WIDTH, page), F32),
                            pltpu.VMEM((nbuf, ATT_WIDTH, page), F32),
                            pltpu.SemaphoreType.DMA((nbuf,)),
                            pltpu.SemaphoreType.DMA((nbuf,))]),
        out_shape=jax.ShapeDtypeStruct((n_seq, n_new, ATT_WIDTH), F32),
        compiler_params=pltpu.CompilerParams(dimension_semantics=("arbitrary",),
                                             vmem_limit_bytes=VMEM_LIMIT),
        name="attn_sample",
    )(page_table, rel_bias, q, k_new, v_new, near_s, cache_kt, cache_vt)


def _conv_offsets(halo, stride):
    off0 = halo - (CONV_KERNEL - 1) * stride
    return [off0 + j * stride for j in range(CONV_KERNEL)]


def _shift_classes(offs):
    return sorted({o % SUBLANES for o in offs} - {0})


def _mix_kernel(att_ref, glu_ref, x_ref, st_ref, cw_ref, cb_ref, lg_ref, lb_ref, wo_ref, gpm_ref, gpf_ref,
                x1_ref, hn_ref, gbuf, cvbuf, wb, *shifted, tm, stride, halo, chunk):
    i = pl.program_id(1)
    group = min(tm, 128)

    @pl.when(i == 0)
    def _():
        gbuf[0:halo, :] = st_ref[0]

    gbuf[halo:halo + tm, :] = glu_ref[0]
    offs = _conv_offsets(halo, stride)
    for r, ref in zip(_shift_classes(offs), shifted):
        ref[...] = gbuf[r:r + ref.shape[0], :]
    src = dict(zip(_shift_classes(offs), shifted))
    src[0] = gbuf
    for j in range(CONV_KERNEL):
        wb[j] = jnp.broadcast_to(cw_ref[j:j + 1, :], (SUBLANES, CONV_WIDTH))
    wb[CONV_KERNEL] = jnp.broadcast_to(cb_ref[...], (SUBLANES, CONV_WIDTH))
    for c in range(tm // chunk):
        r0 = c * chunk
        acc = jnp.broadcast_to(wb[CONV_KERNEL][None], (chunk // SUBLANES, SUBLANES, CONV_WIDTH))
        for j, o in enumerate(offs):
            a = o - o % SUBLANES + r0
            rows = src[o % SUBLANES][a:a + chunk, :].reshape(chunk // SUBLANES, SUBLANES, CONV_WIDTH)
            acc = acc + wb[j][None] * rows
        acc = acc.reshape(chunk, CONV_WIDTH)
        mu = jnp.mean(acc, axis=-1, keepdims=True)
        d = acc - mu
        var = jnp.mean(d * d, axis=-1, keepdims=True)
        y = d * lax.rsqrt(var + EPS) * lg_ref[...] + lb_ref[...]
        cvbuf[r0:r0 + chunk, :] = (y * jax.nn.sigmoid(y)).astype(BF16)

        g1 = r0 + chunk
        if g1 % group == 0 or g1 == tm:
            g0 = (g1 - 1) // group * group
            mix = (jnp.dot(att_ref[0, g0:g1, :].astype(BF16), wo_ref[0:ATT_WIDTH, :], preferred_element_type=F32)
                   + jnp.dot(cvbuf[g0:g1, :], wo_ref[ATT_WIDTH:, :], preferred_element_type=F32))
            x1 = x_ref[0, g0:g1, :] + _rms(mix, gpm_ref[...])
            x1_ref[0, g0:g1, :] = x1
            hn_ref[0, g0:g1, :] = _rms(x1, gpf_ref[...]).astype(BF16)

    if tm >= halo:
        gbuf[0:halo, :] = gbuf[tm:tm + halo, :]


def _mix(att, glu, x, state, cw, cb, lg, lb, wo_bf, gpm, gpf, *, tm, stride):
    n_seq, rows, _ = x.shape
    halo = state.shape[1]
    assert rows == tm or tm >= halo
    offs = _conv_offsets(halo, stride)
    row_spec = lambda width: pl.BlockSpec((1, tm, width), lambda s, i: (s, i, 0))
    const = lambda shape: pl.BlockSpec(shape, lambda s, i: (0,) * len(shape))
    return pl.pallas_call(
        functools.partial(_mix_kernel, tm=tm, stride=stride, halo=halo, chunk=min(tm, 32)),
        grid=(n_seq, rows // tm),
        in_specs=[row_spec(ATT_WIDTH), row_spec(CONV_WIDTH), row_spec(D_MODEL),
                  pl.BlockSpec((1, halo, CONV_WIDTH), lambda s, i: (s, 0, 0)),
                  const((CONV_KERNEL, CONV_WIDTH)), const((1, CONV_WIDTH)), const((1, CONV_WIDTH)),
                  const((1, CONV_WIDTH)), const((D_MODEL, D_MODEL)), const((1, D_MODEL)), const((1, D_MODEL))],
        out_specs=[row_spec(D_MODEL), row_spec(D_MODEL)],
        out_shape=[jax.ShapeDtypeStruct((n_seq, rows, D_MODEL), F32),
                   jax.ShapeDtypeStruct((n_seq, rows, D_MODEL), BF16)],
        scratch_shapes=[pltpu.VMEM((halo + tm, CONV_WIDTH), F32), pltpu.VMEM((tm, CONV_WIDTH), BF16),
                        pltpu.VMEM((CONV_KERNEL + 1, SUBLANES, CONV_WIDTH), F32)] + [
            pltpu.VMEM((max(o - r for o in offs if o % SUBLANES == r) + tm, CONV_WIDTH), F32)
            for r in _shift_classes(offs)],
        compiler_params=pltpu.CompilerParams(dimension_semantics=("arbitrary", "arbitrary"),
                                             vmem_limit_bytes=VMEM_LIMIT),
        name="mix_s%d" % stride,
    )(att, glu, x, state, cw, cb, lg, lb, wo_bf, gpm, gpf)


def _ffn_kernel(hn_ref, x1_ref, wg_ref, wv_ref, wd_ref, dwg_ref, dwv_ref, dbg_ref, dbv_ref, stg_ref, stv_ref,
                gpo_ref, y_ref, tg_ref, tv_ref, ubuf_g, ubuf_v, car_g, car_v, acc_ref, abuf, *, tm, stride, halo, nc):
    i = pl.program_id(1)
    c = pl.program_id(2)
    hn = hn_ref[0]

    @pl.when(i == 0)
    def _():
        ubuf_g[0:halo, :] = stg_ref[0]
        ubuf_v[0:halo, :] = stv_ref[0]

    @pl.when(i > 0)
    def _():
        ubuf_g[0:halo, :] = car_g[c]
        ubuf_v[0:halo, :] = car_v[c]

    @pl.when(c == 0)
    def _():
        acc_ref[...] = jnp.zeros(acc_ref.shape, F32)

    def up(c0, c1):
        ubuf_g[halo:halo + tm, c0:c1] = jnp.dot(hn, wg_ref[:, c0:c1], preferred_element_type=F32)
        ubuf_v[halo:halo + tm, c0:c1] = jnp.dot(hn, wv_ref[:, c0:c1], preferred_element_type=F32)

    def conv(c0, c1, dw_ref, db_ref, ubuf):
        return (dw_ref[0:1, c0:c1] * ubuf[halo - 2 * stride:halo - 2 * stride + tm, c0:c1]
                + dw_ref[1:2, c0:c1] * ubuf[halo - stride:halo - stride + tm, c0:c1]
                + dw_ref[2:3, c0:c1] * ubuf[halo:halo + tm, c0:c1] + db_ref[:, c0:c1])

    ck = wg_ref.shape[1]
    bounds = [(c0, min(c0 + MXU_COLS, ck)) for c0 in range(0, ck, MXU_COLS)]
    def down(c0, c1):
        acc_ref[...] += jnp.dot(abuf[:, c0:c1], wd_ref[c0:c1, :], preferred_element_type=F32)

    up(*bounds[0])
    pending = None
    for n, (c0, c1) in enumerate(bounds):
        if n + 1 < len(bounds):
            up(*bounds[n + 1])
        if pending is not None:
            down(*pending)
            pending = None
        a = jax.nn.gelu(conv(c0, c1, dwg_ref, dbg_ref, ubuf_g), approximate=True) * conv(c0, c1, dwv_ref, dbv_ref,
                                                                                        ubuf_v)
        abuf[:, c0:c1] = a.astype(BF16)
        if (n + 1) % DOWN_GROUP == 0 or n + 1 == len(bounds):
            pending = (bounds[n // DOWN_GROUP * DOWN_GROUP][0], c1)
    down(*pending)

    for ubuf, car, tail_ref in ((ubuf_g, car_g, tg_ref), (ubuf_v, car_v, tv_ref)):
        last = ubuf[tm:tm + halo, :]
        car[c] = last
        tail_ref[0, 0] = last

    @pl.when(c == nc - 1)
    def _():
        y_ref[0] = x1_ref[0] + _rms(acc_ref[...], gpo_ref[...])


def _ffn(hn, x1, wup_bf, wdn_bf, dw, db, state, gpo, *, tm, stride, nc):
    n_seq, rows, _ = x1.shape
    halo = state.shape[1]
    ck = D_FF // nc
    row_spec = lambda width: pl.BlockSpec((1, tm, width), lambda s, i, c: (s, i, 0))
    mode = dict(pipeline_mode=pl.Buffered(1)) if nc == 1 else {}
    gate_cols = lambda shape, **kw: pl.BlockSpec(shape, lambda s, i, c: (0, c), **kw)
    val_cols = lambda shape, **kw: pl.BlockSpec(shape, lambda s, i, c: (0, c + nc), **kw)
    return pl.pallas_call(
        functools.partial(_ffn_kernel, tm=tm, stride=stride, halo=halo, nc=nc),
        grid=(n_seq, rows // tm, nc),
        in_specs=[row_spec(D_MODEL), row_spec(D_MODEL),
                  gate_cols((D_MODEL, ck), **mode), val_cols((D_MODEL, ck), **mode),
                  pl.BlockSpec((ck, D_MODEL), lambda s, i, c: (c, 0), **mode),
                  gate_cols((FFN_KERNEL, ck)), val_cols((FFN_KERNEL, ck)),
                  gate_cols((1, ck)), val_cols((1, ck)),
                  pl.BlockSpec((1, halo, ck), lambda s, i, c: (s, 0, c)),
                  pl.BlockSpec((1, halo, ck), lambda s, i, c: (s, 0, c + nc)),
                  pl.BlockSpec((1, D_MODEL), lambda s, i, c: (0, 0))],
        out_specs=[row_spec(D_MODEL),
                   pl.BlockSpec((1, 1, halo, ck), lambda s, i, c: (s, i, 0, c)),
                   pl.BlockSpec((1, 1, halo, ck), lambda s, i, c: (s, i, 0, c))],
        out_shape=[jax.ShapeDtypeStruct((n_seq, rows, D_MODEL), F32),
                   jax.ShapeDtypeStruct((n_seq, rows // tm, halo, D_FF), F32),
                   jax.ShapeDtypeStruct((n_seq, rows // tm, halo, D_FF), F32)],
        scratch_shapes=[pltpu.VMEM((halo + tm, ck), F32), pltpu.VMEM((halo + tm, ck), F32),
                        pltpu.VMEM((nc, halo, ck), F32), pltpu.VMEM((nc, halo, ck), F32),
                        pltpu.VMEM((tm, D_MODEL), F32), pltpu.VMEM((tm, ck), BF16)],
        compiler_params=pltpu.CompilerParams(dimension_semantics=("arbitrary",) * 3,
                                             vmem_limit_bytes=VMEM_LIMIT),
        name="ffn_s%d" % stride,
    )(hn, x1, wup_bf, wup_bf, wdn_bf, dw, dw, db, db, state, state, gpo)


def _tile(rows, want):
    return want if rows % want == 0 else rows


def kernel(x_prompt, x_sample, cache_k, cache_v, state_conv, state_ffn, page_table, rel_bias, g_pre_mix, w_in,
           conv_dw_w, conv_dw_b, conv_ln_g, conv_ln_b, w_out, g_post_mix, g_pre_ffn, w_ffn_up, ffn_dw_w,
           ffn_dw_b, w_ffn_down, g_post_ffn):
    depth = w_in.shape[0]
    assert depth == 1, "single-layer trunk"
    bp, seq, _ = x_prompt.shape
    bs, n_new, _ = x_sample.shape
    n_pool, page = cache_k.shape[1], cache_k.shape[2]
    n_pages = page_table.shape[1]
    assert seq % MOBA_BLOCK == 0 and page == LANES and (n_pages * page) % MOBA_BLOCK == 0
    assert n_new == SUBLANES and n_pages * page // MOBA_BLOCK <= LANES

    w_in_bf = w_in[0].astype(BF16)
    w_out_bf = w_out[0].astype(BF16)
    w_up_bf = w_ffn_up[0].astype(BF16)
    w_dn_bf = w_ffn_down[0].astype(BF16)
    cw, cb = conv_dw_w[0], conv_dw_b
    near_t, near_s = _bias_tiles(rel_bias)
    conv_halo_p = 32
    ffn_halo_p = SUBLANES

    tm_p = _tile(seq, 512)
    qt, kt, vt, kbf, vtb, kmean, glu_p = _proj(x_prompt, g_pre_mix, w_in_bf, prompt=True, tm=tm_p)
    att_p = _attn_prompt(qt, kbf, vtb, kmean.reshape(bp, seq // MOBA_BLOCK, ATT_WIDTH), near_t)
    x1_p, hn_p = _mix(att_p, glu_p, x_prompt, jnp.zeros((bp, conv_halo_p, CONV_WIDTH), F32), cw, cb,
                      conv_ln_g, conv_ln_b, w_out_bf, g_post_mix, g_pre_ffn, tm=tm_p, stride=1)
    y_prompt, tail_g, tail_v = _ffn(hn_p, x1_p, w_up_bf, w_dn_bf, ffn_dw_w[0], ffn_dw_b,
                                    jnp.zeros((bp, ffn_halo_p, 2 * D_FF), F32), g_post_ffn,
                                    tm=tm_p, stride=1, nc=1)
    k_prompt = kt.reshape(1, bp, N_HEADS, HEAD_DIM, seq).transpose(0, 1, 4, 2, 3)
    v_prompt = vt.reshape(1, bp, N_HEADS, HEAD_DIM, seq).transpose(0, 1, 4, 2, 3)
    conv_prompt = glu_p[:, seq - (CONV_KERNEL - 1):, :][None]
    ffn_prompt = jnp.concatenate([tail_g[:, -1], tail_v[:, -1]],
                                 axis=-1)[:, ffn_halo_p - (FFN_KERNEL - 1):, :][None]

    rows_s = n_new * bs
    to_tb = lambda a: a.transpose(1, 0, 2).reshape(1, rows_s, a.shape[-1])
    to_bt = lambda a: a.reshape(n_new, bs, a.shape[-1]).transpose(1, 0, 2)
    xs = to_tb(x_sample)
    q_s, k_s, v_s, glu_s = _proj(xs, g_pre_mix, w_in_bf, prompt=False, tm=rows_s)
    q_b, k_b, v_b = to_bt(q_s), to_bt(k_s), to_bt(v_s)
    cache_kt = cache_k[0].transpose(0, 2, 3, 1).reshape(n_pool, ATT_WIDTH, page)
    cache_vt = cache_v[0].transpose(0, 2, 3, 1).reshape(n_pool, ATT_WIDTH, page)
    att_b = _attn_sample(page_table, rel_bias, q_b, k_b, v_b, near_s, cache_kt, cache_vt,
                         nbuf=min(16, n_pages))
    conv_state = state_conv[0].transpose(1, 0, 2).reshape(1, (CONV_KERNEL - 1) * bs, CONV_WIDTH)
    x1_s, hn_s = _mix(to_tb(att_b), glu_s, xs, conv_state, cw, cb, conv_ln_g, conv_ln_b, w_out_bf,
                      g_post_mix, g_pre_ffn, tm=rows_s, stride=bs)
    ffn_state = state_ffn[0].transpose(1, 0, 2).reshape(1, (FFN_KERNEL - 1) * bs, 2 * D_FF)
    y_s, tail_gs, tail_vs = _ffn(hn_s, x1_s, w_up_bf, w_dn_bf, ffn_dw_w[0], ffn_dw_b, ffn_state, g_post_ffn,
                                 tm=rows_s, stride=bs, nc=2)
    y_sample = to_bt(y_s)
    k_sample = k_b.reshape(1, bs, n_new, N_HEADS, HEAD_DIM)
    v_sample = v_b.reshape(1, bs, n_new, N_HEADS, HEAD_DIM)
    conv_all = jnp.concatenate([conv_state[0], glu_s[0]], axis=0)[n_new * bs:]
    conv_sample = conv_all.reshape(CONV_KERNEL - 1, bs, CONV_WIDTH).transpose(1, 0, 2)[None]
    ffn_sample = jnp.concatenate([tail_gs[:, -1], tail_vs[:, -1]], axis=-1).reshape(
        FFN_KERNEL - 1, bs, 2 * D_FF).transpose(1, 0, 2)[None]

    return (y_prompt, y_sample, k_prompt, v_prompt, conv_prompt, ffn_prompt,
            k_sample, v_sample, conv_sample, ffn_sample)
```
